```python
import jax
import jax.numpy as jnp
from jax import lax
import numpy as np

D_MODEL = 1024
BATCH = 32
SEQ = 2048
DEPTH = 1
DEC_BATCH = 8
DEC_SEQ = 8192
PAST_LEN = 128

GLA_HEADS = 4
GLA_DK = 128
GLA_DV = 256
GLA_KEY = GLA_HEADS * GLA_DK
GLA_VAL = GLA_HEADS * GLA_DV
GLA_GATE_RANK = 16
GLA_GATE_NORMALIZER = 16.0
GLA_CHUNK = 64

SWA_HEADS = 8
SWA_KV_HEADS = 2
SWA_GROUP = SWA_HEADS // SWA_KV_HEADS
SWA_HEAD_DIM = 128
SWA_WINDOW = 128
SWA_BLOCK = 128
ROPE_THETA = 500000.0
ROPE_DIM = SWA_HEAD_DIM // 4

N_EXPERTS = 32
TOP_K = 4
D_EXPERT = D_MODEL
SWIGLU_LIMIT = 7.0
SWIGLU_ALPHA = 1.702
MOE_BLOCK = 256

DEEPNORM_ALPHA = (2.0 * DEPTH) ** 0.25
DEEPNORM_BETA = (8.0 * DEPTH) ** -0.25
LN_EPS = 1e-5
RMS_EPS = 1e-6

IN_SIZES = (GLA_KEY, GLA_KEY, GLA_VAL, GLA_VAL, 2 * GLA_GATE_RANK,
            SWA_HEADS * SWA_HEAD_DIM, SWA_KV_HEADS * SWA_HEAD_DIM, SWA_KV_HEADS * SWA_HEAD_DIM,
            D_MODEL, D_MODEL)
IN_DIM = sum(IN_SIZES)

kernel_name = "hybrid_gla_swa_moe_encoder"


def layer_norm(x, g, b):
    xf = x.astype(jnp.float32)
    mu = jnp.mean(xf, axis=-1, keepdims=True)
    var = jnp.mean(jnp.square(xf - mu), axis=-1, keepdims=True)
    y = (xf - mu) * lax.rsqrt(var + LN_EPS) * g.astype(jnp.float32) + b.astype(jnp.float32)
    return y.astype(x.dtype)


def split_columns(p):
    idx = []
    acc = 0
    for s in IN_SIZES[:-1]:
        acc += s
        idx.append(acc)
    return jnp.split(p, idx, axis=-1)


def gla_scan_chunked(q, k, v, log_a, strict):
    c = q.shape[-2]
    b = jnp.cumsum(log_a, axis=-2)
    b_ref = b[..., c // 2:c // 2 + 1, :]
    q_in = q * jnp.exp(b - b_ref)
    k_in = k * jnp.exp(b_ref - b)
    scores = jnp.einsum("bhnid,bhnjd->bhnij", q_in, k_in)
    pos = jnp.arange(c)
    if strict:
        mask = pos[None, :] < pos[:, None]
    else:
        mask = pos[None, :] <= pos[:, None]
    scores = jnp.where(mask, scores, 0.0)
    o_intra = jnp.einsum("bhnij,bhnjv->bhniv", scores, v)
    b_last = b[..., -1:, :]
    k_state = k * jnp.exp(b_last - b)
    d_state = jnp.einsum("bhncd,bhncv->nbhdv", k_state, v)
    decay = jnp.moveaxis(jnp.exp(b_last[..., 0, :]), 2, 0)

    def step(state, inp):
        dec, ds = inp
        return state * dec[..., None] + ds, state

    bsz, h, _, _, dk = q.shape
    dv = v.shape[-1]
    s0 = jnp.zeros((bsz, h, dk, dv), jnp.float32)
    _, s_prev = lax.scan(step, s0, (decay, d_state))
    o_inter = jnp.einsum("bhncd,nbhdv->bhncv", q * jnp.exp(b), s_prev)
    return o_intra + o_inter


def gla_branch(q, k, v, r, gate_lr, w_ga2, b_ga, norm_w):
    bsz, seq, _ = q.shape
    n_chunks = seq // GLA_CHUNK
    f32 = jnp.float32

    def to_chunks(t, dh):
        return t.astype(f32).reshape(bsz, n_chunks, GLA_CHUNK, GLA_HEADS, dh).transpose(0, 3, 1, 2, 4)

    qh = to_chunks(q, GLA_DK) * (GLA_DK ** -0.5)
    kh = to_chunks(k, GLA_DK)
    vh = to_chunks(v, GLA_DV)
    lr = gate_lr.astype(f32).reshape(bsz, seq, 2, GLA_GATE_RANK)
    z = jnp.einsum("bsgr,grk->gbsk", lr, w_ga2.astype(f32)) + b_ga.astype(f32)[:, None, None, :]
    log_a = jax.nn.log_sigmoid(z) / GLA_GATE_NORMALIZER
    la_fwd = to_chunks(log_a[0], GLA_DK)
    la_bwd = to_chunks(log_a[1], GLA_DK)
    rev = lambda t: t[:, :, ::-1, ::-1, :]
    o_fwd = gla_scan_chunked(qh, kh, vh, la_fwd, strict=False)
    o_bwd = rev(gla_scan_chunked(rev(qh), rev(kh), rev(vh), rev(la_bwd), strict=True))
    o = o_fwd + o_bwd
    o = o * lax.rsqrt(jnp.mean(jnp.square(o), axis=-1, keepdims=True) + RMS_EPS)
    o = o.transpose(0, 2, 3, 1, 4).reshape(bsz, seq, GLA_VAL)
    out = o * norm_w.astype(f32) * jax.nn.silu(r.astype(f32))
    return out.astype(q.dtype)


def partial_rope(t, cos, sin):
    half = ROPE_DIM // 2
    c = cos[None, :, None, :]
    s = sin[None, :, None, :]
    t1 = t[..., :half]
    t2 = t[..., half:ROPE_DIM]
    return jnp.concatenate([t1 * c - t2 * s, t2 * c + t1 * s, t[..., ROPE_DIM:]], axis=-1)


def swa_branch(q, k, v, sinks):
    bsz, seq, _ = q.shape
    nb = seq // SWA_BLOCK
    f32 = jnp.float32
    pos = jnp.arange(seq, dtype=f32)
    inv_freq = ROPE_THETA ** (-jnp.arange(0, ROPE_DIM, 2, dtype=f32) / ROPE_DIM)
    ang = pos[:, None] * inv_freq[None, :]
    cos, sin = jnp.cos(ang), jnp.sin(ang)
    qh = partial_rope(q.astype(f32).reshape(bsz, seq, SWA_HEADS, SWA_HEAD_DIM), cos, sin)
    kh = partial_rope(k.astype(f32).reshape(bsz, seq, SWA_KV_HEADS, SWA_HEAD_DIM), cos, sin)
    vh = v.astype(f32).reshape(bsz, seq, SWA_KV_HEADS, SWA_HEAD_DIM)
    qb = qh.reshape(bsz, nb, SWA_BLOCK, SWA_KV_HEADS, SWA_GROUP, SWA_HEAD_DIM)

    def kv_blocks(t):
        tp = jnp.pad(t, ((0, 0), (SWA_BLOCK, SWA_BLOCK), (0, 0), (0, 0)))
        tp = tp.reshape(bsz, nb + 2, SWA_BLOCK, SWA_KV_HEADS, SWA_HEAD_DIM)
        return jnp.concatenate([tp[:, :-2], tp[:, 1:-1], tp[:, 2:]], axis=2)

    kb = kv_blocks(kh)
    vb = kv_blocks(vh)
    scores = jnp.einsum("bnqkgd,bnckd->bnkgqc", qb, kb) * (SWA_HEAD_DIM ** -0.5)
    blk = jnp.arange(nb)[:, None, None]
    qpos = blk * SWA_BLOCK + jnp.arange(SWA_BLOCK)[None, :, None]
    kpos = (blk - 1) * SWA_BLOCK + jnp.arange(3 * SWA_BLOCK)[None, None, :]
    valid = (jnp.abs(kpos - qpos) <= SWA_WINDOW) & (kpos >= 0) & (kpos < seq)
    scores = jnp.where(valid[None, :, None, None], scores, -jnp.inf)
    sink = jnp.broadcast_to(sinks.astype(f32).reshape(1, 1, SWA_KV_HEADS, SWA_GROUP, 1, 1),
                            scores.shape[:-1] + (1,))
    probs = jax.nn.softmax(jnp.concatenate([scores, sink], axis=-1), axis=-1)[..., :-1]
    out = jnp.einsum("bnkgqc,bnckd->bnqkgd", probs, vb)
    return out.reshape(bsz, seq, SWA_HEADS * SWA_HEAD_DIM).astype(q.dtype)


def moe_ffn(x, router_w, router_b, w_gate_up, b_gate_up, w_down, b_down):
    bsz, seq, d = x.shape
    n_tok = bsz * seq
    xt = x.reshape(n_tok, d)
    logits = (xt @ router_w + router_b).astype(jnp.float32)
    top_val, top_idx = lax.top_k(logits, TOP_K)
    gates = jax.nn.softmax(top_val, axis=-1)
    n_asg = n_tok * TOP_K
    e = top_idx.reshape(n_asg)
    tok = jnp.repeat(jnp.arange(n_tok, dtype=jnp.int32), TOP_K)
    w = gates.reshape(n_asg)
    order = jnp.argsort(e)
    e_s, tok_s, w_s = e[order], tok[order], w[order]
    counts = jnp.bincount(e, length=N_EXPERTS)
    starts = jnp.cumsum(counts) - counts
    padded = (counts + MOE_BLOCK - 1) // MOE_BLOCK * MOE_BLOCK
    pad_ends = jnp.cumsum(padded)
    pad_starts = pad_ends - padded
    dest = pad_starts[e_s] + jnp.arange(n_asg) - starts[e_s]
    n_blocks = -(-n_asg // MOE_BLOCK) + N_EXPERTS
    n_rows = n_blocks * MOE_BLOCK
    tok_pad = jnp.full((n_rows,), n_tok, jnp.int32).at[dest].set(tok_s)
    w_pad = jnp.zeros((n_rows,), jnp.float32).at[dest].set(w_s)
    block_expert = jnp.minimum(
        jnp.searchsorted(pad_ends, jnp.arange(n_blocks) * MOE_BLOCK, side="right"), N_EXPERTS - 1)
    x_ext = jnp.concatenate([xt, jnp.zeros((1, d), xt.dtype)], axis=0)

    def expert_block(args):
        idx, ex = args
        xb = x_ext[idx]
        hgu = xb @ w_gate_up[ex] + b_gate_up[ex]
        gate = jnp.minimum(hgu[:, :D_EXPERT], SWIGLU_LIMIT)
        up = jnp.clip(hgu[:, D_EXPERT:], -SWIGLU_LIMIT, SWIGLU_LIMIT)
        act = (up + 1.0) * gate * jax.nn.sigmoid(gate * SWIGLU_ALPHA)
        return act @ w_down[ex] + b_down[ex]

    ys = lax.map(expert_block, (tok_pad.reshape(n_blocks, MOE_BLOCK), block_expert))
    ys = ys.reshape(n_rows, d) * w_pad[:, None].astype(ys.dtype)
    out = jnp.zeros((n_tok + 1, d), ys.dtype).at[tok_pad].add(ys)[:n_tok]
    return out.reshape(bsz, seq, d).astype(x.dtype)


def encoder_layer(x, w_in, gla_gate_w2, gla_gate_b, gla_norm_w, swa_sinks, w_o, ln1_g, ln1_b,
                  router_w, router_b, w_gate_up, b_gate_up, w_down, b_down, ln2_g, ln2_b):
    p = x @ w_in
    (gq, gk, gv, gr, glr, sq, sk, sv, ga, gb) = split_columns(p)
    o_a = gla_branch(gq, gk, gv, gr, glr, gla_gate_w2, gla_gate_b, gla_norm_w)
    o_b = swa_branch(sq, sk, sv, swa_sinks)
    h = jax.nn.sigmoid(ga) * o_a + jax.nn.sigmoid(gb) * o_b
    x = layer_norm(DEEPNORM_ALPHA * x + h @ w_o, ln1_g, ln1_b)
    f = moe_ffn(x, router_w, router_b, w_gate_up, b_gate_up, w_down, b_down)
    x = layer_norm(DEEPNORM_ALPHA * x + f, ln2_g, ln2_b)
    return x


def setup_inputs(seed: int = 0) -> dict:
    key = jax.random.key(seed)
    ks = jax.random.split(key, 20)
    f32 = jnp.float32
    L = DEPTH

    def nrm(k, shape, scale):
        return jax.random.normal(k, shape, f32) * scale

    return {
        "x_prompt": nrm(ks[0], (BATCH, SEQ, D_MODEL), 1.0),
        "x_sample": nrm(ks[1], (DEC_BATCH, DEC_SEQ, D_MODEL), 1.0),
        "w_in": nrm(ks[2], (L, D_MODEL, IN_DIM), D_MODEL ** -0.5),
        "gla_gate_w2": nrm(ks[3], (L, 2, GLA_GATE_RANK, GLA_KEY), GLA_GATE_RANK ** -0.5),
        "gla_gate_b": nrm(ks[4], (L, 2, GLA_KEY), 0.1),
        "gla_norm_w": 1.0 + nrm(ks[5], (L, GLA_VAL), 0.02),
        "swa_sinks": nrm(ks[6], (L, SWA_HEADS), 0.5),
        "w_o": nrm(ks[7], (L, D_MODEL, D_MODEL), D_MODEL ** -0.5 * DEEPNORM_BETA),
        "ln1_g": 1.0 + nrm(ks[8], (L, D_MODEL), 0.02),
        "ln1_b": nrm(ks[9], (L, D_MODEL), 0.02),
        "router_w": nrm(ks[10], (L, D_MODEL, N_EXPERTS), D_MODEL ** -0.5),
        "router_b": nrm(ks[11], (L, N_EXPERTS), 0.01),
        "w_gate_up": nrm(ks[12], (L, N_EXPERTS, D_MODEL, 2 * D_EXPERT), D_MODEL ** -0.5),
        "b_gate_up": nrm(ks[13], (L, N_EXPERTS, 2 * D_EXPERT), 0.02),
        "w_down": nrm(ks[14], (L, N_EXPERTS, D_EXPERT, D_MODEL), D_EXPERT ** -0.5 * DEEPNORM_BETA),
        "b_down": nrm(ks[15], (L, N_EXPERTS, D_MODEL), 0.02),
        "ln2_g": 1.0 + nrm(ks[16], (L, D_MODEL), 0.02),
        "ln2_b": nrm(ks[17], (L, D_MODEL), 0.02),
    }


def reference(x_prompt, x_sample, w_in, gla_gate_w2, gla_gate_b, gla_norm_w, swa_sinks, w_o,
              ln1_g, ln1_b, router_w, router_b, w_gate_up, b_gate_up, w_down, b_down, ln2_g, ln2_b):
    def trunk(x):
        for l in range(DEPTH):
            x = encoder_layer(x, w_in[l], gla_gate_w2[l], gla_gate_b[l], gla_norm_w[l], swa_sinks[l],
                              w_o[l], ln1_g[l], ln1_b[l], router_w[l], router_b[l], w_gate_up[l],
                              b_gate_up[l], w_down[l], b_down[l], ln2_g[l], ln2_b[l])
        return x

    y_prompt = trunk(x_prompt)
    y_sample = trunk(x_sample)
    return (y_prompt, y_sample)
```

```python
import functools

import jax
import jax.numpy as jnp
from jax import lax
from jax.experimental import pallas as pl
from jax.experimental.pallas import tpu as pltpu
from jax.experimental.pallas import tpu_sc as plsc

F32 = jnp.float32
BF16 = jnp.bfloat16
I32 = jnp.int32
U32 = jnp.uint32

D_MODEL = 1024
GLA_HEADS = 4
GLA_DK = 128
GLA_DV = 256
GLA_KEY = GLA_HEADS * GLA_DK
GLA_VAL = GLA_HEADS * GLA_DV
GLA_RANK = 16
GLA_NORMALIZER = 16.0
GLA_CHUNK = 64
SWA_HEADS = 8
SWA_KV = 2
SWA_GROUP = SWA_HEADS // SWA_KV
SWA_HD = 128
SWA_WINDOW = 128
ROPE_THETA = 500000.0
ROPE_DIM = SWA_HD // 4
N_EXPERTS = 32
TOP_K = 4
D_EXPERT = D_MODEL
SWIGLU_LIMIT = 7.0
SWIGLU_ALPHA = 1.702
DEEPNORM_ALPHA = 2.0 ** 0.25
LN_EPS = 1e-5
RMS_EPS = 1e-6

LANES = 128
PA_W = 2 * GLA_KEY + 2 * GLA_VAL
PB_W = 2 * D_MODEL + SWA_HEADS * SWA_HD + 2 * SWA_KV * SWA_HD
HALF = D_MODEL // 2

PROJ_TM = 512
GLA_TS = 512
SWA_TQ = 512
MOE_BM = 512
SC_CHUNK = 64
VMEM_LIMIT = 56 * 1024 * 1024


def _cparams(sem):
    return pltpu.CompilerParams(dimension_semantics=sem, vmem_limit_bytes=VMEM_LIMIT)


def _split2(a):
    hi = a.astype(BF16)
    lo = (a - hi.astype(F32)).astype(BF16)
    return hi, lo


def _split3(a):
    hi = a.astype(BF16)
    r = a - hi.astype(F32)
    mid = r.astype(BF16)
    lo = (r - mid.astype(F32)).astype(BF16)
    return hi, mid, lo


def _dot(a, b):
    return jnp.dot(a, b, preferred_element_type=F32)


def _dot_nt(a, b):
    return lax.dot_general(a, b, (((1,), (1,)), ((), ())), preferred_element_type=F32)


def _dot_f32ish(a, b):
    ah, al = _split2(a)
    bh, bl = _split2(b)
    return _dot(ah, bh) + _dot(al, bh) + _dot(ah, bl)


def _pack_bf16_pair(lo_f32, hi_f32):
    lo = lax.bitcast_convert_type(lo_f32.astype(BF16).astype(F32), U32)
    hi = lax.bitcast_convert_type(hi_f32.astype(BF16).astype(F32), U32)
    return (hi & jnp.uint32(0xFFFF0000)) | (lo >> 16)


def _unpack_bf16_pair(w):
    lo = lax.bitcast_convert_type(w << 16, F32)
    hi = lax.bitcast_convert_type(w & jnp.uint32(0xFFFF0000), F32)
    return lo, hi


def _in_proj_kernel(x_ref, wa_ref, wl_ref, wb_ref, cos_ref, sina_ref, sinb_ref,
                    pa_ref, pl_ref, pb_ref):
    xb = x_ref[...].astype(BF16)
    step = 512
    for c0 in range(0, PA_W, step):
        r = _dot(xb, wa_ref[:, c0:c0 + step])
        if c0 < GLA_KEY:
            r = r * (GLA_DK ** -0.5)
        pa_ref[:, c0:c0 + step] = r.astype(BF16)
    pl_ref[...] = _dot(xb, wl_ref[...])
    cos = cos_ref[...]
    sina = sina_ref[...]
    sinb = sinb_ref[...]
    rope0 = 2 * D_MODEL
    rope_end = rope0 + (SWA_HEADS + SWA_KV) * SWA_HD
    for c0 in range(0, PB_W, step):
        r = _dot(xb, wb_ref[:, c0:c0 + step])
        if c0 + step > rope0 and c0 < rope_end:
            parts = []
            for j in range(step // SWA_HD):
                t = r[:, j * SWA_HD:(j + 1) * SWA_HD]
                col = c0 + j * SWA_HD
                if rope0 <= col < rope_end:
                    t = (t * cos + pltpu.roll(t, SWA_HD - ROPE_DIM // 2, 1) * sina
                         + pltpu.roll(t, ROPE_DIM // 2, 1) * sinb)
                if rope0 <= col < rope0 + SWA_HEADS * SWA_HD:
                    t = t * (SWA_HD ** -0.5)
                parts.append(t)
            r = jnp.concatenate(parts, axis=1)
        pb_ref[:, c0:c0 + step] = r.astype(BF16)


def _in_proj(x2d, wa, wl, wb, cos, sina, sinb, seq):
    t = x2d.shape[0]
    tm = PROJ_TM
    nseq = seq // tm
    const = lambda i: (0, 0)
    return pl.pallas_call(
        _in_proj_kernel,
        grid=(t // tm,),
        in_specs=[
            pl.BlockSpec((tm, D_MODEL), lambda i: (i, 0)),
            pl.BlockSpec((D_MODEL, PA_W), const),
            pl.BlockSpec((D_MODEL, LANES), const),
            pl.BlockSpec((D_MODEL, PB_W), const),
            pl.BlockSpec((tm, LANES), lambda i: (i % nseq, 0)),
            pl.BlockSpec((tm, LANES), lambda i: (i % nseq, 0)),
            pl.BlockSpec((tm, LANES), lambda i: (i % nseq, 0)),
        ],
        out_specs=[
            pl.BlockSpec((tm, PA_W), lambda i: (i, 0)),
            pl.BlockSpec((tm, LANES), lambda i: (i, 0)),
            pl.BlockSpec((tm, PB_W), lambda i: (i, 0)),
        ],
        out_shape=[
            jax.ShapeDtypeStruct((t, PA_W), BF16),
            jax.ShapeDtypeStruct((t, LANES), F32),
            jax.ShapeDtypeStruct((t, PB_W), BF16),
        ],
        compiler_params=_cparams(("parallel",)),
        name="in_proj",
    )(x2d, wa, wl, wb, cos, sina, sinb)


def _gla_kernel(q_ref, k_ref, v_ref, r_ref, lr_ref, w2_ref, bga_ref, nw_ref, o_ref,
                state_ref, ofwd_ref, qb_ref, ds_ref, dec_ref, oi_ref, *, ts, nt):
    ph = pl.program_id(2)
    t = pl.program_id(3)
    fwd = ph == 0
    c = GLA_CHUNK
    nc = ts // c
    tile = jnp.where(fwd, t, nt - 1 - t)

    @pl.when(t == 0)
    def _():
        state_ref[...] = jnp.zeros_like(state_ref)

    q = q_ref[0].astype(F32)
    k = k_ref[0].astype(F32)
    v = v_ref[0]
    z = _dot_f32ish(lr_ref[0], w2_ref[0]) + bga_ref[0]
    la = (jnp.minimum(z, 0.0) - jnp.log1p(jnp.exp(-jnp.abs(z)))) * (1.0 / GLA_NORMALIZER)

    row = lax.broadcasted_iota(I32, (c, c), 0)
    col = lax.broadcasted_iota(I32, (c, c), 1)
    ahead = (col - row) * jnp.where(fwd, 1, -1)
    tri = (ahead <= 0).astype(BF16)
    keep = ahead <= jnp.where(fwd, 0, -1)
    b_parts = []
    for ci in range(nc):
        hi, mid, lo = _split3(la[ci * c:(ci + 1) * c])
        b_parts.append(_dot(tri, hi) + _dot(tri, mid) + _dot(tri, lo))
    b3 = jnp.stack(b_parts)
    b_mid = jnp.where(fwd, b3[:, c // 2:c // 2 + 1], b3[:, c // 2 - 1:c // 2])
    b_end = jnp.where(fwd, b3[:, c - 1:c], b3[:, 0:1])
    q3 = q.reshape(nc, c, GLA_DK)
    k3 = k.reshape(nc, c, GLA_DK)
    v3 = v.reshape(nc, c, GLA_DV)
    q_in = (q3 * jnp.exp(b3 - b_mid)).astype(BF16)
    k_in = (k3 * jnp.exp(b_mid - b3)).astype(BF16)
    k_st = (k3 * jnp.exp(b_end - b3)).astype(BF16)
    qb_ref[...] = (q3 * jnp.exp(b3)).astype(BF16)
    dec_ref[...] = jnp.exp(b_end)
    scores = jnp.einsum("cid,cjd->cij", q_in, k_in, preferred_element_type=F32)
    scores = jnp.where(keep[None], scores, 0.0).astype(BF16)
    o_intra = jnp.einsum("cij,cjv->civ", scores, v3, preferred_element_type=F32)
    ds_ref[...] = jnp.einsum("cjv,cjd->cvd", v3, k_st, preferred_element_type=F32)

    def chunk_step(i, carry):
        ci = jnp.where(fwd, i, nc - 1 - i)
        st = state_ref[...]
        oi_ref[ci] = _dot_nt(qb_ref[ci], st.astype(BF16))
        state_ref[...] = st * dec_ref[ci] + ds_ref[ci]
        return carry

    lax.fori_loop(0, nc, chunk_step, 0)
    o = (o_intra + oi_ref[...]).reshape(ts, GLA_DV)
    off = pl.multiple_of(tile * ts, ts)

    @pl.when(fwd)
    def _():
        ofwd_ref[pl.ds(off, ts), :] = o

    @pl.when(jnp.logical_not(fwd))
    def _():
        tot = o + ofwd_ref[pl.ds(off, ts), :]
        tot = tot * lax.rsqrt(jnp.mean(tot * tot, axis=-1, keepdims=True) + RMS_EPS)
        r = r_ref[0].astype(F32)
        o_ref[0] = (tot * nw_ref[...] * (r * jax.nn.sigmoid(r))).astype(BF16)


def _gla(pa, plr, w2p, bga, nw, bsz, seq):
    ts = min(GLA_TS, seq)
    nt = seq // ts
    pa3 = pa.reshape(bsz, seq, PA_W)
    lr3 = plr.reshape(bsz, seq, LANES)
    tile = lambda p, t: t + p * (nt - 1 - 2 * t)
    kq = GLA_KEY // GLA_DK
    kv = 2 * GLA_KEY // GLA_DV
    kr = kv + GLA_VAL // GLA_DV
    kern = functools.partial(_gla_kernel, ts=ts, nt=nt)
    return pl.pallas_call(
        kern,
        grid=(bsz, GLA_HEADS, 2, nt),
        in_specs=[
            pl.BlockSpec((1, ts, GLA_DK), lambda b, h, p, t: (b, tile(p, t), h)),
            pl.BlockSpec((1, ts, GLA_DK), lambda b, h, p, t: (b, tile(p, t), kq + h)),
            pl.BlockSpec((1, ts, GLA_DV), lambda b, h, p, t: (b, tile(p, t), kv + h)),
            pl.BlockSpec((1, ts, GLA_DV), lambda b, h, p, t: (b, nt - 1 - p * t, kr + h)),
            pl.BlockSpec((1, ts, LANES), lambda b, h, p, t: (b, tile(p, t), 0)),
            pl.BlockSpec((1, LANES, GLA_DK), lambda b, h, p, t: (p, 0, h)),
            pl.BlockSpec((1, 1, GLA_DK), lambda b, h, p, t: (p, 0, h)),
            pl.BlockSpec((1, GLA_DV), lambda b, h, p, t: (0, h)),
        ],
        out_specs=pl.BlockSpec((1, ts, GLA_DV), lambda b, h, p, t: (b, nt - 1 - p * t, h)),
        out_shape=jax.ShapeDtypeStruct((bsz, seq, GLA_VAL), BF16),
        scratch_shapes=[
            pltpu.VMEM((GLA_DV, GLA_DK), F32),
            pltpu.VMEM((seq, GLA_DV), F32),
            pltpu.VMEM((ts // GLA_CHUNK, GLA_CHUNK, GLA_DK), BF16),
            pltpu.VMEM((ts // GLA_CHUNK, GLA_DV, GLA_DK), F32),
            pltpu.VMEM((ts // GLA_CHUNK, 1, GLA_DK), F32),
            pltpu.VMEM((ts // GLA_CHUNK, GLA_CHUNK, GLA_DV), F32),
        ],
        compiler_params=_cparams(("parallel", "parallel", "arbitrary", "arbitrary")),
        name="gla",
    )(pa3, pa3, pa3, pa3, lr3, w2p, bga, nw)


def _swa_kernel(sink_ref, q_ref, kl_ref, kc_ref, kr_ref, vl_ref, vc_ref, vr_ref, o_ref, *, tq, seq):
    kvh = pl.program_id(1)
    n = pl.program_id(2)
    blk = SWA_WINDOW
    k_all = jnp.concatenate([kl_ref[0], kc_ref[0], kr_ref[0]], axis=0)
    v_all = jnp.concatenate([vl_ref[0], vc_ref[0], vr_ref[0]], axis=0)
    rows = SWA_GROUP * blk
    qi = lax.broadcasted_iota(I32, (rows, 3 * blk), 0) & (blk - 1)
    kj = lax.broadcasted_iota(I32, (rows, 3 * blk), 1) - blk
    band = jnp.abs(kj - qi) <= SWA_WINDOW
    head = lax.broadcasted_iota(I32, (rows, 1), 0) // blk
    sink = jnp.zeros((rows, 1), F32)
    for g in range(SWA_GROUP):
        sink = jnp.where(head == g, sink_ref[kvh * SWA_GROUP + g], sink)
    for jq in range(tq // blk):
        q0 = n * tq + jq * blk
        kpos = kj + q0
        valid = band & (kpos >= 0) & (kpos < seq)
        qs = q_ref[0, jq * blk:(jq + 1) * blk, :]
        q4 = jnp.concatenate([qs[:, g * SWA_HD:(g + 1) * SWA_HD] for g in range(SWA_GROUP)], axis=0)
        keys = k_all[jq * blk:(jq + 3) * blk]
        vals = v_all[jq * blk:(jq + 3) * blk]
        s = jnp.where(valid, _dot_nt(q4, keys), -jnp.inf)
        m = jnp.maximum(jnp.max(s, axis=-1, keepdims=True), sink)
        p = jnp.exp(s - m)
        denom = jnp.sum(p, axis=-1, keepdims=True) + jnp.exp(sink - m)
        o = _dot(p.astype(BF16), vals) / denom
        for g in range(SWA_GROUP):
            o_ref[0, jq * blk:(jq + 1) * blk, g * SWA_HD:(g + 1) * SWA_HD] = (
                o[g * blk:(g + 1) * blk].astype(BF16))


def _swa(pb, sinks, bsz, seq):
    tq = min(SWA_TQ, seq)
    blk = SWA_WINDOW
    r = tq // blk
    nb = seq // blk
    pb3 = pb.reshape(bsz, seq, PB_W)
    gw = SWA_GROUP * SWA_HD
    cq = 2 * D_MODEL // gw
    ck = (2 * D_MODEL + SWA_HEADS * SWA_HD) // SWA_HD
    cv = ck + SWA_KV
    left = lambda n: jnp.maximum(n * r - 1, 0)
    right = lambda n: jnp.minimum(n * r + r, nb - 1)
    kern = functools.partial(_swa_kernel, tq=tq, seq=seq)
    grid_spec = pltpu.PrefetchScalarGridSpec(
        num_scalar_prefetch=1,
        grid=(bsz, SWA_KV, seq // tq),
        in_specs=[
            pl.BlockSpec((1, tq, gw), lambda b, h, n, s: (b, n, cq + h)),
            pl.BlockSpec((1, blk, SWA_HD), lambda b, h, n, s: (b, left(n), ck + h)),
            pl.BlockSpec((1, tq, SWA_HD), lambda b, h, n, s: (b, n, ck + h)),
            pl.BlockSpec((1, blk, SWA_HD), lambda b, h, n, s: (b, right(n), ck + h)),
            pl.BlockSpec((1, blk, SWA_HD), lambda b, h, n, s: (b, left(n), cv + h)),
            pl.BlockSpec((1, tq, SWA_HD), lambda b, h, n, s: (b, n, cv + h)),
            pl.BlockSpec((1, blk, SWA_HD), lambda b, h, n, s: (b, right(n), cv + h)),
        ],
        out_specs=pl.BlockSpec((1, tq, gw), lambda b, h, n, s: (b, n, h)),
    )
    return pl.pallas_call(
        kern,
        grid_spec=grid_spec,
        out_shape=jax.ShapeDtypeStruct((bsz, seq, SWA_HEADS * SWA_HD), BF16),
        compiler_params=_cparams(("parallel", "parallel", "parallel")),
        name="swa",
    )(sinks, pb3, pb3, pb3, pb3, pb3, pb3, pb3)


def _layer_norm(y, g, b):
    mu = jnp.mean(y, axis=-1, keepdims=True)
    d = y - mu
    var = jnp.mean(d * d, axis=-1, keepdims=True)
    return d * lax.rsqrt(var + LN_EPS) * g + b


def _merge_kernel(oa_ref, ob_ref, ga_ref, gb_ref, x_ref, wo_ref, g_ref, b_ref, rw_ref, rb_ref,
                  x1_ref, xp_ref, idx_ref, gate_ref, rank_ref, cnt_ref, run_ref, *, tm):
    @pl.when(pl.program_id(0) == 0)
    def _():
        run_ref[...] = jnp.zeros_like(run_ref)

    h = (jax.nn.sigmoid(ga_ref[...].astype(F32)) * oa_ref[...].astype(F32)
         + jax.nn.sigmoid(gb_ref[...].astype(F32)) * ob_ref[...].astype(F32))
    y = DEEPNORM_ALPHA * x_ref[...] + _dot(h.astype(BF16), wo_ref[...])
    x1 = _layer_norm(y, g_ref[...], b_ref[...])
    x1_ref[...] = x1
    xp_ref[...] = _pack_bf16_pair(x1[:, :HALF], x1[:, HALF:])

    logits = _dot_f32ish(x1, rw_ref[...]) + rb_ref[...]
    lane_i = lax.broadcasted_iota(I32, (tm, LANES), 1)
    lane = lane_i.astype(F32)
    work = logits
    vals, idxs, hots = [], [], []
    for _ in range(TOP_K):
        m = jnp.max(work, axis=-1, keepdims=True)
        sel = jnp.min(jnp.where(work == m, lane, float(LANES)), axis=-1, keepdims=True)
        hot = lane == sel
        vals.append(m)
        idxs.append(sel)
        hots.append(hot)
        work = jnp.where(hot, -jnp.inf, work)
    e = [jnp.exp(v - vals[0]) for v in vals]
    tot = e[0] + e[1] + e[2] + e[3]

    def spread(cols):
        out = jnp.zeros((tm, LANES), F32)
        for kk, col in enumerate(cols):
            out = jnp.where(lane_i == kk, col, out)
        return out

    gate_ref[...] = spread([ei / tot for ei in e])
    idx_ref[...] = spread(idxs).astype(I32)

    member = (hots[0] | hots[1] | hots[2] | hots[3])
    member_f = member.astype(F32)
    r_i = lax.broadcasted_iota(I32, (tm, tm), 0)
    c_i = lax.broadcasted_iota(I32, (tm, tm), 1)
    before = (c_i < r_i).astype(BF16)
    cum = _dot(before, member.astype(BF16)) + run_ref[...]
    ranks = [jnp.sum(jnp.where(hot, cum, 0.0), axis=-1, keepdims=True) for hot in hots]
    rank_ref[...] = spread(ranks).astype(I32)
    run = run_ref[...] + jnp.sum(member_f, axis=0, keepdims=True)
    run_ref[...] = run
    cnt_ref[...] = run


def _merge(oa, ob, pb, x2d, wo, g, b, rw, rb):
    t = x2d.shape[0]
    tm = PROJ_TM
    const = lambda i: (0, 0)
    kern = functools.partial(_merge_kernel, tm=tm)
    return pl.pallas_call(
        kern,
        grid=(t // tm,),
        in_specs=[
            pl.BlockSpec((tm, D_MODEL), lambda i: (i, 0)),
            pl.BlockSpec((tm, D_MODEL), lambda i: (i, 0)),
            pl.BlockSpec((tm, D_MODEL), lambda i: (i, 0)),
            pl.BlockSpec((tm, D_MODEL), lambda i: (i, 1)),
            pl.BlockSpec((tm, D_MODEL), lambda i: (i, 0)),
            pl.BlockSpec((D_MODEL, D_MODEL), const),
            pl.BlockSpec((1, D_MODEL), const),
            pl.BlockSpec((1, D_MODEL), const),
            pl.BlockSpec((D_MODEL, LANES), const),
            pl.BlockSpec((1, LANES), const),
        ],
        out_specs=[
            pl.BlockSpec((tm, D_MODEL), lambda i: (i, 0)),
            pl.BlockSpec((tm, HALF), lambda i: (i, 0)),
            pl.BlockSpec((tm, LANES), lambda i: (i, 0)),
            pl.BlockSpec((tm, LANES), lambda i: (i, 0)),
            pl.BlockSpec((tm, LANES), lambda i: (i, 0)),
            pl.BlockSpec((1, LANES), const),
        ],
        out_shape=[
            jax.ShapeDtypeStruct((t, D_MODEL), F32),
            jax.ShapeDtypeStruct((t, HALF), U32),
            jax.ShapeDtypeStruct((t, LANES), I32),
            jax.ShapeDtypeStruct((t, LANES), F32),
            jax.ShapeDtypeStruct((t, LANES), I32),
            jax.ShapeDtypeStruct((1, LANES), F32),
        ],
        scratch_shapes=[pltpu.VMEM((1, LANES), F32)],
        compiler_params=_cparams(("arbitrary",)),
        name="merge",
    )(oa, ob, pb, pb, x2d, wo, g, b, rw, rb)


def _gather_rows(table, idx):
    n = idx.shape[0]
    d = table.shape[1]
    info = plsc.get_sparse_core_info()
    nw = info.num_cores * info.num_subcores
    per_w = n // nw
    assert per_w * nw == n and per_w % SC_CHUNK == 0
    n_chunks = per_w // SC_CHUNK
    mesh = plsc.VectorSubcoreMesh(core_axis_name="c", subcore_axis_name="s")

    @functools.partial(
        pl.kernel, mesh=mesh,
        out_type=jax.ShapeDtypeStruct((n, d), table.dtype),
        scratch_types=[
            pltpu.VMEM((SC_CHUNK,), I32),
            pltpu.VMEM((SC_CHUNK, d), table.dtype),
            pltpu.SemaphoreType.DMA,
        ],
    )
    def k(table_hbm, idx_hbm, out_hbm, idx_v, rows_v, sem):
        wid = lax.axis_index("s") * info.num_cores + lax.axis_index("c")
        base = wid * per_w

        @pl.loop(0, n_chunks)
        def _(j):
            off = pl.multiple_of(base + j * SC_CHUNK, 8)
            pltpu.sync_copy(idx_hbm.at[pl.ds(off, SC_CHUNK)], idx_v)
            pltpu.async_copy(table_hbm.at[idx_v], rows_v, sem).wait()
            pltpu.sync_copy(rows_v, out_hbm.at[pl.ds(off, SC_CHUNK)])

    return k(table, idx)


def _expert_kernel(be_ref, nused_ref, xs_ref, wgu_ref, bgu_ref, wd_ref, bd_ref, ys_ref):
    @pl.when(pl.program_id(0) < nused_ref[0])
    def _():
        lo, hi = _unpack_bf16_pair(xs_ref[...])
        lo = lo.astype(BF16)
        hi = hi.astype(BF16)
        acc = None
        step = 512
        for c0 in range(0, D_EXPERT, step):
            def proj(col):
                return (_dot(lo, wgu_ref[0, :HALF, col:col + step])
                        + _dot(hi, wgu_ref[0, HALF:, col:col + step])
                        + bgu_ref[0, :, col:col + step])
            gate = jnp.minimum(proj(c0), SWIGLU_LIMIT)
            up = jnp.clip(proj(D_EXPERT + c0), -SWIGLU_LIMIT, SWIGLU_LIMIT)
            act = ((up + 1.0) * gate * jax.nn.sigmoid(gate * SWIGLU_ALPHA)).astype(BF16)
            part = _dot(act, wd_ref[0, c0:c0 + step, :])
            acc = part if acc is None else acc + part
        y = acc + bd_ref[0]
        ys_ref[...] = _pack_bf16_pair(y[:, :HALF], y[:, HALF:])


def _experts(xs, block_expert, n_used, wgu, bgu, wd, bd):
    n_rows = xs.shape[0]
    bm = MOE_BM
    n_blocks = n_rows // bm
    rows = lambda i, be, nu: (jnp.minimum(i, nu[0] - 1), 0)
    grid_spec = pltpu.PrefetchScalarGridSpec(
        num_scalar_prefetch=2,
        grid=(n_blocks,),
        in_specs=[
            pl.BlockSpec((bm, HALF), rows),
            pl.BlockSpec((1, D_MODEL, 2 * D_EXPERT), lambda i, be, nu: (be[i], 0, 0)),
            pl.BlockSpec((1, 1, 2 * D_EXPERT), lambda i, be, nu: (be[i], 0, 0)),
            pl.BlockSpec((1, D_EXPERT, D_MODEL), lambda i, be, nu: (be[i], 0, 0)),
            pl.BlockSpec((1, 1, D_MODEL), lambda i, be, nu: (be[i], 0, 0)),
        ],
        out_specs=pl.BlockSpec((bm, HALF), rows),
    )
    return pl.pallas_call(
        _expert_kernel,
        grid_spec=grid_spec,
        out_shape=jax.ShapeDtypeStruct((n_rows, HALF), U32),
        compiler_params=_cparams(("arbitrary",)),
        name="experts",
    )(block_expert, n_used, xs, wgu, bgu, wd, bd)


def _combine_kernel(x1_ref, yg_ref, gate_ref, g_ref, b_ref, o_ref):
    gates = gate_ref[...]
    f_lo = None
    f_hi = None
    for kk in range(TOP_K):
        lo, hi = _unpack_bf16_pair(yg_ref[kk])
        w = gates[:, kk:kk + 1]
        f_lo = lo * w if f_lo is None else f_lo + lo * w
        f_hi = hi * w if f_hi is None else f_hi + hi * w
    x1 = x1_ref[...]
    y_lo = DEEPNORM_ALPHA * x1[:, :HALF] + f_lo
    y_hi = DEEPNORM_ALPHA * x1[:, HALF:] + f_hi
    mu = (jnp.sum(y_lo, axis=-1, keepdims=True) + jnp.sum(y_hi, axis=-1, keepdims=True)) / D_MODEL
    d_lo = y_lo - mu
    d_hi = y_hi - mu
    var = (jnp.sum(d_lo * d_lo, axis=-1, keepdims=True)
           + jnp.sum(d_hi * d_hi, axis=-1, keepdims=True)) / D_MODEL
    inv = lax.rsqrt(var + LN_EPS)
    g = g_ref[...]
    b = b_ref[...]
    o_ref[:, :HALF] = d_lo * inv * g[:, :HALF] + b[:, :HALF]
    o_ref[:, HALF:] = d_hi * inv * g[:, HALF:] + b[:, HALF:]


def _combine(x1, yg, gates, g, b):
    t = x1.shape[0]
    tm = PROJ_TM
    const = lambda i: (0, 0)
    return pl.pallas_call(
        _combine_kernel,
        grid=(t // tm,),
        in_specs=[
            pl.BlockSpec((tm, D_MODEL), lambda i: (i, 0)),
            pl.BlockSpec((TOP_K, tm, HALF), lambda i: (0, i, 0)),
            pl.BlockSpec((tm, LANES), lambda i: (i, 0)),
            pl.BlockSpec((1, D_MODEL), const),
            pl.BlockSpec((1, D_MODEL), const),
        ],
        out_specs=pl.BlockSpec((tm, D_MODEL), lambda i: (i, 0)),
        out_shape=jax.ShapeDtypeStruct((t, D_MODEL), F32),
        compiler_params=_cparams(("parallel",)),
        name="combine",
    )(x1, yg, gates, g, b)


def _rope_tables(seq):
    half = ROPE_DIM // 2
    pos = jnp.arange(seq, dtype=F32)
    inv_freq = ROPE_THETA ** (-jnp.arange(0, ROPE_DIM, 2, dtype=F32) / ROPE_DIM)
    ang = pos[:, None] * inv_freq[None, :]
    cos, sin = jnp.cos(ang), jnp.sin(ang)
    zeros = jnp.zeros((seq, LANES - ROPE_DIM), F32)
    cos_f = jnp.concatenate([cos, cos, jnp.ones((seq, LANES - ROPE_DIM), F32)], axis=1)
    sin_a = jnp.concatenate([-sin, jnp.zeros((seq, half), F32), zeros], axis=1)
    sin_b = jnp.concatenate([jnp.zeros((seq, half), F32), sin, zeros], axis=1)
    return cos_f, sin_a, sin_b


def _prep_weights(w_in, gla_gate_w2, gla_gate_b, gla_norm_w, w_o, ln1_g, ln1_b, router_w, router_b,
                  w_gate_up, b_gate_up, w_down, b_down, ln2_g, ln2_b):
    sizes = (GLA_KEY, GLA_KEY, GLA_VAL, GLA_VAL, 2 * GLA_RANK, SWA_HEADS * SWA_HD,
             SWA_KV * SWA_HD, SWA_KV * SWA_HD, D_MODEL, D_MODEL)
    cols, acc = [], 0
    for s in sizes:
        cols.append(w_in[:, acc:acc + s])
        acc += s
    gq, gk, gv, gr, glr, sq, sk, sv, ga, gb = cols
    wa = jnp.concatenate([gq, gk, gv, gr], axis=1).astype(BF16)
    wl = jnp.pad(glr, ((0, 0), (0, LANES - 2 * GLA_RANK))).astype(BF16)
    wb = jnp.concatenate([ga, gb, sq, sk, sv], axis=1).astype(BF16)
    w2p = jnp.zeros((2, LANES, GLA_KEY), F32)
    w2p = w2p.at[0, :GLA_RANK].set(gla_gate_w2[0]).at[1, GLA_RANK:2 * GLA_RANK].set(gla_gate_w2[1])
    rw = jnp.pad(router_w, ((0, 0), (0, LANES - N_EXPERTS)))
    rb = jnp.concatenate([router_b, jnp.full((LANES - N_EXPERTS,), -jnp.inf, F32)]).reshape(1, LANES)
    return dict(
        wa=wa, wl=wl, wb=wb, w2p=w2p, bga=gla_gate_b.reshape(2, 1, GLA_KEY),
        nw=gla_norm_w.reshape(1, GLA_VAL), wo=w_o.astype(BF16),
        ln1_g=ln1_g.reshape(1, D_MODEL), ln1_b=ln1_b.reshape(1, D_MODEL), rw=rw, rb=rb,
        wgu=w_gate_up.astype(BF16), bgu=b_gate_up.reshape(N_EXPERTS, 1, 2 * D_EXPERT),
        wd=w_down.astype(BF16), bd=b_down.reshape(N_EXPERTS, 1, D_MODEL),
        ln2_g=ln2_g.reshape(1, D_MODEL), ln2_b=ln2_b.reshape(1, D_MODEL))


def _routing_plan(idx, rank, counts):
    t = idx.shape[0]
    bm = MOE_BM
    n_rows = t * TOP_K + N_EXPERTS * bm
    n_blocks = n_rows // bm
    counts = counts[0, :N_EXPERTS].astype(I32)
    padded = (counts + bm - 1) // bm * bm
    pad_ends = jnp.cumsum(padded)
    pad_starts = pad_ends - padded
    hot = idx[:, :, None] == jnp.arange(N_EXPERTS, dtype=I32)[None, None, :]
    dest = jnp.sum(jnp.where(hot, pad_starts[None, None, :], 0), axis=-1) + rank
    tok = jnp.broadcast_to(jnp.arange(t, dtype=I32)[:, None], (t, TOP_K))
    tok_pad = jnp.zeros((n_rows,), I32).at[dest.reshape(-1)].set(tok.reshape(-1))
    block_start = jnp.arange(n_blocks, dtype=I32) * bm
    block_expert = jnp.minimum(
        jnp.sum(block_start[:, None] >= pad_ends[None, :], axis=1), N_EXPERTS - 1).astype(I32)
    n_used = (pad_ends[-1] // bm).astype(I32).reshape(1)
    return dest, tok_pad, block_expert, n_used


def _trunk(x, w, sinks, gather_rows):
    bsz, seq, _ = x.shape
    t = bsz * seq
    x2d = x.reshape(t, D_MODEL)
    cos_f, sin_a, sin_b = _rope_tables(seq)
    pa, plr, pb = _in_proj(x2d, w["wa"], w["wl"], w["wb"], cos_f, sin_a, sin_b, seq)
    oa = _gla(pa, plr, w["w2p"], w["bga"], w["nw"], bsz, seq).reshape(t, GLA_VAL)
    ob = _swa(pb, sinks, bsz, seq).reshape(t, SWA_HEADS * SWA_HD)
    x1, xp, idx, gates, rank, counts = _merge(
        oa, ob, pb, x2d, w["wo"], w["ln1_g"], w["ln1_b"], w["rw"], w["rb"])
    dest, tok_pad, block_expert, n_used = _routing_plan(idx[:, :TOP_K], rank[:, :TOP_K], counts)
    xs = gather_rows(xp, tok_pad)
    ys = _experts(xs, block_expert, n_used, w["wgu"], w["bgu"], w["wd"], w["bd"])
    yg = gather_rows(ys, dest.T.reshape(-1)).reshape(TOP_K, t, HALF)
    out = _combine(x1, yg, gates, w["ln2_g"], w["ln2_b"])
    return out.reshape(bsz, seq, D_MODEL)


def _forward(x_prompt, x_sample, w_in, gla_gate_w2, gla_gate_b, gla_norm_w, swa_sinks, w_o, ln1_g,
             ln1_b, router_w, router_b, w_gate_up, b_gate_up, w_down, b_down, ln2_g, ln2_b,
             gather_rows=_gather_rows):
    w = _prep_weights(w_in[0], gla_gate_w2[0], gla_gate_b[0], gla_norm_w[0], w_o[0], ln1_g[0],
                      ln1_b[0], router_w[0], router_b[0], w_gate_up[0], b_gate_up[0], w_down[0],
                      b_down[0], ln2_g[0], ln2_b[0])
    sinks = swa_sinks[0].astype(F32)
    return (_trunk(x_prompt, w, sinks, gather_rows), _trunk(x_sample, w, sinks, gather_rows))


def kernel(x_prompt, x_sample, w_in, gla_gate_w2, gla_gate_b, gla_norm_w, swa_sinks, w_o, ln1_g, ln1_b, router_w, router_b, w_gate_up, b_gate_up, w_down, b_down, ln2_g, ln2_b):
    return _forward(x_prompt, x_sample, w_in, gla_gate_w2, gla_gate_b, gla_norm_w, swa_sinks, w_o,
                    ln1_g, ln1_b, router_w, router_b, w_gate_up, b_gate_up, w_down, b_down,
                    ln2_g, ln2_b)
```

```python
import functools

import jax
import jax.numpy as jnp
from jax import lax
from jax.experimental import pallas as pl
from jax.experimental.pallas import tpu as pltpu
from jax.experimental.pallas import tpu_sc as plsc

F32 = jnp.float32
BF16 = jnp.bfloat16
I32 = jnp.int32
U32 = jnp.uint32

D_MODEL = 1024
GLA_HEADS = 4
GLA_DK = 128
GLA_DV = 256
GLA_KEY = GLA_HEADS * GLA_DK
GLA_VAL = GLA_HEADS * GLA_DV
GLA_RANK = 16
GLA_NORMALIZER = 16.0
GLA_CHUNK = 64
SWA_HEADS = 8
SWA_KV = 2
SWA_GROUP = SWA_HEADS // SWA_KV
SWA_HD = 128
SWA_WINDOW = 128
ROPE_THETA = 500000.0
ROPE_DIM = SWA_HD // 4
N_EXPERTS = 32
TOP_K = 4
D_EXPERT = D_MODEL
SWIGLU_LIMIT = 7.0
SWIGLU_ALPHA = 1.702
DEEPNORM_ALPHA = 2.0 ** 0.25
LN_EPS = 1e-5
RMS_EPS = 1e-6

LANES = 128
PA_W = 2 * GLA_KEY + 2 * GLA_VAL
PB_W = 2 * D_MODEL + SWA_HEADS * SWA_HD + 2 * SWA_KV * SWA_HD
HALF = D_MODEL // 2

PROJ_TM = 512
GLA_TS = 512
SWA_TQ = 512
MOE_BM = 512
SC_CHUNK = 64
VMEM_LIMIT = 56 * 1024 * 1024


def _cparams(sem):
    return pltpu.CompilerParams(dimension_semantics=sem, vmem_limit_bytes=VMEM_LIMIT)


def _split2(a):
    hi = a.astype(BF16)
    lo = (a - hi.astype(F32)).astype(BF16)
    return hi, lo


def _split3(a):
    hi = a.astype(BF16)
    r = a - hi.astype(F32)
    mid = r.astype(BF16)
    lo = (r - mid.astype(F32)).astype(BF16)
    return hi, mid, lo


def _dot(a, b):
    return jnp.dot(a, b, preferred_element_type=F32)


def _dot_nt(a, b):
    return lax.dot_general(a, b, (((1,), (1,)), ((), ())), preferred_element_type=F32)


def _dot_f32ish(a, b):
    ah, al = _split2(a)
    bh, bl = _split2(b)
    return _dot(ah, bh) + _dot(al, bh) + _dot(ah, bl)


def _pack_bf16_pair(lo_f32, hi_f32):
    lo = lax.bitcast_convert_type(lo_f32.astype(BF16).astype(F32), U32)
    hi = lax.bitcast_convert_type(hi_f32.astype(BF16).astype(F32), U32)
    return (hi & jnp.uint32(0xFFFF0000)) | (lo >> 16)


def _unpack_bf16_pair(w):
    lo = lax.bitcast_convert_type(w << 16, F32)
    hi = lax.bitcast_convert_type(w & jnp.uint32(0xFFFF0000), F32)
    return lo, hi


def _in_proj_kernel(x_ref, wa_ref, wl_ref, wb_ref, cos_ref, sina_ref, sinb_ref,
                    pa_ref, pl_ref, pb_ref):
    xb = x_ref[...].astype(BF16)
    step = 512
    for c0 in range(0, PA_W, step):
        r = _dot(xb, wa_ref[:, c0:c0 + step])
        if c0 < GLA_KEY:
            r = r * (GLA_DK ** -0.5)
        pa_ref[:, c0:c0 + step] = r.astype(BF16)
    pl_ref[...] = _dot(xb, wl_ref[...])
    cos = cos_ref[...]
    sina = sina_ref[...]
    sinb = sinb_ref[...]
    rope0 = 2 * D_MODEL
    rope_end = rope0 + (SWA_HEADS + SWA_KV) * SWA_HD
    for c0 in range(0, PB_W, step):
        r = _dot(xb, wb_ref[:, c0:c0 + step])
        if c0 + step > rope0 and c0 < rope_end:
            parts = []
            for j in range(step // SWA_HD):
                t = r[:, j * SWA_HD:(j + 1) * SWA_HD]
                col = c0 + j * SWA_HD
                if rope0 <= col < rope_end:
                    t = (t * cos + pltpu.roll(t, SWA_HD - ROPE_DIM // 2, 1) * sina
                         + pltpu.roll(t, ROPE_DIM // 2, 1) * sinb)
                if rope0 <= col < rope0 + SWA_HEADS * SWA_HD:
                    t = t * (SWA_HD ** -0.5)
                parts.append(t)
            r = jnp.concatenate(parts, axis=1)
        pb_ref[:, c0:c0 + step] = r.astype(BF16)


def _in_proj(x2d, wa, wl, wb, cos, sina, sinb, seq):
    t = x2d.shape[0]
    tm = PROJ_TM
    nseq = seq // tm
    const = lambda i: (0, 0)
    return pl.pallas_call(
        _in_proj_kernel,
        grid=(t // tm,),
        in_specs=[
            pl.BlockSpec((tm, D_MODEL), lambda i: (i, 0)),
            pl.BlockSpec((D_MODEL, PA_W), const),
            pl.BlockSpec((D_MODEL, LANES), const),
            pl.BlockSpec((D_MODEL, PB_W), const),
            pl.BlockSpec((tm, LANES), lambda i: (i % nseq, 0)),
            pl.BlockSpec((tm, LANES), lambda i: (i % nseq, 0)),
            pl.BlockSpec((tm, LANES), lambda i: (i % nseq, 0)),
        ],
        out_specs=[
            pl.BlockSpec((tm, PA_W), lambda i: (i, 0)),
            pl.BlockSpec((tm, LANES), lambda i: (i, 0)),
            pl.BlockSpec((tm, PB_W), lambda i: (i, 0)),
        ],
        out_shape=[
            jax.ShapeDtypeStruct((t, PA_W), BF16),
            jax.ShapeDtypeStruct((t, LANES), F32),
            jax.ShapeDtypeStruct((t, PB_W), BF16),
        ],
        compiler_params=_cparams(("parallel",)),
        name="in_proj",
    )(x2d, wa, wl, wb, cos, sina, sinb)


def _gla_kernel(q_ref, k_ref, v_ref, r_ref, lr_ref, w2_ref, bga_ref, nw_ref, o_ref,
                state_ref, ofwd_ref, sall_ref, ds_ref, dec_ref, *, ts, nt):
    ph = pl.program_id(2)
    t = pl.program_id(3)
    fwd = ph == 0
    c = GLA_CHUNK
    nc = ts // c
    tile = jnp.where(fwd, t, nt - 1 - t)

    @pl.when(t == 0)
    def _():
        state_ref[...] = jnp.zeros_like(state_ref)

    q = q_ref[0].astype(F32)
    k = k_ref[0].astype(F32)
    v = v_ref[0]
    z = _dot_f32ish(lr_ref[0], w2_ref[0]) + bga_ref[0]
    la = (jnp.minimum(z, 0.0) - jnp.log(1.0 + jnp.exp(-jnp.abs(z)))) * (1.0 / GLA_NORMALIZER)

    row = lax.broadcasted_iota(I32, (c, c), 0)
    col = lax.broadcasted_iota(I32, (c, c), 1)
    ahead = (col - row) * jnp.where(fwd, 1, -1)
    tri = (ahead <= 0).astype(BF16)
    keep = ahead <= jnp.where(fwd, 0, -1)
    b_parts = []
    for ci in range(nc):
        hi, mid, lo = _split3(la[ci * c:(ci + 1) * c])
        b_parts.append(_dot(tri, hi) + _dot(tri, mid) + _dot(tri, lo))
    b3 = jnp.stack(b_parts)
    b_mid = jnp.where(fwd, b3[:, c // 2:c // 2 + 1], b3[:, c // 2 - 1:c // 2])
    b_end = jnp.where(fwd, b3[:, c - 1:c], b3[:, 0:1])
    q3 = q.reshape(nc, c, GLA_DK)
    k3 = k.reshape(nc, c, GLA_DK)
    v3 = v.reshape(nc, c, GLA_DV)
    q_mid = q3 * jnp.exp(b3 - b_mid)
    k_mid = k3 * jnp.exp(b_mid - b3)
    q_in = q_mid.astype(BF16)
    k_in = k_mid.astype(BF16)
    k_st = (k_mid * jnp.exp(b_end - b_mid)).astype(BF16)
    q_b = (q_mid * jnp.exp(b_mid)).astype(BF16)
    dec_ref[...] = jnp.exp(b_end)
    scores = jnp.einsum("cid,cjd->cij", q_in, k_in, preferred_element_type=F32)
    scores = jnp.where(keep[None], scores, 0.0).astype(BF16)
    o_intra = jnp.einsum("cij,cjv->civ", scores, v3, preferred_element_type=F32)
    ds_ref[...] = jnp.einsum("cjv,cjd->cvd", v3, k_st, preferred_element_type=F32)

    st = state_ref[...]
    for i in range(nc):
        ci = jnp.where(fwd, i, nc - 1 - i)
        sall_ref[ci] = st.astype(BF16)
        st = st * dec_ref[ci] + ds_ref[ci]
    state_ref[...] = st
    o_inter = jnp.einsum("cid,cvd->civ", q_b, sall_ref[...], preferred_element_type=F32)
    o = (o_intra + o_inter).reshape(ts, GLA_DV)
    off = pl.multiple_of(tile * ts, ts)

    @pl.when(fwd)
    def _():
        ofwd_ref[pl.ds(off, ts), :] = o

    @pl.when(jnp.logical_not(fwd))
    def _():
        tot = o + ofwd_ref[pl.ds(off, ts), :]
        tot = tot * lax.rsqrt(jnp.mean(tot * tot, axis=-1, keepdims=True) + RMS_EPS)
        r = r_ref[0].astype(F32)
        o_ref[0] = (tot * nw_ref[...] * (r * jax.nn.sigmoid(r))).astype(BF16)


def _gla(pa, plr, w2p, bga, nw, bsz, seq):
    ts = min(GLA_TS, seq)
    nt = seq // ts
    pa3 = pa.reshape(bsz, seq, PA_W)
    lr3 = plr.reshape(bsz, seq, LANES)
    tile = lambda p, t: t + p * (nt - 1 - 2 * t)
    kq = GLA_KEY // GLA_DK
    kv = 2 * GLA_KEY // GLA_DV
    kr = kv + GLA_VAL // GLA_DV
    kern = functools.partial(_gla_kernel, ts=ts, nt=nt)
    return pl.pallas_call(
        kern,
        grid=(bsz, GLA_HEADS, 2, nt),
        in_specs=[
            pl.BlockSpec((1, ts, GLA_DK), lambda b, h, p, t: (b, tile(p, t), h)),
            pl.BlockSpec((1, ts, GLA_DK), lambda b, h, p, t: (b, tile(p, t), kq + h)),
            pl.BlockSpec((1, ts, GLA_DV), lambda b, h, p, t: (b, tile(p, t), kv + h)),
            pl.BlockSpec((1, ts, GLA_DV), lambda b, h, p, t: (b, nt - 1 - p * t, kr + h)),
            pl.BlockSpec((1, ts, LANES), lambda b, h, p, t: (b, tile(p, t), 0)),
            pl.BlockSpec((1, LANES, GLA_DK), lambda b, h, p, t: (p, 0, h)),
            pl.BlockSpec((1, 1, GLA_DK), lambda b, h, p, t: (p, 0, h)),
            pl.BlockSpec((1, GLA_DV), lambda b, h, p, t: (0, h)),
        ],
        out_specs=pl.BlockSpec((1, ts, GLA_DV), lambda b, h, p, t: (b, nt - 1 - p * t, h)),
        out_shape=jax.ShapeDtypeStruct((bsz, seq, GLA_VAL), BF16),
        scratch_shapes=[
            pltpu.VMEM((GLA_DV, GLA_DK), F32),
            pltpu.VMEM((seq, GLA_DV), F32),
            pltpu.VMEM((ts // GLA_CHUNK, GLA_DV, GLA_DK), BF16),
            pltpu.VMEM((ts // GLA_CHUNK, GLA_DV, GLA_DK), F32),
            pltpu.VMEM((ts // GLA_CHUNK, 1, GLA_DK), F32),
        ],
        compiler_params=_cparams(("parallel", "parallel", "arbitrary", "arbitrary")),
        name="gla",
    )(pa3, pa3, pa3, pa3, lr3, w2p, bga, nw)


def _swa_kernel(sink_ref, q_ref, kl_ref, kc_ref, kr_ref, vl_ref, vc_ref, vr_ref, o_ref, *, tq, seq):
    kvh = pl.program_id(1)
    n = pl.program_id(2)
    blk = SWA_WINDOW
    k_all = jnp.concatenate([kl_ref[0], kc_ref[0], kr_ref[0]], axis=0)
    v_all = jnp.concatenate([vl_ref[0], vc_ref[0], vr_ref[0]], axis=0)
    rows = SWA_GROUP * blk
    qi = lax.broadcasted_iota(I32, (rows, 3 * blk), 0) & (blk - 1)
    kj = lax.broadcasted_iota(I32, (rows, 3 * blk), 1) - blk
    band = jnp.abs(kj - qi) <= SWA_WINDOW
    head = lax.broadcasted_iota(I32, (rows, 1), 0) // blk
    sink = jnp.zeros((rows, 1), F32)
    for g in range(SWA_GROUP):
        sink = jnp.where(head == g, sink_ref[kvh * SWA_GROUP + g], sink)
    for jq in range(tq // blk):
        q0 = n * tq + jq * blk
        kpos = kj + q0
        valid = band & (kpos >= 0) & (kpos < seq)
        qs = q_ref[0, jq * blk:(jq + 1) * blk, :]
        q4 = jnp.concatenate([qs[:, g * SWA_HD:(g + 1) * SWA_HD] for g in range(SWA_GROUP)], axis=0)
        keys = k_all[jq * blk:(jq + 3) * blk]
        vals = v_all[jq * blk:(jq + 3) * blk]
        s = jnp.where(valid, _dot_nt(q4, keys), -jnp.inf)
        m = jnp.maximum(jnp.max(s, axis=-1, keepdims=True), sink)
        p = jnp.exp(s - m)
        denom = jnp.sum(p, axis=-1, keepdims=True) + jnp.exp(sink - m)
        o = _dot(p.astype(BF16), vals) / denom
        for g in range(SWA_GROUP):
            o_ref[0, jq * blk:(jq + 1) * blk, g * SWA_HD:(g + 1) * SWA_HD] = (
                o[g * blk:(g + 1) * blk].astype(BF16))


def _swa(pb, sinks, bsz, seq):
    tq = min(SWA_TQ, seq)
    blk = SWA_WINDOW
    r = tq // blk
    nb = seq // blk
    pb3 = pb.reshape(bsz, seq, PB_W)
    gw = SWA_GROUP * SWA_HD
    cq = 2 * D_MODEL // gw
    ck = (2 * D_MODEL + SWA_HEADS * SWA_HD) // SWA_HD
    cv = ck + SWA_KV
    left = lambda n: jnp.maximum(n * r - 1, 0)
    right = lambda n: jnp.minimum(n * r + r, nb - 1)
    kern = functools.partial(_swa_kernel, tq=tq, seq=seq)
    grid_spec = pltpu.PrefetchScalarGridSpec(
        num_scalar_prefetch=1,
        grid=(bsz, SWA_KV, seq // tq),
        in_specs=[
            pl.BlockSpec((1, tq, gw), lambda b, h, n, s: (b, n, cq + h)),
            pl.BlockSpec((1, blk, SWA_HD), lambda b, h, n, s: (b, left(n), ck + h)),
            pl.BlockSpec((1, tq, SWA_HD), lambda b, h, n, s: (b, n, ck + h)),
            pl.BlockSpec((1, blk, SWA_HD), lambda b, h, n, s: (b, right(n), ck + h)),
            pl.BlockSpec((1, blk, SWA_HD), lambda b, h, n, s: (b, left(n), cv + h)),
            pl.BlockSpec((1, tq, SWA_HD), lambda b, h, n, s: (b, n, cv + h)),
            pl.BlockSpec((1, blk, SWA_HD), lambda b, h, n, s: (b, right(n), cv + h)),
        ],
        out_specs=pl.BlockSpec((1, tq, gw), lambda b, h, n, s: (b, n, h)),
    )
    return pl.pallas_call(
        kern,
        grid_spec=grid_spec,
        out_shape=jax.ShapeDtypeStruct((bsz, seq, SWA_HEADS * SWA_HD), BF16),
        compiler_params=_cparams(("parallel", "parallel", "parallel")),
        name="swa",
    )(sinks, pb3, pb3, pb3, pb3, pb3, pb3, pb3)


def _layer_norm(y, g, b):
    mu = jnp.mean(y, axis=-1, keepdims=True)
    d = y - mu
    var = jnp.mean(d * d, axis=-1, keepdims=True)
    return d * lax.rsqrt(var + LN_EPS) * g + b


def _merge_kernel(oa_ref, ob_ref, ga_ref, gb_ref, x_ref, wo_ref, g_ref, b_ref, rw_ref, rb_ref,
                  x1_ref, xp_ref, idx_ref, gate_ref, rank_ref, cnt_ref, run_ref, *, tm):
    @pl.when(pl.program_id(0) == 0)
    def _():
        run_ref[...] = jnp.zeros_like(run_ref)

    h = (jax.nn.sigmoid(ga_ref[...].astype(F32)) * oa_ref[...].astype(F32)
         + jax.nn.sigmoid(gb_ref[...].astype(F32)) * ob_ref[...].astype(F32))
    y = DEEPNORM_ALPHA * x_ref[...] + _dot(h.astype(BF16), wo_ref[...])
    x1 = _layer_norm(y, g_ref[...], b_ref[...])
    x1_ref[...] = x1
    xp_ref[...] = _pack_bf16_pair(x1[:, :HALF], x1[:, HALF:])

    logits = _dot_f32ish(x1, rw_ref[...]) + rb_ref[...]
    lane_i = lax.broadcasted_iota(I32, (tm, LANES), 1)
    lane = lane_i.astype(F32)
    work = logits
    vals, idxs, hots = [], [], []
    for _ in range(TOP_K):
        m = jnp.max(work, axis=-1, keepdims=True)
        sel = jnp.min(jnp.where(work == m, lane, float(LANES)), axis=-1, keepdims=True)
        hot = lane == sel
        vals.append(m)
        idxs.append(sel)
        hots.append(hot)
        work = jnp.where(hot, -jnp.inf, work)
    e = [jnp.exp(v - vals[0]) for v in vals]
    tot = e[0] + e[1] + e[2] + e[3]

    def spread(cols):
        out = jnp.zeros((tm, LANES), F32)
        for kk, col in enumerate(cols):
            out = jnp.where(lane_i == kk, col, out)
        return out

    gate_ref[...] = spread([ei / tot for ei in e])
    idx_ref[...] = spread(idxs).astype(I32)

    member = (hots[0] | hots[1] | hots[2] | hots[3])
    member_f = member.astype(F32)
    r_i = lax.broadcasted_iota(I32, (tm, tm), 0)
    c_i = lax.broadcasted_iota(I32, (tm, tm), 1)
    before = (c_i < r_i).astype(BF16)
    cum = _dot(before, member.astype(BF16)) + run_ref[...]
    ranks = [jnp.sum(jnp.where(hot, cum, 0.0), axis=-1, keepdims=True) for hot in hots]
    rank_ref[...] = spread(ranks).astype(I32)
    run = run_ref[...] + jnp.sum(member_f, axis=0, keepdims=True)
    run_ref[...] = run
    cnt_ref[...] = run


def _merge(oa, ob, pb, x2d, wo, g, b, rw, rb):
    t = x2d.shape[0]
    tm = PROJ_TM
    const = lambda i: (0, 0)
    kern = functools.partial(_merge_kernel, tm=tm)
    return pl.pallas_call(
        kern,
        grid=(t // tm,),
        in_specs=[
            pl.BlockSpec((tm, D_MODEL), lambda i: (i, 0)),
            pl.BlockSpec((tm, D_MODEL), lambda i: (i, 0)),
            pl.BlockSpec((tm, D_MODEL), lambda i: (i, 0)),
            pl.BlockSpec((tm, D_MODEL), lambda i: (i, 1)),
            pl.BlockSpec((tm, D_MODEL), lambda i: (i, 0)),
            pl.BlockSpec((D_MODEL, D_MODEL), const),
            pl.BlockSpec((1, D_MODEL), const),
            pl.BlockSpec((1, D_MODEL), const),
            pl.BlockSpec((D_MODEL, LANES), const),
            pl.BlockSpec((1, LANES), const),
        ],
        out_specs=[
            pl.BlockSpec((tm, D_MODEL), lambda i: (i, 0)),
            pl.BlockSpec((tm, HALF), lambda i: (i, 0)),
            pl.BlockSpec((tm, LANES), lambda i: (i, 0)),
            pl.BlockSpec((tm, LANES), lambda i: (i, 0)),
            pl.BlockSpec((tm, LANES), lambda i: (i, 0)),
            pl.BlockSpec((1, LANES), const),
        ],
        out_shape=[
            jax.ShapeDtypeStruct((t, D_MODEL), F32),
            jax.ShapeDtypeStruct((t, HALF), U32),
            jax.ShapeDtypeStruct((t, LANES), I32),
            jax.ShapeDtypeStruct((t, LANES), F32),
            jax.ShapeDtypeStruct((t, LANES), I32),
            jax.ShapeDtypeStruct((1, LANES), F32),
        ],
        scratch_shapes=[pltpu.VMEM((1, LANES), F32)],
        compiler_params=_cparams(("arbitrary",)),
        name="merge",
    )(oa, ob, pb, pb, x2d, wo, g, b, rw, rb)


def _gather_rows(table, idx):
    n = idx.shape[0]
    d = table.shape[1]
    info = plsc.get_sparse_core_info()
    nw = info.num_cores * info.num_subcores
    per_w = n // nw
    assert per_w * nw == n and per_w % SC_CHUNK == 0
    n_chunks = per_w // SC_CHUNK
    mesh = plsc.VectorSubcoreMesh(core_axis_name="c", subcore_axis_name="s")

    @functools.partial(
        pl.kernel, mesh=mesh,
        out_type=jax.ShapeDtypeStruct((n, d), table.dtype),
        scratch_types=[
            pltpu.VMEM((SC_CHUNK,), I32),
            pltpu.VMEM((SC_CHUNK, d), table.dtype),
            pltpu.SemaphoreType.DMA,
        ],
    )
    def k(table_hbm, idx_hbm, out_hbm, idx_v, rows_v, sem):
        wid = lax.axis_index("s") * info.num_cores + lax.axis_index("c")
        base = wid * per_w

        @pl.loop(0, n_chunks)
        def _(j):
            off = pl.multiple_of(base + j * SC_CHUNK, 8)
            pltpu.sync_copy(idx_hbm.at[pl.ds(off, SC_CHUNK)], idx_v)
            pltpu.async_copy(table_hbm.at[idx_v], rows_v, sem).wait()
            pltpu.sync_copy(rows_v, out_hbm.at[pl.ds(off, SC_CHUNK)])

    return k(table, idx)


def _scatter_rows(src, dest_km, n_rows):
    t, d = src.shape
    info = plsc.get_sparse_core_info()
    nw = info.num_cores * info.num_subcores
    per_w = t // nw
    assert per_w * nw == t and per_w % SC_CHUNK == 0
    n_chunks = per_w // SC_CHUNK
    mesh = plsc.VectorSubcoreMesh(core_axis_name="c", subcore_axis_name="s")

    @functools.partial(
        pl.kernel, mesh=mesh,
        out_type=jax.ShapeDtypeStruct((n_rows, d), src.dtype),
        scratch_types=[
            pltpu.VMEM((SC_CHUNK,), I32),
            pltpu.VMEM((SC_CHUNK, d), src.dtype),
            pltpu.SemaphoreType.DMA,
        ],
    )
    def k(src_hbm, dest_hbm, out_hbm, idx_v, rows_v, sem):
        wid = lax.axis_index("s") * info.num_cores + lax.axis_index("c")
        base = wid * per_w

        @pl.loop(0, n_chunks)
        def _(j):
            off = pl.multiple_of(base + j * SC_CHUNK, 8)
            pltpu.sync_copy(src_hbm.at[pl.ds(off, SC_CHUNK)], rows_v)
            for kk in range(TOP_K):
                pltpu.sync_copy(dest_hbm.at[pl.ds(pl.multiple_of(kk * t + off, 8), SC_CHUNK)], idx_v)
                pltpu.async_copy(rows_v, out_hbm.at[idx_v], sem).wait()

    return k(src, dest_km)


def _expert_kernel(be_ref, nused_ref, nvalid_ref, xs_ref, wgu_ref, bgu_ref, wd_ref, bd_ref, ys_ref,
                   wgu_bf, wd_bf):
    i = pl.program_id(0)

    @pl.when(i < nused_ref[0])
    def _():
        @pl.when((i == 0) | (be_ref[i] != be_ref[jnp.maximum(i - 1, 0)]))
        def _():
            rows = 256
            for r0 in range(0, D_MODEL, rows):
                wgu_bf[r0:r0 + rows, :] = wgu_ref[0, r0:r0 + rows, :].astype(BF16)
            for r0 in range(0, D_EXPERT, rows):
                wd_bf[r0:r0 + rows, :] = wd_ref[0, r0:r0 + rows, :].astype(BF16)

        rowid = lax.broadcasted_iota(I32, (MOE_BM, 1), 0)
        words = jnp.where(rowid < nvalid_ref[i], xs_ref[...], jnp.uint32(0))
        lo, hi = _unpack_bf16_pair(words)
        lo = lo.astype(BF16)
        hi = hi.astype(BF16)
        acc = None
        step = 512
        for c0 in range(0, D_EXPERT, step):
            def proj(col):
                return (_dot(lo, wgu_bf[:HALF, col:col + step])
                        + _dot(hi, wgu_bf[HALF:, col:col + step])
                        + bgu_ref[0, :, col:col + step])
            gate = jnp.minimum(proj(c0), SWIGLU_LIMIT)
            up = jnp.clip(proj(D_EXPERT + c0), -SWIGLU_LIMIT, SWIGLU_LIMIT)
            act = ((up + 1.0) * gate * jax.nn.sigmoid(gate * SWIGLU_ALPHA)).astype(BF16)
            part = _dot(act, wd_bf[c0:c0 + step, :])
            acc = part if acc is None else acc + part
        y = acc + bd_ref[0]
        ys_ref[...] = _pack_bf16_pair(y[:, :HALF], y[:, HALF:])


def _experts(xs, block_expert, n_used, n_valid, wgu, bgu, wd, bd):
    n_rows = xs.shape[0]
    bm = MOE_BM
    n_blocks = n_rows // bm
    rows = lambda i, be, nu, nv: (jnp.minimum(i, nu[0] - 1), 0)
    expert = lambda i, be, nu, nv: (be[i], 0, 0)
    grid_spec = pltpu.PrefetchScalarGridSpec(
        num_scalar_prefetch=3,
        grid=(n_blocks,),
        in_specs=[
            pl.BlockSpec((bm, HALF), rows),
            pl.BlockSpec((1, D_MODEL, 2 * D_EXPERT), expert),
            pl.BlockSpec((1, 1, 2 * D_EXPERT), expert),
            pl.BlockSpec((1, D_EXPERT, D_MODEL), expert),
            pl.BlockSpec((1, 1, D_MODEL), expert),
        ],
        out_specs=pl.BlockSpec((bm, HALF), rows),
        scratch_shapes=[
            pltpu.VMEM((D_MODEL, 2 * D_EXPERT), BF16),
            pltpu.VMEM((D_EXPERT, D_MODEL), BF16),
        ],
    )
    return pl.pallas_call(
        _expert_kernel,
        grid_spec=grid_spec,
        out_shape=jax.ShapeDtypeStruct((n_rows, HALF), U32),
        compiler_params=_cparams(("arbitrary",)),
        name="experts",
    )(block_expert, n_used, n_valid, xs, wgu, bgu, wd, bd)


def _combine_kernel(x1_ref, yg_ref, gate_ref, g_ref, b_ref, o_ref):
    gates = gate_ref[...]
    f_lo = None
    f_hi = None
    for kk in range(TOP_K):
        lo, hi = _unpack_bf16_pair(yg_ref[kk])
        w = gates[:, kk:kk + 1]
        f_lo = lo * w if f_lo is None else f_lo + lo * w
        f_hi = hi * w if f_hi is None else f_hi + hi * w
    x1 = x1_ref[...]
    y_lo = DEEPNORM_ALPHA * x1[:, :HALF] + f_lo
    y_hi = DEEPNORM_ALPHA * x1[:, HALF:] + f_hi
    mu = (jnp.sum(y_lo, axis=-1, keepdims=True) + jnp.sum(y_hi, axis=-1, keepdims=True)) / D_MODEL
    d_lo = y_lo - mu
    d_hi = y_hi - mu
    var = (jnp.sum(d_lo * d_lo, axis=-1, keepdims=True)
           + jnp.sum(d_hi * d_hi, axis=-1, keepdims=True)) / D_MODEL
    inv = lax.rsqrt(var + LN_EPS)
    g = g_ref[...]
    b = b_ref[...]
    o_ref[:, :HALF] = d_lo * inv * g[:, :HALF] + b[:, :HALF]
    o_ref[:, HALF:] = d_hi * inv * g[:, HALF:] + b[:, HALF:]


def _combine(x1, yg, gates, g, b):
    t = x1.shape[0]
    tm = PROJ_TM
    const = lambda i: (0, 0)
    return pl.pallas_call(
        _combine_kernel,
        grid=(t // tm,),
        in_specs=[
            pl.BlockSpec((tm, D_MODEL), lambda i: (i, 0)),
            pl.BlockSpec((TOP_K, tm, HALF), lambda i: (0, i, 0)),
            pl.BlockSpec((tm, LANES), lambda i: (i, 0)),
            pl.BlockSpec((1, D_MODEL), const),
            pl.BlockSpec((1, D_MODEL), const),
        ],
        out_specs=pl.BlockSpec((tm, D_MODEL), lambda i: (i, 0)),
        out_shape=jax.ShapeDtypeStruct((t, D_MODEL), F32),
        compiler_params=_cparams(("parallel",)),
        name="combine",
    )(x1, yg, gates, g, b)


def _rope_tables(seq):
    half = ROPE_DIM // 2
    pos = jnp.arange(seq, dtype=F32)
    inv_freq = ROPE_THETA ** (-jnp.arange(0, ROPE_DIM, 2, dtype=F32) / ROPE_DIM)
    ang = pos[:, None] * inv_freq[None, :]
    cos, sin = jnp.cos(ang), jnp.sin(ang)
    zeros = jnp.zeros((seq, LANES - ROPE_DIM), F32)
    cos_f = jnp.concatenate([cos, cos, jnp.ones((seq, LANES - ROPE_DIM), F32)], axis=1)
    sin_a = jnp.concatenate([-sin, jnp.zeros((seq, half), F32), zeros], axis=1)
    sin_b = jnp.concatenate([jnp.zeros((seq, half), F32), sin, zeros], axis=1)
    return cos_f, sin_a, sin_b


def _prep_weights(w_in, gla_gate_w2, gla_gate_b, gla_norm_w, w_o, ln1_g, ln1_b, router_w, router_b,
                  w_gate_up, b_gate_up, w_down, b_down, ln2_g, ln2_b):
    sizes = (GLA_KEY, GLA_KEY, GLA_VAL, GLA_VAL, 2 * GLA_RANK, SWA_HEADS * SWA_HD,
             SWA_KV * SWA_HD, SWA_KV * SWA_HD, D_MODEL, D_MODEL)
    cols, acc = [], 0
    for s in sizes:
        cols.append(w_in[:, acc:acc + s])
        acc += s
    gq, gk, gv, gr, glr, sq, sk, sv, ga, gb = cols
    wa = jnp.concatenate([gq, gk, gv, gr], axis=1).astype(BF16)
    wl = jnp.pad(glr, ((0, 0), (0, LANES - 2 * GLA_RANK))).astype(BF16)
    wb = jnp.concatenate([ga, gb, sq, sk, sv], axis=1).astype(BF16)
    w2p = jnp.zeros((2, LANES, GLA_KEY), F32)
    w2p = w2p.at[0, :GLA_RANK].set(gla_gate_w2[0]).at[1, GLA_RANK:2 * GLA_RANK].set(gla_gate_w2[1])
    rw = jnp.pad(router_w, ((0, 0), (0, LANES - N_EXPERTS)))
    rb = jnp.concatenate([router_b, jnp.full((LANES - N_EXPERTS,), -jnp.inf, F32)]).reshape(1, LANES)
    return dict(
        wa=wa, wl=wl, wb=wb, w2p=w2p, bga=gla_gate_b.reshape(2, 1, GLA_KEY),
        nw=gla_norm_w.reshape(1, GLA_VAL), wo=w_o.astype(BF16),
        ln1_g=ln1_g.reshape(1, D_MODEL), ln1_b=ln1_b.reshape(1, D_MODEL), rw=rw, rb=rb,
        wgu=w_gate_up, bgu=b_gate_up.reshape(N_EXPERTS, 1, 2 * D_EXPERT),
        wd=w_down, bd=b_down.reshape(N_EXPERTS, 1, D_MODEL),
        ln2_g=ln2_g.reshape(1, D_MODEL), ln2_b=ln2_b.reshape(1, D_MODEL))


def _routing_plan(idx, rank, counts):
    t = idx.shape[0]
    bm = MOE_BM
    n_rows = t * TOP_K + N_EXPERTS * bm
    n_blocks = n_rows // bm
    counts = counts[0, :N_EXPERTS].astype(I32)
    padded = (counts + bm - 1) // bm * bm
    pad_ends = jnp.cumsum(padded)
    pad_starts = pad_ends - padded
    hot = idx[:, :, None] == jnp.arange(N_EXPERTS, dtype=I32)[None, None, :]
    dest = jnp.sum(jnp.where(hot, pad_starts[None, None, :], 0), axis=-1) + rank
    block_start = jnp.arange(n_blocks, dtype=I32) * bm
    block_expert = jnp.minimum(
        jnp.sum(block_start[:, None] >= pad_ends[None, :], axis=1), N_EXPERTS - 1).astype(I32)
    mine = block_expert[:, None] == jnp.arange(N_EXPERTS, dtype=I32)[None, :]
    seg_end = jnp.sum(jnp.where(mine, (pad_starts + counts)[None, :], 0), axis=1)
    n_valid = jnp.clip(seg_end - block_start, 0, bm).astype(I32)
    n_used = (pad_ends[-1] // bm).astype(I32).reshape(1)
    return dest.T.reshape(-1), n_rows, block_expert, n_used, n_valid


def _trunk(x, w, sinks, gather_rows, scatter_rows):
    bsz, seq, _ = x.shape
    t = bsz * seq
    x2d = x.reshape(t, D_MODEL)
    cos_f, sin_a, sin_b = _rope_tables(seq)
    pa, plr, pb = _in_proj(x2d, w["wa"], w["wl"], w["wb"], cos_f, sin_a, sin_b, seq)
    oa = _gla(pa, plr, w["w2p"], w["bga"], w["nw"], bsz, seq).reshape(t, GLA_VAL)
    ob = _swa(pb, sinks, bsz, seq).reshape(t, SWA_HEADS * SWA_HD)
    x1, xp, idx, gates, rank, counts = _merge(
        oa, ob, pb, x2d, w["wo"], w["ln1_g"], w["ln1_b"], w["rw"], w["rb"])
    dest_km, n_rows, block_expert, n_used, n_valid = _routing_plan(
        idx[:, :TOP_K], rank[:, :TOP_K], counts)
    xs = scatter_rows(xp, dest_km, n_rows)
    ys = _experts(xs, block_expert, n_used, n_valid, w["wgu"], w["bgu"], w["wd"], w["bd"])
    yg = gather_rows(ys, dest_km).reshape(TOP_K, t, HALF)
    out = _combine(x1, yg, gates, w["ln2_g"], w["ln2_b"])
    return out.reshape(bsz, seq, D_MODEL)


def _forward(x_prompt, x_sample, w_in, gla_gate_w2, gla_gate_b, gla_norm_w, swa_sinks, w_o, ln1_g,
             ln1_b, router_w, router_b, w_gate_up, b_gate_up, w_down, b_down, ln2_g, ln2_b):
    w = _prep_weights(w_in[0], gla_gate_w2[0], gla_gate_b[0], gla_norm_w[0], w_o[0], ln1_g[0],
                      ln1_b[0], router_w[0], router_b[0], w_gate_up[0], b_gate_up[0], w_down[0],
                      b_down[0], ln2_g[0], ln2_b[0])
    sinks = swa_sinks[0].astype(F32)
    return (_trunk(x_prompt, w, sinks, _gather_rows, _scatter_rows),
            _trunk(x_sample, w, sinks, _gather_rows, _scatter_rows))


def kernel(x_prompt, x_sample, w_in, gla_gate_w2, gla_gate_b, gla_norm_w, swa_sinks, w_o, ln1_g, ln1_b, router_w, router_b, w_gate_up, b_gate_up, w_down, b_down, ln2_g, ln2_b):
    return _forward(x_prompt, x_sample, w_in, gla_gate_w2, gla_gate_b, gla_norm_w, swa_sinks, w_o,
                    ln1_g, ln1_b, router_w, router_b, w_gate_up, b_gate_up, w_down, b_down,
                    ln2_g, ln2_b)
```

```python
import functools

import jax
import jax.numpy as jnp
from jax import lax
from jax.experimental import pallas as pl
from jax.experimental.pallas import tpu as pltpu
from jax.experimental.pallas import tpu_sc as plsc

F32 = jnp.float32
BF16 = jnp.bfloat16
I32 = jnp.int32
U32 = jnp.uint32

D_MODEL = 1024
GLA_HEADS = 4
GLA_DK = 128
GLA_DV = 256
GLA_KEY = GLA_HEADS * GLA_DK
GLA_VAL = GLA_HEADS * GLA_DV
GLA_RANK = 16
GLA_NORMALIZER = 16.0
GLA_CHUNK = 64
SWA_HEADS = 8
SWA_KV = 2
SWA_GROUP = SWA_HEADS // SWA_KV
SWA_HD = 128
SWA_WINDOW = 128
ROPE_THETA = 500000.0
ROPE_DIM = SWA_HD // 4
N_EXPERTS = 32
TOP_K = 4
D_EXPERT = D_MODEL
SWIGLU_LIMIT = 7.0
SWIGLU_ALPHA = 1.702
DEEPNORM_ALPHA = 2.0 ** 0.25
LN_EPS = 1e-5
RMS_EPS = 1e-6

LANES = 128
PA_W = 2 * GLA_KEY + 2 * GLA_VAL
PB_W = 2 * D_MODEL + SWA_HEADS * SWA_HD + 2 * SWA_KV * SWA_HD
HALF = D_MODEL // 2

PROJ_TM = 512
GLA_TS = 2048
GLA_HPS = 1
SWA_TQ = 512
MOE_BM = 512
SC_CHUNK = 64
VMEM_LIMIT = 56 * 1024 * 1024


def _cparams(sem):
    return pltpu.CompilerParams(dimension_semantics=sem, vmem_limit_bytes=VMEM_LIMIT)


def _split2(a):
    hi = a.astype(BF16)
    lo = (a - hi.astype(F32)).astype(BF16)
    return hi, lo


def _split3(a):
    hi = a.astype(BF16)
    r = a - hi.astype(F32)
    mid = r.astype(BF16)
    lo = (r - mid.astype(F32)).astype(BF16)
    return hi, mid, lo


def _dot(a, b):
    return jnp.dot(a, b, preferred_element_type=F32)


def _dot_nt(a, b):
    return lax.dot_general(a, b, (((1,), (1,)), ((), ())), preferred_element_type=F32)


def _dot_f32ish(a, b):
    ah, al = _split2(a)
    bh, bl = _split2(b)
    return _dot(ah, bh) + _dot(al, bh) + _dot(ah, bl)


def _sigmoid(x):
    return 0.5 * jnp.tanh(0.5 * x) + 0.5


def _pack_bf16_pair(lo_f32, hi_f32):
    lo = lax.bitcast_convert_type(lo_f32.astype(BF16).astype(F32), U32)
    hi = lax.bitcast_convert_type(hi_f32.astype(BF16).astype(F32), U32)
    return (hi & jnp.uint32(0xFFFF0000)) | (lo >> 16)


def _unpack_bf16_pair(w):
    lo = lax.bitcast_convert_type(w << 16, F32)
    hi = lax.bitcast_convert_type(w & jnp.uint32(0xFFFF0000), F32)
    return lo, hi


def _in_proj_kernel(x_ref, wa_ref, wl_ref, wb_ref, cos_ref, sina_ref, sinb_ref,
                    pa_ref, pl_ref, pb_ref):
    xb = x_ref[...].astype(BF16)
    step = 512
    for c0 in range(0, PA_W, step):
        r = _dot(xb, wa_ref[:, c0:c0 + step])
        if c0 < GLA_KEY:
            r = r * (GLA_DK ** -0.5)
        pa_ref[:, c0:c0 + step] = r.astype(BF16)
    pl_ref[...] = _dot(xb, wl_ref[...])
    cos = cos_ref[...]
    sina = sina_ref[...]
    sinb = sinb_ref[...]
    rope0 = 2 * D_MODEL
    rope_end = rope0 + (SWA_HEADS + SWA_KV) * SWA_HD
    for c0 in range(0, PB_W, step):
        r = _dot(xb, wb_ref[:, c0:c0 + step])
        if c0 + step > rope0 and c0 < rope_end:
            parts = []
            for j in range(step // SWA_HD):
                t = r[:, j * SWA_HD:(j + 1) * SWA_HD]
                col = c0 + j * SWA_HD
                if rope0 <= col < rope_end:
                    t = (t * cos + pltpu.roll(t, SWA_HD - ROPE_DIM // 2, 1) * sina
                         + pltpu.roll(t, ROPE_DIM // 2, 1) * sinb)
                if rope0 <= col < rope0 + SWA_HEADS * SWA_HD:
                    t = t * (SWA_HD ** -0.5)
                parts.append(t)
            r = jnp.concatenate(parts, axis=1)
        pb_ref[:, c0:c0 + step] = r.astype(BF16)


def _in_proj(x2d, wa, wl, wb, cos, sina, sinb, seq):
    t = x2d.shape[0]
    tm = PROJ_TM
    nseq = seq // tm
    const = lambda i: (0, 0)
    return pl.pallas_call(
        _in_proj_kernel,
        grid=(t // tm,),
        in_specs=[
            pl.BlockSpec((tm, D_MODEL), lambda i: (i, 0)),
            pl.BlockSpec((D_MODEL, PA_W), const),
            pl.BlockSpec((D_MODEL, LANES), const),
            pl.BlockSpec((D_MODEL, PB_W), const),
            pl.BlockSpec((tm, LANES), lambda i: (i % nseq, 0)),
            pl.BlockSpec((tm, LANES), lambda i: (i % nseq, 0)),
            pl.BlockSpec((tm, LANES), lambda i: (i % nseq, 0)),
        ],
        out_specs=[
            pl.BlockSpec((tm, PA_W), lambda i: (i, 0)),
            pl.BlockSpec((tm, LANES), lambda i: (i, 0)),
            pl.BlockSpec((tm, PB_W), lambda i: (i, 0)),
        ],
        out_shape=[
            jax.ShapeDtypeStruct((t, PA_W), BF16),
            jax.ShapeDtypeStruct((t, LANES), F32),
            jax.ShapeDtypeStruct((t, PB_W), BF16),
        ],
        compiler_params=_cparams(("parallel",)),
        name="in_proj",
    )(x2d, wa, wl, wb, cos, sina, sinb)


def _gla_kernel(q_ref, k_ref, v_ref, r_ref, lr_ref, w2_ref, bga_ref, nw_ref, o_ref,
                state_ref, ofwd_ref, sall_ref, ds_ref, dec_ref, *, ts, nt):
    ph = pl.program_id(2)
    t = pl.program_id(3)
    fwd = ph == 0
    c = GLA_CHUNK
    nc = ts // c
    tile = jnp.where(fwd, t, nt - 1 - t)

    @pl.when(t == 0)
    def _():
        state_ref[...] = jnp.zeros_like(state_ref)

    row = lax.broadcasted_iota(I32, (c, c), 0)
    col = lax.broadcasted_iota(I32, (c, c), 1)
    ahead = (col - row) * jnp.where(fwd, 1, -1)
    tri = (ahead <= 0).astype(BF16)
    keep = ahead <= jnp.where(fwd, 0, -1)
    lr_hi, lr_lo = _split2(lr_ref[0])

    def head_output(hh):
        dk = slice(hh * GLA_DK, (hh + 1) * GLA_DK)
        dv = slice(hh * GLA_DV, (hh + 1) * GLA_DV)
        q = q_ref[0, :, dk].astype(F32)
        k = k_ref[0, :, dk].astype(F32)
        v = v_ref[0, :, dv]
        w2_hi, w2_lo = _split2(w2_ref[0, :, dk])
        z = _dot(lr_hi, w2_hi) + _dot(lr_lo, w2_hi) + _dot(lr_hi, w2_lo) + bga_ref[0, :, dk]
        la = (jnp.minimum(z, 0.0) - jnp.log(1.0 + jnp.exp(-jnp.abs(z)))) * (1.0 / GLA_NORMALIZER)
        b_parts = []
        for ci in range(nc):
            hi, mid, lo = _split3(la[ci * c:(ci + 1) * c])
            b_parts.append(_dot(tri, hi) + _dot(tri, mid) + _dot(tri, lo))
        b3 = jnp.stack(b_parts)
        b_mid = jnp.where(fwd, b3[:, c // 2:c // 2 + 1], b3[:, c // 2 - 1:c // 2])
        b_end = jnp.where(fwd, b3[:, c - 1:c], b3[:, 0:1])
        q3 = q.reshape(nc, c, GLA_DK)
        k3 = k.reshape(nc, c, GLA_DK)
        v3 = v.reshape(nc, c, GLA_DV)
        q_mid = q3 * jnp.exp(b3 - b_mid)
        k_mid = k3 * jnp.exp(b_mid - b3)
        q_in = q_mid.astype(BF16)
        k_in = k_mid.astype(BF16)
        k_st = (k_mid * jnp.exp(b_end - b_mid)).astype(BF16)
        q_b = (q_mid * jnp.exp(b_mid)).astype(BF16)
        dec_ref[hh] = jnp.exp(b_end)
        scores = jnp.einsum("cid,cjd->cij", q_in, k_in, preferred_element_type=F32)
        scores = jnp.where(keep[None], scores, 0.0).astype(BF16)
        o_intra = jnp.einsum("cij,cjv->civ", scores, v3, preferred_element_type=F32)
        ds_ref[hh] = jnp.einsum("cjv,cjd->cvd", v3, k_st, preferred_element_type=F32)
        st = state_ref[hh]
        for i in range(nc):
            ci = jnp.where(fwd, i, nc - 1 - i)
            sall_ref[hh, ci] = st.astype(BF16)
            st = st * dec_ref[hh, ci] + ds_ref[hh, ci]
        state_ref[hh] = st
        o_inter = jnp.einsum("cid,cvd->civ", q_b, sall_ref[hh], preferred_element_type=F32)
        return (o_intra + o_inter).reshape(ts, GLA_DV)

    outs = [head_output(hh) for hh in range(GLA_HPS)]
    off = pl.multiple_of(tile * ts, ts)

    @pl.when(fwd)
    def _():
        for hh in range(GLA_HPS):
            ofwd_ref[hh, pl.ds(off, ts), :] = outs[hh]

    @pl.when(jnp.logical_not(fwd))
    def _():
        for hh in range(GLA_HPS):
            dv = slice(hh * GLA_DV, (hh + 1) * GLA_DV)
            tot = outs[hh] + ofwd_ref[hh, pl.ds(off, ts), :]
            tot = tot * lax.rsqrt(jnp.mean(tot * tot, axis=-1, keepdims=True) + RMS_EPS)
            r = r_ref[0, :, dv].astype(F32)
            o_ref[0, :, dv] = (tot * nw_ref[:, dv] * (r * _sigmoid(r))).astype(BF16)


def _gla(pa, plr, w2p, bga, nw, bsz, seq):
    ts = min(GLA_TS, seq)
    nt = seq // ts
    nc = ts // GLA_CHUNK
    hps = GLA_HPS
    pa3 = pa.reshape(bsz, seq, PA_W)
    lr3 = plr.reshape(bsz, seq, LANES)
    tile = lambda p, t: t + p * (nt - 1 - 2 * t)
    wk, wv = hps * GLA_DK, hps * GLA_DV
    kq = GLA_KEY // wk
    kv = 2 * GLA_KEY // wv
    kr = kv + GLA_VAL // wv
    kern = functools.partial(_gla_kernel, ts=ts, nt=nt)
    return pl.pallas_call(
        kern,
        grid=(bsz, GLA_HEADS // hps, 2, nt),
        in_specs=[
            pl.BlockSpec((1, ts, wk), lambda b, h, p, t: (b, tile(p, t), h)),
            pl.BlockSpec((1, ts, wk), lambda b, h, p, t: (b, tile(p, t), kq + h)),
            pl.BlockSpec((1, ts, wv), lambda b, h, p, t: (b, tile(p, t), kv + h)),
            pl.BlockSpec((1, ts, wv), lambda b, h, p, t: (b, nt - 1 - p * t, kr + h)),
            pl.BlockSpec((1, ts, LANES), lambda b, h, p, t: (b, tile(p, t), 0)),
            pl.BlockSpec((1, LANES, wk), lambda b, h, p, t: (p, 0, h)),
            pl.BlockSpec((1, 1, wk), lambda b, h, p, t: (p, 0, h)),
            pl.BlockSpec((1, wv), lambda b, h, p, t: (0, h)),
        ],
        out_specs=pl.BlockSpec((1, ts, wv), lambda b, h, p, t: (b, nt - 1 - p * t, h)),
        out_shape=jax.ShapeDtypeStruct((bsz, seq, GLA_VAL), BF16),
        scratch_shapes=[
            pltpu.VMEM((hps, GLA_DV, GLA_DK), F32),
            pltpu.VMEM((hps, seq, GLA_DV), F32),
            pltpu.VMEM((hps, nc, GLA_DV, GLA_DK), BF16),
            pltpu.VMEM((hps, nc, GLA_DV, GLA_DK), F32),
            pltpu.VMEM((hps, nc, 1, GLA_DK), F32),
        ],
        compiler_params=_cparams(("parallel", "parallel", "arbitrary", "arbitrary")),
        name="gla",
    )(pa3, pa3, pa3, pa3, lr3, w2p, bga, nw)


def _swa_kernel(sink_ref, q_ref, kl_ref, kc_ref, kr_ref, vl_ref, vc_ref, vr_ref, o_ref, *, tq):
    kvh = pl.program_id(1)
    n = pl.program_id(2)
    blk = SWA_WINDOW
    k_all = jnp.concatenate([kl_ref[0], kc_ref[0], kr_ref[0]], axis=0)
    v_all = jnp.concatenate([vl_ref[0], vc_ref[0], vr_ref[0]], axis=0)
    rows = SWA_GROUP * blk
    qi = lax.broadcasted_iota(I32, (rows, 3 * blk), 0) & (blk - 1)
    kj = lax.broadcasted_iota(I32, (rows, 3 * blk), 1) - blk
    band = jnp.where(jnp.abs(kj - qi) <= SWA_WINDOW, 0.0, -jnp.inf)
    head = lax.broadcasted_iota(I32, (rows, 1), 0) // blk
    sink = jnp.zeros((rows, 1), F32)
    for g in range(SWA_GROUP):
        sink = jnp.where(head == g, sink_ref[kvh * SWA_GROUP + g], sink)
    n_sub = tq // blk
    before = jnp.where(n == 0, -jnp.inf, 0.0)
    after = jnp.where(n == pl.num_programs(2) - 1, -jnp.inf, 0.0)
    for jq in range(n_sub):
        bias = band
        if jq == 0:
            bias = bias + jnp.where(kj < 0, before, 0.0)
        if jq == n_sub - 1:
            bias = bias + jnp.where(kj >= blk, after, 0.0)
        qs = q_ref[0, jq * blk:(jq + 1) * blk, :]
        q4 = jnp.concatenate([qs[:, g * SWA_HD:(g + 1) * SWA_HD] for g in range(SWA_GROUP)], axis=0)
        keys = k_all[jq * blk:(jq + 3) * blk]
        vals = v_all[jq * blk:(jq + 3) * blk]
        s = _dot_nt(q4, keys) + bias
        m = jnp.maximum(jnp.max(s, axis=-1, keepdims=True), sink)
        p = jnp.exp(s - m)
        denom = jnp.sum(p, axis=-1, keepdims=True) + jnp.exp(sink - m)
        o = _dot(p.astype(BF16), vals) * (1.0 / denom)
        for g in range(SWA_GROUP):
            o_ref[0, jq * blk:(jq + 1) * blk, g * SWA_HD:(g + 1) * SWA_HD] = (
                o[g * blk:(g + 1) * blk].astype(BF16))


def _swa(pb, sinks, bsz, seq):
    tq = min(SWA_TQ, seq)
    blk = SWA_WINDOW
    r = tq // blk
    nb = seq // blk
    pb3 = pb.reshape(bsz, seq, PB_W)
    gw = SWA_GROUP * SWA_HD
    cq = 2 * D_MODEL // gw
    ck = (2 * D_MODEL + SWA_HEADS * SWA_HD) // SWA_HD
    cv = ck + SWA_KV
    left = lambda n: jnp.maximum(n * r - 1, 0)
    right = lambda n: jnp.minimum(n * r + r, nb - 1)
    kern = functools.partial(_swa_kernel, tq=tq)
    grid_spec = pltpu.PrefetchScalarGridSpec(
        num_scalar_prefetch=1,
        grid=(bsz, SWA_KV, seq // tq),
        in_specs=[
            pl.BlockSpec((1, tq, gw), lambda b, h, n, s: (b, n, cq + h)),
            pl.BlockSpec((1, blk, SWA_HD), lambda b, h, n, s: (b, left(n), ck + h)),
            pl.BlockSpec((1, tq, SWA_HD), lambda b, h, n, s: (b, n, ck + h)),
            pl.BlockSpec((1, blk, SWA_HD), lambda b, h, n, s: (b, right(n), ck + h)),
            pl.BlockSpec((1, blk, SWA_HD), lambda b, h, n, s: (b, left(n), cv + h)),
            pl.BlockSpec((1, tq, SWA_HD), lambda b, h, n, s: (b, n, cv + h)),
            pl.BlockSpec((1, blk, SWA_HD), lambda b, h, n, s: (b, right(n), cv + h)),
        ],
        out_specs=pl.BlockSpec((1, tq, gw), lambda b, h, n, s: (b, n, h)),
    )
    return pl.pallas_call(
        kern,
        grid_spec=grid_spec,
        out_shape=jax.ShapeDtypeStruct((bsz, seq, SWA_HEADS * SWA_HD), BF16),
        compiler_params=_cparams(("parallel", "parallel", "parallel")),
        name="swa",
    )(sinks, pb3, pb3, pb3, pb3, pb3, pb3, pb3)


def _layer_norm(y, g, b):
    mu = jnp.mean(y, axis=-1, keepdims=True)
    d = y - mu
    var = jnp.mean(d * d, axis=-1, keepdims=True)
    return d * lax.rsqrt(var + LN_EPS) * g + b


def _merge_kernel(oa_ref, ob_ref, ga_ref, gb_ref, x_ref, wo_ref, g_ref, b_ref, rw_ref, rb_ref,
                  x1_ref, xp_ref, idx_ref, gate_ref, rank_ref, cnt_ref, run_ref, *, tm):
    @pl.when(pl.program_id(0) == 0)
    def _():
        run_ref[...] = jnp.zeros_like(run_ref)

    h = _sigmoid(ga_ref[...]) * oa_ref[...] + _sigmoid(gb_ref[...]) * ob_ref[...]
    y = DEEPNORM_ALPHA * x_ref[...] + _dot(h, wo_ref[...])
    x1 = _layer_norm(y, g_ref[...], b_ref[...])
    x1_ref[...] = x1
    xp_ref[...] = _pack_bf16_pair(x1[:, :HALF], x1[:, HALF:])

    logits = _dot_f32ish(x1, rw_ref[...]) + rb_ref[...]
    lane_i = lax.broadcasted_iota(I32, (tm, LANES), 1)
    lane = lane_i.astype(F32)
    work = logits
    vals, idxs, hots = [], [], []
    for _ in range(TOP_K):
        m = jnp.max(work, axis=-1, keepdims=True)
        sel = jnp.min(jnp.where(work == m, lane, float(LANES)), axis=-1, keepdims=True)
        hot = lane == sel
        vals.append(m)
        idxs.append(sel)
        hots.append(hot)
        work = jnp.where(hot, -jnp.inf, work)
    e = [jnp.exp(v - vals[0]) for v in vals]
    tot = e[0] + e[1] + e[2] + e[3]

    def spread(cols):
        out = jnp.zeros((tm, LANES), F32)
        for kk, col in enumerate(cols):
            out = jnp.where(lane_i == kk, col, out)
        return out

    gate_ref[...] = spread([ei / tot for ei in e])
    idx_ref[...] = spread(idxs).astype(I32)

    member = (hots[0] | hots[1] | hots[2] | hots[3])
    member_f = member.astype(F32)
    r_i = lax.broadcasted_iota(I32, (tm, tm), 0)
    c_i = lax.broadcasted_iota(I32, (tm, tm), 1)
    before = (c_i < r_i).astype(BF16)
    cum = _dot(before, member.astype(BF16)) + run_ref[...]
    ranks = [jnp.sum(jnp.where(hot, cum, 0.0), axis=-1, keepdims=True) for hot in hots]
    rank_ref[...] = spread(ranks).astype(I32)
    run = run_ref[...] + jnp.sum(member_f, axis=0, keepdims=True)
    run_ref[...] = run
    cnt_ref[...] = run


def _merge(oa, ob, pb, x2d, wo, g, b, rw, rb):
    t = x2d.shape[0]
    tm = PROJ_TM
    const = lambda i: (0, 0)
    kern = functools.partial(_merge_kernel, tm=tm)
    return pl.pallas_call(
        kern,
        grid=(t // tm,),
        in_specs=[
            pl.BlockSpec((tm, D_MODEL), lambda i: (i, 0)),
            pl.BlockSpec((tm, D_MODEL), lambda i: (i, 0)),
            pl.BlockSpec((tm, D_MODEL), lambda i: (i, 0)),
            pl.BlockSpec((tm, D_MODEL), lambda i: (i, 1)),
            pl.BlockSpec((tm, D_MODEL), lambda i: (i, 0)),
            pl.BlockSpec((D_MODEL, D_MODEL), const),
            pl.BlockSpec((1, D_MODEL), const),
            pl.BlockSpec((1, D_MODEL), const),
            pl.BlockSpec((D_MODEL, LANES), const),
            pl.BlockSpec((1, LANES), const),
        ],
        out_specs=[
            pl.BlockSpec((tm, D_MODEL), lambda i: (i, 0)),
            pl.BlockSpec((tm, HALF), lambda i: (i, 0)),
            pl.BlockSpec((tm, LANES), lambda i: (i, 0)),
            pl.BlockSpec((tm, LANES), lambda i: (i, 0)),
            pl.BlockSpec((tm, LANES), lambda i: (i, 0)),
            pl.BlockSpec((1, LANES), const),
        ],
        out_shape=[
            jax.ShapeDtypeStruct((t, D_MODEL), F32),
            jax.ShapeDtypeStruct((t, HALF), U32),
            jax.ShapeDtypeStruct((t, LANES), I32),
            jax.ShapeDtypeStruct((t, LANES), F32),
            jax.ShapeDtypeStruct((t, LANES), I32),
            jax.ShapeDtypeStruct((1, LANES), F32),
        ],
        scratch_shapes=[pltpu.VMEM((1, LANES), F32)],
        compiler_params=_cparams(("arbitrary",)),
        name="merge",
    )(oa, ob, pb, pb, x2d, wo, g, b, rw, rb)


def _gather_rows(table, idx):
    n = idx.shape[0]
    d = table.shape[1]
    info = plsc.get_sparse_core_info()
    nw = info.num_cores * info.num_subcores
    per_w = n // nw
    assert per_w * nw == n and per_w % SC_CHUNK == 0
    n_chunks = per_w // SC_CHUNK
    mesh = plsc.VectorSubcoreMesh(core_axis_name="c", subcore_axis_name="s")

    @functools.partial(
        pl.kernel, mesh=mesh,
        out_type=jax.ShapeDtypeStruct((n, d), table.dtype),
        scratch_types=[
            pltpu.VMEM((SC_CHUNK,), I32),
            pltpu.VMEM((SC_CHUNK, d), table.dtype),
            pltpu.SemaphoreType.DMA,
        ],
    )
    def k(table_hbm, idx_hbm, out_hbm, idx_v, rows_v, sem):
        wid = lax.axis_index("s") * info.num_cores + lax.axis_index("c")
        base = wid * per_w

        @pl.loop(0, n_chunks)
        def _(j):
            off = pl.multiple_of(base + j * SC_CHUNK, 8)
            pltpu.sync_copy(idx_hbm.at[pl.ds(off, SC_CHUNK)], idx_v)
            pltpu.async_copy(table_hbm.at[idx_v], rows_v, sem).wait()
            pltpu.sync_copy(rows_v, out_hbm.at[pl.ds(off, SC_CHUNK)])

    return k(table, idx)


def _scatter_rows(src, dest_km, n_rows):
    t, d = src.shape
    info = plsc.get_sparse_core_info()
    nw = info.num_cores * info.num_subcores
    per_w = t // nw
    assert per_w * nw == t and per_w % SC_CHUNK == 0
    n_chunks = per_w // SC_CHUNK
    mesh = plsc.VectorSubcoreMesh(core_axis_name="c", subcore_axis_name="s")

    @functools.partial(
        pl.kernel, mesh=mesh,
        out_type=jax.ShapeDtypeStruct((n_rows, d), src.dtype),
        scratch_types=[
            pltpu.VMEM((SC_CHUNK,), I32),
            pltpu.VMEM((SC_CHUNK, d), src.dtype),
            pltpu.SemaphoreType.DMA,
        ],
    )
    def k(src_hbm, dest_hbm, out_hbm, idx_v, rows_v, sem):
        wid = lax.axis_index("s") * info.num_cores + lax.axis_index("c")
        base = wid * per_w

        @pl.loop(0, n_chunks)
        def _(j):
            off = pl.multiple_of(base + j * SC_CHUNK, 8)
            pltpu.sync_copy(src_hbm.at[pl.ds(off, SC_CHUNK)], rows_v)
            for kk in range(TOP_K):
                pltpu.sync_copy(dest_hbm.at[pl.ds(pl.multiple_of(kk * t + off, 8), SC_CHUNK)], idx_v)
                pltpu.async_copy(rows_v, out_hbm.at[idx_v], sem).wait()

    return k(src, dest_km)


def _expert_kernel(be_ref, nused_ref, nvalid_ref, xs_ref, wgu_ref, bgu_ref, wd_ref, bd_ref, ys_ref,
                   wgu_bf, wd_bf):
    i = pl.program_id(0)

    @pl.when(i < nused_ref[0])
    def _():
        @pl.when((i == 0) | (be_ref[i] != be_ref[jnp.maximum(i - 1, 0)]))
        def _():
            rows = 256
            for r0 in range(0, D_MODEL, rows):
                wgu_bf[r0:r0 + rows, :] = wgu_ref[0, r0:r0 + rows, :].astype(BF16)
            for r0 in range(0, D_EXPERT, rows):
                wd_bf[r0:r0 + rows, :] = wd_ref[0, r0:r0 + rows, :].astype(BF16)

        rowid = lax.broadcasted_iota(I32, (MOE_BM, 1), 0)
        words = jnp.where(rowid < nvalid_ref[i], xs_ref[...], jnp.uint32(0))
        lo, hi = _unpack_bf16_pair(words)
        lo = lo.astype(BF16)
        hi = hi.astype(BF16)
        acc = None
        step = 1024
        for c0 in range(0, D_EXPERT, step):
            def proj(col):
                return (_dot(lo, wgu_bf[:HALF, col:col + step])
                        + _dot(hi, wgu_bf[HALF:, col:col + step])
                        + bgu_ref[0, :, col:col + step])
            gate = jnp.minimum(proj(c0), SWIGLU_LIMIT)
            up = jnp.clip(proj(D_EXPERT + c0), -SWIGLU_LIMIT, SWIGLU_LIMIT)
            act = ((up + 1.0) * gate * _sigmoid(gate * SWIGLU_ALPHA)).astype(BF16)
            part = _dot(act, wd_bf[c0:c0 + step, :])
            acc = part if acc is None else acc + part
        y = acc + bd_ref[0]
        ys_ref[...] = _pack_bf16_pair(y[:, :HALF], y[:, HALF:])


def _experts(xs, block_expert, n_used, n_valid, wgu, bgu, wd, bd):
    n_rows = xs.shape[0]
    bm = MOE_BM
    n_blocks = n_rows // bm
    rows = lambda i, be, nu, nv: (jnp.minimum(i, nu[0] - 1), 0)
    expert = lambda i, be, nu, nv: (be[i], 0, 0)
    grid_spec = pltpu.PrefetchScalarGridSpec(
        num_scalar_prefetch=3,
        grid=(n_blocks,),
        in_specs=[
            pl.BlockSpec((bm, HALF), rows),
            pl.BlockSpec((1, D_MODEL, 2 * D_EXPERT), expert),
            pl.BlockSpec((1, 1, 2 * D_EXPERT), expert),
            pl.BlockSpec((1, D_EXPERT, D_MODEL), expert),
            pl.BlockSpec((1, 1, D_MODEL), expert),
        ],
        out_specs=pl.BlockSpec((bm, HALF), rows),
        scratch_shapes=[
            pltpu.VMEM((D_MODEL, 2 * D_EXPERT), BF16),
            pltpu.VMEM((D_EXPERT, D_MODEL), BF16),
        ],
    )
    return pl.pallas_call(
        _expert_kernel,
        grid_spec=grid_spec,
        out_shape=jax.ShapeDtypeStruct((n_rows, HALF), U32),
        compiler_params=_cparams(("arbitrary",)),
        name="experts",
    )(block_expert, n_used, n_valid, xs, wgu, bgu, wd, bd)


def _combine_kernel(x1_ref, yg_ref, gate_ref, g_ref, b_ref, o_ref):
    gates = gate_ref[...]
    f_lo = None
    f_hi = None
    for kk in range(TOP_K):
        lo, hi = _unpack_bf16_pair(yg_ref[kk])
        w = gates[:, kk:kk + 1]
        f_lo = lo * w if f_lo is None else f_lo + lo * w
        f_hi = hi * w if f_hi is None else f_hi + hi * w
    x1 = x1_ref[...]
    y_lo = DEEPNORM_ALPHA * x1[:, :HALF] + f_lo
    y_hi = DEEPNORM_ALPHA * x1[:, HALF:] + f_hi
    mu = (jnp.sum(y_lo, axis=-1, keepdims=True) + jnp.sum(y_hi, axis=-1, keepdims=True)) / D_MODEL
    d_lo = y_lo - mu
    d_hi = y_hi - mu
    var = (jnp.sum(d_lo * d_lo, axis=-1, keepdims=True)
           + jnp.sum(d_hi * d_hi, axis=-1, keepdims=True)) / D_MODEL
    inv = lax.rsqrt(var + LN_EPS)
    g = g_ref[...]
    b = b_ref[...]
    o_ref[:, :HALF] = d_lo * inv * g[:, :HALF] + b[:, :HALF]
    o_ref[:, HALF:] = d_hi * inv * g[:, HALF:] + b[:, HALF:]


def _combine(x1, yg, gates, g, b):
    t = x1.shape[0]
    tm = PROJ_TM
    const = lambda i: (0, 0)
    return pl.pallas_call(
        _combine_kernel,
        grid=(t // tm,),
        in_specs=[
            pl.BlockSpec((tm, D_MODEL), lambda i: (i, 0)),
            pl.BlockSpec((TOP_K, tm, HALF), lambda i: (0, i, 0)),
            pl.BlockSpec((tm, LANES), lambda i: (i, 0)),
            pl.BlockSpec((1, D_MODEL), const),
            pl.BlockSpec((1, D_MODEL), const),
        ],
        out_specs=pl.BlockSpec((tm, D_MODEL), lambda i: (i, 0)),
        out_shape=jax.ShapeDtypeStruct((t, D_MODEL), F32),
        compiler_params=_cparams(("parallel",)),
        name="combine",
    )(x1, yg, gates, g, b)


def _rope_tables(seq):
    half = ROPE_DIM // 2
    pos = jnp.arange(seq, dtype=F32)
    inv_freq = ROPE_THETA ** (-jnp.arange(0, ROPE_DIM, 2, dtype=F32) / ROPE_DIM)
    ang = pos[:, None] * inv_freq[None, :]
    cos, sin = jnp.cos(ang), jnp.sin(ang)
    zeros = jnp.zeros((seq, LANES - ROPE_DIM), F32)
    cos_f = jnp.concatenate([cos, cos, jnp.ones((seq, LANES - ROPE_DIM), F32)], axis=1)
    sin_a = jnp.concatenate([-sin, jnp.zeros((seq, half), F32), zeros], axis=1)
    sin_b = jnp.concatenate([jnp.zeros((seq, half), F32), sin, zeros], axis=1)
    return cos_f, sin_a, sin_b


def _prep_weights(w_in, gla_gate_w2, gla_gate_b, gla_norm_w, w_o, ln1_g, ln1_b, router_w, router_b,
                  w_gate_up, b_gate_up, w_down, b_down, ln2_g, ln2_b):
    sizes = (GLA_KEY, GLA_KEY, GLA_VAL, GLA_VAL, 2 * GLA_RANK, SWA_HEADS * SWA_HD,
             SWA_KV * SWA_HD, SWA_KV * SWA_HD, D_MODEL, D_MODEL)
    cols, acc = [], 0
    for s in sizes:
        cols.append(w_in[:, acc:acc + s])
        acc += s
    gq, gk, gv, gr, glr, sq, sk, sv, ga, gb = cols
    wa = jnp.concatenate([gq, gk, gv, gr], axis=1).astype(BF16)
    wl = jnp.pad(glr, ((0, 0), (0, LANES - 2 * GLA_RANK))).astype(BF16)
    wb = jnp.concatenate([ga, gb, sq, sk, sv], axis=1).astype(BF16)
    w2p = jnp.zeros((2, LANES, GLA_KEY), F32)
    w2p = w2p.at[0, :GLA_RANK].set(gla_gate_w2[0]).at[1, GLA_RANK:2 * GLA_RANK].set(gla_gate_w2[1])
    rw = jnp.pad(router_w, ((0, 0), (0, LANES - N_EXPERTS)))
    rb = jnp.concatenate([router_b, jnp.full((LANES - N_EXPERTS,), -jnp.inf, F32)]).reshape(1, LANES)
    return dict(
        wa=wa, wl=wl, wb=wb, w2p=w2p, bga=gla_gate_b.reshape(2, 1, GLA_KEY),
        nw=gla_norm_w.reshape(1, GLA_VAL), wo=w_o.astype(BF16),
        ln1_g=ln1_g.reshape(1, D_MODEL), ln1_b=ln1_b.reshape(1, D_MODEL), rw=rw, rb=rb,
        wgu=w_gate_up, bgu=b_gate_up.reshape(N_EXPERTS, 1, 2 * D_EXPERT),
        wd=w_down, bd=b_down.reshape(N_EXPERTS, 1, D_MODEL),
        ln2_g=ln2_g.reshape(1, D_MODEL), ln2_b=ln2_b.reshape(1, D_MODEL))


def _routing_plan(idx, rank, counts):
    t = idx.shape[0]
    bm = MOE_BM
    n_rows = t * TOP_K + N_EXPERTS * bm
    n_blocks = n_rows // bm
    counts = counts[0, :N_EXPERTS].astype(I32)
    padded = (counts + bm - 1) // bm * bm
    pad_ends = jnp.cumsum(padded)
    pad_starts = pad_ends - padded
    hot = idx[:, :, None] == jnp.arange(N_EXPERTS, dtype=I32)[None, None, :]
    dest = jnp.sum(jnp.where(hot, pad_starts[None, None, :], 0), axis=-1) + rank
    block_start = jnp.arange(n_blocks, dtype=I32) * bm
    block_expert = jnp.minimum(
        jnp.sum(block_start[:, None] >= pad_ends[None, :], axis=1), N_EXPERTS - 1).astype(I32)
    mine = block_expert[:, None] == jnp.arange(N_EXPERTS, dtype=I32)[None, :]
    seg_end = jnp.sum(jnp.where(mine, (pad_starts + counts)[None, :], 0), axis=1)
    n_valid = jnp.clip(seg_end - block_start, 0, bm).astype(I32)
    n_used = (pad_ends[-1] // bm).astype(I32).reshape(1)
    return dest.T.reshape(-1), n_rows, block_expert, n_used, n_valid


def _trunk(x, w, sinks, gather_rows, scatter_rows):
    bsz, seq, _ = x.shape
    t = bsz * seq
    x2d = x.reshape(t, D_MODEL)
    cos_f, sin_a, sin_b = _rope_tables(seq)
    pa, plr, pb = _in_proj(x2d, w["wa"], w["wl"], w["wb"], cos_f, sin_a, sin_b, seq)
    oa = _gla(pa, plr, w["w2p"], w["bga"], w["nw"], bsz, seq).reshape(t, GLA_VAL)
    ob = _swa(pb, sinks, bsz, seq).reshape(t, SWA_HEADS * SWA_HD)
    x1, xp, idx, gates, rank, counts = _merge(
        oa, ob, pb, x2d, w["wo"], w["ln1_g"], w["ln1_b"], w["rw"], w["rb"])
    dest_km, n_rows, block_expert, n_used, n_valid = _routing_plan(
        idx[:, :TOP_K], rank[:, :TOP_K], counts)
    xs = scatter_rows(xp, dest_km, n_rows)
    ys = _experts(xs, block_expert, n_used, n_valid, w["wgu"], w["bgu"], w["wd"], w["bd"])
    yg = gather_rows(ys, dest_km).reshape(TOP_K, t, HALF)
    out = _combine(x1, yg, gates, w["ln2_g"], w["ln2_b"])
    return out.reshape(bsz, seq, D_MODEL)


def _forward(x_prompt, x_sample, w_in, gla_gate_w2, gla_gate_b, gla_norm_w, swa_sinks, w_o, ln1_g,
             ln1_b, router_w, router_b, w_gate_up, b_gate_up, w_down, b_down, ln2_g, ln2_b):
    w = _prep_weights(w_in[0], gla_gate_w2[0], gla_gate_b[0], gla_norm_w[0], w_o[0], ln1_g[0],
                      ln1_b[0], router_w[0], router_b[0], w_gate_up[0], b_gate_up[0], w_down[0],
                      b_down[0], ln2_g[0], ln2_b[0])
    sinks = swa_sinks[0].astype(F32)
    return (_trunk(x_prompt, w, sinks, _gather_rows, _scatter_rows),
            _trunk(x_sample, w, sinks, _gather_rows, _scatter_rows))


def kernel(x_prompt, x_sample, w_in, gla_gate_w2, gla_gate_b, gla_norm_w, swa_sinks, w_o, ln1_g, ln1_b, router_w, router_b, w_gate_up, b_gate_up, w_down, b_down, ln2_g, ln2_b):
    return _forward(x_prompt, x_sample, w_in, gla_gate_w2, gla_gate_b, gla_norm_w, swa_sinks, w_o,
                    ln1_g, ln1_b, router_w, router_b, w_gate_up, b_gate_up, w_down, b_down,
                    ln2_g, ln2_b)
```

```python
import functools

import jax
import jax.numpy as jnp
from jax import lax
from jax.experimental import pallas as pl
from jax.experimental.pallas import tpu as pltpu
from jax.experimental.pallas import tpu_sc as plsc

F32 = jnp.float32
BF16 = jnp.bfloat16
I32 = jnp.int32
U32 = jnp.uint32

D_MODEL = 1024
GLA_HEADS = 4
GLA_DK = 128
GLA_DV = 256
GLA_KEY = GLA_HEADS * GLA_DK
GLA_VAL = GLA_HEADS * GLA_DV
GLA_RANK = 16
GLA_NORMALIZER = 16.0
GLA_CHUNK = 64
SWA_HEADS = 8
SWA_KV = 2
SWA_GROUP = SWA_HEADS // SWA_KV
SWA_HD = 128
SWA_WINDOW = 128
ROPE_THETA = 500000.0
ROPE_DIM = SWA_HD // 4
N_EXPERTS = 32
TOP_K = 4
D_EXPERT = D_MODEL
SWIGLU_LIMIT = 7.0
SWIGLU_ALPHA = 1.702
DEEPNORM_ALPHA = 2.0 ** 0.25
LN_EPS = 1e-5
RMS_EPS = 1e-6

LANES = 128
PA_W = 2 * GLA_KEY + 2 * GLA_VAL
PB_W = 2 * D_MODEL + SWA_HEADS * SWA_HD + 2 * SWA_KV * SWA_HD
HALF = D_MODEL // 2

PROJ_TM = 512
GLA_TS = 2048
GLA_HPS = 1
SWA_TQ = 512
MOE_BM = 512
SC_CHUNK = 64
VMEM_LIMIT = 56 * 1024 * 1024


def _cparams(sem):
    return pltpu.CompilerParams(dimension_semantics=sem, vmem_limit_bytes=VMEM_LIMIT)


def _split2(a):
    hi = a.astype(BF16)
    lo = (a - hi.astype(F32)).astype(BF16)
    return hi, lo


def _split3(a):
    hi = a.astype(BF16)
    r = a - hi.astype(F32)
    mid = r.astype(BF16)
    lo = (r - mid.astype(F32)).astype(BF16)
    return hi, mid, lo


def _dot(a, b):
    return jnp.dot(a, b, preferred_element_type=F32)


def _dot_nt(a, b):
    return lax.dot_general(a, b, (((1,), (1,)), ((), ())), preferred_element_type=F32)


def _dot_f32ish(a, b):
    ah, al = _split2(a)
    bh, bl = _split2(b)
    return _dot(ah, bh) + _dot(al, bh) + _dot(ah, bl)


def _sigmoid(x):
    return 0.5 * jnp.tanh(0.5 * x) + 0.5


def _pack_bf16_pair(lo_f32, hi_f32):
    lo = lax.bitcast_convert_type(lo_f32.astype(BF16).astype(F32), U32)
    hi = lax.bitcast_convert_type(hi_f32.astype(BF16).astype(F32), U32)
    return (hi & jnp.uint32(0xFFFF0000)) | (lo >> 16)


def _unpack_bf16_pair(w):
    lo = lax.bitcast_convert_type(w << 16, F32)
    hi = lax.bitcast_convert_type(w & jnp.uint32(0xFFFF0000), F32)
    return lo, hi


def _chunk_cumsum(x, reverse):
    rows = x.shape[0]
    pos = lax.broadcasted_iota(I32, x.shape, 0) & (GLA_CHUNK - 1)
    s = 1
    while s < GLA_CHUNK:
        if reverse:
            x = x + jnp.where(pos < GLA_CHUNK - s, pltpu.roll(x, rows - s, 0), 0.0)
        else:
            x = x + jnp.where(pos >= s, pltpu.roll(x, s, 0), 0.0)
        s *= 2
    return x


def _in_proj_kernel(x_ref, wa_ref, wl_ref, w2_ref, bga_ref, wb_ref, cos_ref, sina_ref, sinb_ref,
                    pa_ref, bd_ref, pb_ref):
    xb = x_ref[...].astype(BF16)
    lrb = _dot(xb, wl_ref[...]).astype(BF16)

    def decay_slab(j):
        cols = slice(j * LANES, (j + 1) * LANES)
        z = _dot(lrb, w2_ref[:, cols]) + bga_ref[:, cols]
        la = (jnp.minimum(z, 0.0) - jnp.log(1.0 + jnp.exp(-jnp.abs(z)))) * (1.0 / GLA_NORMALIZER)
        bd_ref[:, cols] = _chunk_cumsum(la, reverse=j * LANES >= GLA_KEY)

    n_slabs = 2 * GLA_KEY // LANES
    slab = 0
    step = 512
    for c0 in range(0, PA_W, step):
        r = _dot(xb, wa_ref[:, c0:c0 + step])
        if c0 < GLA_KEY:
            r = r * (GLA_DK ** -0.5)
        pa_ref[:, c0:c0 + step] = r.astype(BF16)
        if slab < n_slabs:
            decay_slab(slab)
            slab += 1
    cos = cos_ref[...]
    sina = sina_ref[...]
    sinb = sinb_ref[...]
    rope0 = 2 * D_MODEL
    rope_end = rope0 + (SWA_HEADS + SWA_KV) * SWA_HD
    for c0 in range(0, PB_W, step):
        r = _dot(xb, wb_ref[:, c0:c0 + step])
        if c0 + step > rope0 and c0 < rope_end:
            parts = []
            for j in range(step // SWA_HD):
                t = r[:, j * SWA_HD:(j + 1) * SWA_HD]
                col = c0 + j * SWA_HD
                if rope0 <= col < rope_end:
                    t = (t * cos + pltpu.roll(t, SWA_HD - ROPE_DIM // 2, 1) * sina
                         + pltpu.roll(t, ROPE_DIM // 2, 1) * sinb)
                if rope0 <= col < rope0 + SWA_HEADS * SWA_HD:
                    t = t * (SWA_HD ** -0.5)
                parts.append(t)
            r = jnp.concatenate(parts, axis=1)
        pb_ref[:, c0:c0 + step] = r.astype(BF16)
        if slab < n_slabs:
            decay_slab(slab)
            slab += 1
    assert slab == n_slabs


def _in_proj(x2d, wa, wl, w2f, bgaf, wb, cos, sina, sinb, seq):
    t = x2d.shape[0]
    tm = PROJ_TM
    nseq = seq // tm
    const = lambda i: (0, 0)
    return pl.pallas_call(
        _in_proj_kernel,
        grid=(t // tm,),
        in_specs=[
            pl.BlockSpec((tm, D_MODEL), lambda i: (i, 0)),
            pl.BlockSpec((D_MODEL, PA_W), const),
            pl.BlockSpec((D_MODEL, LANES), const),
            pl.BlockSpec((LANES, 2 * GLA_KEY), const),
            pl.BlockSpec((1, 2 * GLA_KEY), const),
            pl.BlockSpec((D_MODEL, PB_W), const),
            pl.BlockSpec((tm, LANES), lambda i: (i % nseq, 0)),
            pl.BlockSpec((tm, LANES), lambda i: (i % nseq, 0)),
            pl.BlockSpec((tm, LANES), lambda i: (i % nseq, 0)),
        ],
        out_specs=[
            pl.BlockSpec((tm, PA_W), lambda i: (i, 0)),
            pl.BlockSpec((tm, 2 * GLA_KEY), lambda i: (i, 0)),
            pl.BlockSpec((tm, PB_W), lambda i: (i, 0)),
        ],
        out_shape=[
            jax.ShapeDtypeStruct((t, PA_W), BF16),
            jax.ShapeDtypeStruct((t, 2 * GLA_KEY), F32),
            jax.ShapeDtypeStruct((t, PB_W), BF16),
        ],
        compiler_params=_cparams(("parallel",)),
        name="in_proj",
    )(x2d, wa, wl, w2f, bgaf, wb, cos, sina, sinb)


def _gla_kernel(q_ref, k_ref, v_ref, r_ref, b_ref, nw_ref, o_ref,
                state_ref, ofwd_ref, sall_ref, ds_ref, dec_ref, *, ts, nt):
    ph = pl.program_id(2)
    t = pl.program_id(3)
    fwd = ph == 0
    c = GLA_CHUNK
    nc = ts // c
    tile = jnp.where(fwd, t, nt - 1 - t)

    @pl.when(t == 0)
    def _():
        state_ref[...] = jnp.zeros_like(state_ref)

    row = lax.broadcasted_iota(I32, (c, c), 0)
    col = lax.broadcasted_iota(I32, (c, c), 1)
    ahead = (col - row) * jnp.where(fwd, 1, -1)
    keep = ahead <= jnp.where(fwd, 0, -1)

    def head_output(hh):
        dk = slice(hh * GLA_DK, (hh + 1) * GLA_DK)
        dv = slice(hh * GLA_DV, (hh + 1) * GLA_DV)
        q3 = q_ref[0, :, dk].reshape(nc, c, GLA_DK)
        k3 = k_ref[0, :, dk].reshape(nc, c, GLA_DK)
        v3 = v_ref[0, :, dv].reshape(nc, c, GLA_DV)
        b3 = b_ref[0, :, dk].reshape(nc, c, GLA_DK)
        b_mid = jnp.where(fwd, b3[:, c // 2:c // 2 + 1], b3[:, c // 2 - 1:c // 2])
        b_end = jnp.where(fwd, b3[:, c - 1:c], b3[:, 0:1])
        q_in = q3 * jnp.exp(b3 - b_mid).astype(BF16)
        k_in = k3 * jnp.exp(b_mid - b3).astype(BF16)
        k_st = k_in * jnp.exp(b_end - b_mid).astype(BF16)
        q_b = q_in * jnp.exp(b_mid).astype(BF16)
        dec_ref[hh] = jnp.exp(b_end)
        scores = jnp.einsum("cid,cjd->cij", q_in, k_in, preferred_element_type=F32)
        scores = jnp.where(keep[None], scores, 0.0).astype(BF16)
        o_intra = jnp.einsum("cij,cjv->civ", scores, v3, preferred_element_type=F32)
        ds_ref[hh] = jnp.einsum("cjv,cjd->cvd", v3, k_st, preferred_element_type=F32)
        st = state_ref[hh]
        for i in range(nc):
            ci = jnp.where(fwd, i, nc - 1 - i)
            sall_ref[hh, ci] = st.astype(BF16)
            st = st * dec_ref[hh, ci] + ds_ref[hh, ci]
        state_ref[hh] = st
        o_inter = jnp.einsum("cid,cvd->civ", q_b, sall_ref[hh], preferred_element_type=F32)
        return (o_intra + o_inter).reshape(ts, GLA_DV)

    outs = [head_output(hh) for hh in range(GLA_HPS)]
    off = pl.multiple_of(tile * ts, ts)

    @pl.when(fwd)
    def _():
        for hh in range(GLA_HPS):
            ofwd_ref[hh, pl.ds(off, ts), :] = outs[hh]

    @pl.when(jnp.logical_not(fwd))
    def _():
        for hh in range(GLA_HPS):
            dv = slice(hh * GLA_DV, (hh + 1) * GLA_DV)
            tot = outs[hh] + ofwd_ref[hh, pl.ds(off, ts), :]
            tot = tot * lax.rsqrt(jnp.mean(tot * tot, axis=-1, keepdims=True) + RMS_EPS)
            r = r_ref[0, :, dv].astype(F32)
            o_ref[0, :, dv] = (tot * nw_ref[:, dv] * (r * _sigmoid(r))).astype(BF16)


def _gla(pa, bdec, nw, bsz, seq):
    ts = min(GLA_TS, seq)
    nt = seq // ts
    nc = ts // GLA_CHUNK
    hps = GLA_HPS
    pa3 = pa.reshape(bsz, seq, PA_W)
    bd3 = bdec.reshape(bsz, seq, 2 * GLA_KEY)
    tile = lambda p, t: t + p * (nt - 1 - 2 * t)
    wk, wv = hps * GLA_DK, hps * GLA_DV
    kq = GLA_KEY // wk
    kv = 2 * GLA_KEY // wv
    kr = kv + GLA_VAL // wv
    kern = functools.partial(_gla_kernel, ts=ts, nt=nt)
    return pl.pallas_call(
        kern,
        grid=(bsz, GLA_HEADS // hps, 2, nt),
        in_specs=[
            pl.BlockSpec((1, ts, wk), lambda b, h, p, t: (b, tile(p, t), h)),
            pl.BlockSpec((1, ts, wk), lambda b, h, p, t: (b, tile(p, t), kq + h)),
            pl.BlockSpec((1, ts, wv), lambda b, h, p, t: (b, tile(p, t), kv + h)),
            pl.BlockSpec((1, ts, wv), lambda b, h, p, t: (b, nt - 1 - p * t, kr + h)),
            pl.BlockSpec((1, ts, wk), lambda b, h, p, t: (b, tile(p, t), p * kq + h)),
            pl.BlockSpec((1, wv), lambda b, h, p, t: (0, h)),
        ],
        out_specs=pl.BlockSpec((1, ts, wv), lambda b, h, p, t: (b, nt - 1 - p * t, h)),
        out_shape=jax.ShapeDtypeStruct((bsz, seq, GLA_VAL), BF16),
        scratch_shapes=[
            pltpu.VMEM((hps, GLA_DV, GLA_DK), F32),
            pltpu.VMEM((hps, seq, GLA_DV), F32),
            pltpu.VMEM((hps, nc, GLA_DV, GLA_DK), BF16),
            pltpu.VMEM((hps, nc, GLA_DV, GLA_DK), F32),
            pltpu.VMEM((hps, nc, 1, GLA_DK), F32),
        ],
        compiler_params=_cparams(("parallel", "parallel", "arbitrary", "arbitrary")),
        name="gla",
    )(pa3, pa3, pa3, pa3, bd3, nw)


def _swa_kernel(sink_ref, q_ref, kl_ref, kc_ref, kr_ref, vl_ref, vc_ref, vr_ref, o_ref, *, tq):
    kvh = pl.program_id(1)
    n = pl.program_id(2)
    blk = SWA_WINDOW
    k_all = jnp.concatenate([kl_ref[0], kc_ref[0], kr_ref[0]], axis=0)
    v_all = jnp.concatenate([vl_ref[0], vc_ref[0], vr_ref[0]], axis=0)
    rows = SWA_GROUP * blk
    qi = lax.broadcasted_iota(I32, (rows, 3 * blk), 0) & (blk - 1)
    kj = lax.broadcasted_iota(I32, (rows, 3 * blk), 1) - blk
    band = jnp.where(jnp.abs(kj - qi) <= SWA_WINDOW, 0.0, -jnp.inf)
    head = lax.broadcasted_iota(I32, (rows, 1), 0) // blk
    sink = jnp.zeros((rows, 1), F32)
    for g in range(SWA_GROUP):
        sink = jnp.where(head == g, sink_ref[kvh * SWA_GROUP + g], sink)
    n_sub = tq // blk
    before = jnp.where(n == 0, -jnp.inf, 0.0)
    after = jnp.where(n == pl.num_programs(2) - 1, -jnp.inf, 0.0)
    for jq in range(n_sub):
        bias = band
        if jq == 0:
            bias = bias + jnp.where(kj < 0, before, 0.0)
        if jq == n_sub - 1:
            bias = bias + jnp.where(kj >= blk, after, 0.0)
        qs = q_ref[0, jq * blk:(jq + 1) * blk, :]
        q4 = jnp.concatenate([qs[:, g * SWA_HD:(g + 1) * SWA_HD] for g in range(SWA_GROUP)], axis=0)
        keys = k_all[jq * blk:(jq + 3) * blk]
        vals = v_all[jq * blk:(jq + 3) * blk]
        s = _dot_nt(q4, keys) + bias
        m = jnp.maximum(jnp.max(s, axis=-1, keepdims=True), sink)
        p = jnp.exp(s - m)
        denom = jnp.sum(p, axis=-1, keepdims=True) + jnp.exp(sink - m)
        o = _dot(p.astype(BF16), vals) * (1.0 / denom)
        for g in range(SWA_GROUP):
            o_ref[0, jq * blk:(jq + 1) * blk, g * SWA_HD:(g + 1) * SWA_HD] = (
                o[g * blk:(g + 1) * blk].astype(BF16))


def _swa(pb, sinks, bsz, seq):
    tq = min(SWA_TQ, seq)
    blk = SWA_WINDOW
    r = tq // blk
    nb = seq // blk
    pb3 = pb.reshape(bsz, seq, PB_W)
    gw = SWA_GROUP * SWA_HD
    cq = 2 * D_MODEL // gw
    ck = (2 * D_MODEL + SWA_HEADS * SWA_HD) // SWA_HD
    cv = ck + SWA_KV
    left = lambda n: jnp.maximum(n * r - 1, 0)
    right = lambda n: jnp.minimum(n * r + r, nb - 1)
    kern = functools.partial(_swa_kernel, tq=tq)
    grid_spec = pltpu.PrefetchScalarGridSpec(
        num_scalar_prefetch=1,
        grid=(bsz, SWA_KV, seq // tq),
        in_specs=[
            pl.BlockSpec((1, tq, gw), lambda b, h, n, s: (b, n, cq + h)),
            pl.BlockSpec((1, blk, SWA_HD), lambda b, h, n, s: (b, left(n), ck + h)),
            pl.BlockSpec((1, tq, SWA_HD), lambda b, h, n, s: (b, n, ck + h)),
            pl.BlockSpec((1, blk, SWA_HD), lambda b, h, n, s: (b, right(n), ck + h)),
            pl.BlockSpec((1, blk, SWA_HD), lambda b, h, n, s: (b, left(n), cv + h)),
            pl.BlockSpec((1, tq, SWA_HD), lambda b, h, n, s: (b, n, cv + h)),
            pl.BlockSpec((1, blk, SWA_HD), lambda b, h, n, s: (b, right(n), cv + h)),
        ],
        out_specs=pl.BlockSpec((1, tq, gw), lambda b, h, n, s: (b, n, h)),
    )
    return pl.pallas_call(
        kern,
        grid_spec=grid_spec,
        out_shape=jax.ShapeDtypeStruct((bsz, seq, SWA_HEADS * SWA_HD), BF16),
        compiler_params=_cparams(("parallel", "parallel", "parallel")),
        name="swa",
    )(sinks, pb3, pb3, pb3, pb3, pb3, pb3, pb3)


def _layer_norm(y, g, b):
    mu = jnp.mean(y, axis=-1, keepdims=True)
    d = y - mu
    var = jnp.mean(d * d, axis=-1, keepdims=True)
    return d * lax.rsqrt(var + LN_EPS) * g + b


def _merge_kernel(oa_ref, ob_ref, ga_ref, gb_ref, x_ref, wo_ref, g_ref, b_ref, rw_ref, rb_ref,
                  x1_ref, xp_ref, idx_ref, gate_ref, rank_ref, cnt_ref, run_ref, *, tm):
    @pl.when(pl.program_id(0) == 0)
    def _():
        run_ref[...] = jnp.zeros_like(run_ref)

    h = _sigmoid(ga_ref[...]) * oa_ref[...] + _sigmoid(gb_ref[...]) * ob_ref[...]
    y = DEEPNORM_ALPHA * x_ref[...] + _dot(h, wo_ref[...])
    x1 = _layer_norm(y, g_ref[...], b_ref[...])
    x1_ref[...] = x1
    xp_ref[...] = _pack_bf16_pair(x1[:, :HALF], x1[:, HALF:])

    logits = _dot_f32ish(x1, rw_ref[...]) + rb_ref[...]
    lane_i = lax.broadcasted_iota(I32, (tm, LANES), 1)
    lane = lane_i.astype(F32)
    work = logits
    vals, idxs, hots = [], [], []
    for _ in range(TOP_K):
        m = jnp.max(work, axis=-1, keepdims=True)
        sel = jnp.min(jnp.where(work == m, lane, float(LANES)), axis=-1, keepdims=True)
        hot = lane == sel
        vals.append(m)
        idxs.append(sel)
        hots.append(hot)
        work = jnp.where(hot, -jnp.inf, work)
    e = [jnp.exp(v - vals[0]) for v in vals]
    tot = e[0] + e[1] + e[2] + e[3]

    def spread(cols):
        out = jnp.zeros((tm, LANES), F32)
        for kk, col in enumerate(cols):
            out = jnp.where(lane_i == kk, col, out)
        return out

    gate_ref[...] = spread([ei / tot for ei in e])
    idx_ref[...] = spread(idxs).astype(I32)

    member = (hots[0] | hots[1] | hots[2] | hots[3])
    member_f = member.astype(F32)
    r_i = lax.broadcasted_iota(I32, (tm, tm), 0)
    c_i = lax.broadcasted_iota(I32, (tm, tm), 1)
    before = (c_i < r_i).astype(BF16)
    cum = _dot(before, member.astype(BF16)) + run_ref[...]
    ranks = [jnp.sum(jnp.where(hot, cum, 0.0), axis=-1, keepdims=True) for hot in hots]
    rank_ref[...] = spread(ranks).astype(I32)
    run = run_ref[...] + jnp.sum(member_f, axis=0, keepdims=True)
    run_ref[...] = run
    cnt_ref[...] = run


def _merge(oa, ob, pb, x2d, wo, g, b, rw, rb):
    t = x2d.shape[0]
    tm = PROJ_TM
    const = lambda i: (0, 0)
    kern = functools.partial(_merge_kernel, tm=tm)
    return pl.pallas_call(
        kern,
        grid=(t // tm,),
        in_specs=[
            pl.BlockSpec((tm, D_MODEL), lambda i: (i, 0)),
            pl.BlockSpec((tm, D_MODEL), lambda i: (i, 0)),
            pl.BlockSpec((tm, D_MODEL), lambda i: (i, 0)),
            pl.BlockSpec((tm, D_MODEL), lambda i: (i, 1)),
            pl.BlockSpec((tm, D_MODEL), lambda i: (i, 0)),
            pl.BlockSpec((D_MODEL, D_MODEL), const),
            pl.BlockSpec((1, D_MODEL), const),
            pl.BlockSpec((1, D_MODEL), const),
            pl.BlockSpec((D_MODEL, LANES), const),
            pl.BlockSpec((1, LANES), const),
        ],
        out_specs=[
            pl.BlockSpec((tm, D_MODEL), lambda i: (i, 0)),
            pl.BlockSpec((tm, HALF), lambda i: (i, 0)),
            pl.BlockSpec((tm, LANES), lambda i: (i, 0)),
            pl.BlockSpec((tm, LANES), lambda i: (i, 0)),
            pl.BlockSpec((tm, LANES), lambda i: (i, 0)),
            pl.BlockSpec((1, LANES), const),
        ],
        out_shape=[
            jax.ShapeDtypeStruct((t, D_MODEL), F32),
            jax.ShapeDtypeStruct((t, HALF), U32),
            jax.ShapeDtypeStruct((t, LANES), I32),
            jax.ShapeDtypeStruct((t, LANES), F32),
            jax.ShapeDtypeStruct((t, LANES), I32),
            jax.ShapeDtypeStruct((1, LANES), F32),
        ],
        scratch_shapes=[pltpu.VMEM((1, LANES), F32)],
        compiler_params=_cparams(("arbitrary",)),
        name="merge",
    )(oa, ob, pb, pb, x2d, wo, g, b, rw, rb)


def _gather_rows(table, idx):
    n = idx.shape[0]
    d = table.shape[1]
    info = plsc.get_sparse_core_info()
    nw = info.num_cores * info.num_subcores
    per_w = n // nw
    assert per_w * nw == n and per_w % SC_CHUNK == 0
    n_chunks = per_w // SC_CHUNK
    mesh = plsc.VectorSubcoreMesh(core_axis_name="c", subcore_axis_name="s")

    @functools.partial(
        pl.kernel, mesh=mesh,
        out_type=jax.ShapeDtypeStruct((n, d), table.dtype),
        scratch_types=[
            pltpu.VMEM((SC_CHUNK,), I32),
            pltpu.VMEM((SC_CHUNK, d), table.dtype),
            pltpu.SemaphoreType.DMA,
        ],
    )
    def k(table_hbm, idx_hbm, out_hbm, idx_v, rows_v, sem):
        wid = lax.axis_index("s") * info.num_cores + lax.axis_index("c")
        base = wid * per_w

        @pl.loop(0, n_chunks)
        def _(j):
            off = pl.multiple_of(base + j * SC_CHUNK, 8)
            pltpu.sync_copy(idx_hbm.at[pl.ds(off, SC_CHUNK)], idx_v)
            pltpu.async_copy(table_hbm.at[idx_v], rows_v, sem).wait()
            pltpu.sync_copy(rows_v, out_hbm.at[pl.ds(off, SC_CHUNK)])

    return k(table, idx)


def _scatter_rows(src, dest_km, n_rows):
    t, d = src.shape
    info = plsc.get_sparse_core_info()
    nw = info.num_cores * info.num_subcores
    per_w = t // nw
    assert per_w * nw == t and per_w % SC_CHUNK == 0
    n_chunks = per_w // SC_CHUNK
    mesh = plsc.VectorSubcoreMesh(core_axis_name="c", subcore_axis_name="s")

    @functools.partial(
        pl.kernel, mesh=mesh,
        out_type=jax.ShapeDtypeStruct((n_rows, d), src.dtype),
        scratch_types=[
            pltpu.VMEM((SC_CHUNK,), I32),
            pltpu.VMEM((SC_CHUNK, d), src.dtype),
            pltpu.SemaphoreType.DMA,
        ],
    )
    def k(src_hbm, dest_hbm, out_hbm, idx_v, rows_v, sem):
        wid = lax.axis_index("s") * info.num_cores + lax.axis_index("c")
        base = wid * per_w

        @pl.loop(0, n_chunks)
        def _(j):
            off = pl.multiple_of(base + j * SC_CHUNK, 8)
            pltpu.sync_copy(src_hbm.at[pl.ds(off, SC_CHUNK)], rows_v)
            for kk in range(TOP_K):
                pltpu.sync_copy(dest_hbm.at[pl.ds(pl.multiple_of(kk * t + off, 8), SC_CHUNK)], idx_v)
                pltpu.async_copy(rows_v, out_hbm.at[idx_v], sem).wait()

    return k(src, dest_km)


def _expert_kernel(be_ref, nused_ref, nvalid_ref, xs_ref, wgu_ref, bgu_ref, wd_ref, bd_ref, ys_ref,
                   wgu_bf, wd_bf):
    i = pl.program_id(0)

    @pl.when(i < nused_ref[0])
    def _():
        @pl.when((i == 0) | (be_ref[i] != be_ref[jnp.maximum(i - 1, 0)]))
        def _():
            rows = 256
            for r0 in range(0, D_MODEL, rows):
                wgu_bf[r0:r0 + rows, :] = wgu_ref[0, r0:r0 + rows, :].astype(BF16)
            for r0 in range(0, D_EXPERT, rows):
                wd_bf[r0:r0 + rows, :] = wd_ref[0, r0:r0 + rows, :].astype(BF16)

        rowid = lax.broadcasted_iota(I32, (MOE_BM, 1), 0)
        words = jnp.where(rowid < nvalid_ref[i], xs_ref[...], jnp.uint32(0))
        lo, hi = _unpack_bf16_pair(words)
        lo = lo.astype(BF16)
        hi = hi.astype(BF16)
        acc = None
        step = 1024
        for c0 in range(0, D_EXPERT, step):
            def proj(col):
                return (_dot(lo, wgu_bf[:HALF, col:col + step])
                        + _dot(hi, wgu_bf[HALF:, col:col + step])
                        + bgu_ref[0, :, col:col + step])
            gate = jnp.minimum(proj(c0), SWIGLU_LIMIT)
            up = jnp.clip(proj(D_EXPERT + c0), -SWIGLU_LIMIT, SWIGLU_LIMIT)
            act = ((up + 1.0) * gate * _sigmoid(gate * SWIGLU_ALPHA)).astype(BF16)
            part = _dot(act, wd_bf[c0:c0 + step, :])
            acc = part if acc is None else acc + part
        y = acc + bd_ref[0]
        ys_ref[...] = _pack_bf16_pair(y[:, :HALF], y[:, HALF:])


def _experts(xs, block_expert, n_used, n_valid, wgu, bgu, wd, bd):
    n_rows = xs.shape[0]
    bm = MOE_BM
    n_blocks = n_rows // bm
    rows = lambda i, be, nu, nv: (jnp.minimum(i, nu[0] - 1), 0)
    expert = lambda i, be, nu, nv: (be[i], 0, 0)
    grid_spec = pltpu.PrefetchScalarGridSpec(
        num_scalar_prefetch=3,
        grid=(n_blocks,),
        in_specs=[
            pl.BlockSpec((bm, HALF), rows),
            pl.BlockSpec((1, D_MODEL, 2 * D_EXPERT), expert),
            pl.BlockSpec((1, 1, 2 * D_EXPERT), expert),
            pl.BlockSpec((1, D_EXPERT, D_MODEL), expert),
            pl.BlockSpec((1, 1, D_MODEL), expert),
        ],
        out_specs=pl.BlockSpec((bm, HALF), rows),
        scratch_shapes=[
            pltpu.VMEM((D_MODEL, 2 * D_EXPERT), BF16),
            pltpu.VMEM((D_EXPERT, D_MODEL), BF16),
        ],
    )
    return pl.pallas_call(
        _expert_kernel,
        grid_spec=grid_spec,
        out_shape=jax.ShapeDtypeStruct((n_rows, HALF), U32),
        compiler_params=_cparams(("arbitrary",)),
        name="experts",
    )(block_expert, n_used, n_valid, xs, wgu, bgu, wd, bd)


def _combine_kernel(x1_ref, yg_ref, gate_ref, g_ref, b_ref, o_ref):
    gates = gate_ref[...]
    f_lo = None
    f_hi = None
    for kk in range(TOP_K):
        lo, hi = _unpack_bf16_pair(yg_ref[kk])
        w = gates[:, kk:kk + 1]
        f_lo = lo * w if f_lo is None else f_lo + lo * w
        f_hi = hi * w if f_hi is None else f_hi + hi * w
    x1 = x1_ref[...]
    y_lo = DEEPNORM_ALPHA * x1[:, :HALF] + f_lo
    y_hi = DEEPNORM_ALPHA * x1[:, HALF:] + f_hi
    mu = (jnp.sum(y_lo, axis=-1, keepdims=True) + jnp.sum(y_hi, axis=-1, keepdims=True)) / D_MODEL
    d_lo = y_lo - mu
    d_hi = y_hi - mu
    var = (jnp.sum(d_lo * d_lo, axis=-1, keepdims=True)
           + jnp.sum(d_hi * d_hi, axis=-1, keepdims=True)) / D_MODEL
    inv = lax.rsqrt(var + LN_EPS)
    g = g_ref[...]
    b = b_ref[...]
    o_ref[:, :HALF] = d_lo * inv * g[:, :HALF] + b[:, :HALF]
    o_ref[:, HALF:] = d_hi * inv * g[:, HALF:] + b[:, HALF:]


def _combine(x1, yg, gates, g, b):
    t = x1.shape[0]
    tm = PROJ_TM
    const = lambda i: (0, 0)
    return pl.pallas_call(
        _combine_kernel,
        grid=(t // tm,),
        in_specs=[
            pl.BlockSpec((tm, D_MODEL), lambda i: (i, 0)),
            pl.BlockSpec((TOP_K, tm, HALF), lambda i: (0, i, 0)),
            pl.BlockSpec((tm, LANES), lambda i: (i, 0)),
            pl.BlockSpec((1, D_MODEL), const),
            pl.BlockSpec((1, D_MODEL), const),
        ],
        out_specs=pl.BlockSpec((tm, D_MODEL), lambda i: (i, 0)),
        out_shape=jax.ShapeDtypeStruct((t, D_MODEL), F32),
        compiler_params=_cparams(("parallel",)),
        name="combine",
    )(x1, yg, gates, g, b)


def _rope_tables(seq):
    half = ROPE_DIM // 2
    pos = jnp.arange(seq, dtype=F32)
    inv_freq = ROPE_THETA ** (-jnp.arange(0, ROPE_DIM, 2, dtype=F32) / ROPE_DIM)
    ang = pos[:, None] * inv_freq[None, :]
    cos, sin = jnp.cos(ang), jnp.sin(ang)
    zeros = jnp.zeros((seq, LANES - ROPE_DIM), F32)
    cos_f = jnp.concatenate([cos, cos, jnp.ones((seq, LANES - ROPE_DIM), F32)], axis=1)
    sin_a = jnp.concatenate([-sin, jnp.zeros((seq, half), F32), zeros], axis=1)
    sin_b = jnp.concatenate([jnp.zeros((seq, half), F32), sin, zeros], axis=1)
    return cos_f, sin_a, sin_b


def _prep_weights(w_in, gla_gate_w2, gla_gate_b, gla_norm_w, w_o, ln1_g, ln1_b, router_w, router_b,
                  w_gate_up, b_gate_up, w_down, b_down, ln2_g, ln2_b):
    sizes = (GLA_KEY, GLA_KEY, GLA_VAL, GLA_VAL, 2 * GLA_RANK, SWA_HEADS * SWA_HD,
             SWA_KV * SWA_HD, SWA_KV * SWA_HD, D_MODEL, D_MODEL)
    cols, acc = [], 0
    for s in sizes:
        cols.append(w_in[:, acc:acc + s])
        acc += s
    gq, gk, gv, gr, glr, sq, sk, sv, ga, gb = cols
    wa = jnp.concatenate([gq, gk, gv, gr], axis=1).astype(BF16)
    wl = jnp.pad(glr, ((0, 0), (0, LANES - 2 * GLA_RANK))).astype(BF16)
    wb = jnp.concatenate([ga, gb, sq, sk, sv], axis=1).astype(BF16)
    w2f = jnp.zeros((LANES, 2 * GLA_KEY), F32)
    w2f = w2f.at[:GLA_RANK, :GLA_KEY].set(gla_gate_w2[0])
    w2f = w2f.at[GLA_RANK:2 * GLA_RANK, GLA_KEY:].set(gla_gate_w2[1]).astype(BF16)
    rw = jnp.pad(router_w, ((0, 0), (0, LANES - N_EXPERTS)))
    rb = jnp.concatenate([router_b, jnp.full((LANES - N_EXPERTS,), -jnp.inf, F32)]).reshape(1, LANES)
    return dict(
        wa=wa, wl=wl, wb=wb, w2f=w2f, bgaf=gla_gate_b.reshape(1, 2 * GLA_KEY),
        nw=gla_norm_w.reshape(1, GLA_VAL), wo=w_o.astype(BF16),
        ln1_g=ln1_g.reshape(1, D_MODEL), ln1_b=ln1_b.reshape(1, D_MODEL), rw=rw, rb=rb,
        wgu=w_gate_up, bgu=b_gate_up.reshape(N_EXPERTS, 1, 2 * D_EXPERT),
        wd=w_down, bd=b_down.reshape(N_EXPERTS, 1, D_MODEL),
        ln2_g=ln2_g.reshape(1, D_MODEL), ln2_b=ln2_b.reshape(1, D_MODEL))


def _routing_plan(idx, rank, counts):
    t = idx.shape[0]
    bm = MOE_BM
    n_rows = t * TOP_K + N_EXPERTS * bm
    n_blocks = n_rows // bm
    counts = counts[0, :N_EXPERTS].astype(I32)
    padded = (counts + bm - 1) // bm * bm
    pad_ends = jnp.cumsum(padded)
    pad_starts = pad_ends - padded
    hot = idx[:, :, None] == jnp.arange(N_EXPERTS, dtype=I32)[None, None, :]
    dest = jnp.sum(jnp.where(hot, pad_starts[None, None, :], 0), axis=-1) + rank
    block_start = jnp.arange(n_blocks, dtype=I32) * bm
    block_expert = jnp.minimum(
        jnp.sum(block_start[:, None] >= pad_ends[None, :], axis=1), N_EXPERTS - 1).astype(I32)
    mine = block_expert[:, None] == jnp.arange(N_EXPERTS, dtype=I32)[None, :]
    seg_end = jnp.sum(jnp.where(mine, (pad_starts + counts)[None, :], 0), axis=1)
    n_valid = jnp.clip(seg_end - block_start, 0, bm).astype(I32)
    n_used = (pad_ends[-1] // bm).astype(I32).reshape(1)
    return dest.T.reshape(-1), n_rows, block_expert, n_used, n_valid


def _trunk(x, w, sinks, gather_rows, scatter_rows):
    bsz, seq, _ = x.shape
    t = bsz * seq
    x2d = x.reshape(t, D_MODEL)
    cos_f, sin_a, sin_b = _rope_tables(seq)
    pa, bdec, pb = _in_proj(x2d, w["wa"], w["wl"], w["w2f"], w["bgaf"], w["wb"],
                            cos_f, sin_a, sin_b, seq)
    oa = _gla(pa, bdec, w["nw"], bsz, seq).reshape(t, GLA_VAL)
    ob = _swa(pb, sinks, bsz, seq).reshape(t, SWA_HEADS * SWA_HD)
    x1, xp, idx, gates, rank, counts = _merge(
        oa, ob, pb, x2d, w["wo"], w["ln1_g"], w["ln1_b"], w["rw"], w["rb"])
    dest_km, n_rows, block_expert, n_used, n_valid = _routing_plan(
        idx[:, :TOP_K], rank[:, :TOP_K], counts)
    xs = scatter_rows(xp, dest_km, n_rows)
    ys = _experts(xs, block_expert, n_used, n_valid, w["wgu"], w["bgu"], w["wd"], w["bd"])
    yg = gather_rows(ys, dest_km).reshape(TOP_K, t, HALF)
    out = _combine(x1, yg, gates, w["ln2_g"], w["ln2_b"])
    return out.reshape(bsz, seq, D_MODEL)


def _forward(x_prompt, x_sample, w_in, gla_gate_w2, gla_gate_b, gla_norm_w, swa_sinks, w_o, ln1_g,
             ln1_b, router_w, router_b, w_gate_up, b_gate_up, w_down, b_down, ln2_g, ln2_b):
    w = _prep_weights(w_in[0], gla_gate_w2[0], gla_gate_b[0], gla_norm_w[0], w_o[0], ln1_g[0],
                      ln1_b[0], router_w[0], router_b[0], w_gate_up[0], b_gate_up[0], w_down[0],
                      b_down[0], ln2_g[0], ln2_b[0])
    sinks = swa_sinks[0].astype(F32)
    return (_trunk(x_prompt, w, sinks, _gather_rows, _scatter_rows),
            _trunk(x_sample, w, sinks, _gather_rows, _scatter_rows))


def kernel(x_prompt, x_sample, w_in, gla_gate_w2, gla_gate_b, gla_norm_w, swa_sinks, w_o, ln1_g, ln1_b, router_w, router_b, w_gate_up, b_gate_up, w_down, b_down, ln2_g, ln2_b):
    return _forward(x_prompt, x_sample, w_in, gla_gate_w2, gla_gate_b, gla_norm_w, swa_sinks, w_o,
                    ln1_g, ln1_b, router_w, router_b, w_gate_up, b_gate_up, w_down, b_down,
                    ln2_g, ln2_b)
```

```python
import functools

import jax
import jax.numpy as jnp
from jax import lax
from jax.experimental import pallas as pl
from jax.experimental.pallas import tpu as pltpu
from jax.experimental.pallas import tpu_sc as plsc

F32 = jnp.float32
BF16 = jnp.bfloat16
I32 = jnp.int32
U32 = jnp.uint32

D_MODEL = 1024
GLA_HEADS = 4
GLA_DK = 128
GLA_DV = 256
GLA_KEY = GLA_HEADS * GLA_DK
GLA_VAL = GLA_HEADS * GLA_DV
GLA_RANK = 16
GLA_NORMALIZER = 16.0
GLA_CHUNK = 64
SWA_HEADS = 8
SWA_KV = 2
SWA_GROUP = SWA_HEADS // SWA_KV
SWA_HD = 128
SWA_WINDOW = 128
ROPE_THETA = 500000.0
ROPE_DIM = SWA_HD // 4
N_EXPERTS = 32
TOP_K = 4
D_EXPERT = D_MODEL
SWIGLU_LIMIT = 7.0
SWIGLU_ALPHA = 1.702
DEEPNORM_ALPHA = 2.0 ** 0.25
LN_EPS = 1e-5
RMS_EPS = 1e-6

LANES = 128
PA_W = 2 * GLA_KEY + 2 * GLA_VAL
PB_W = 2 * D_MODEL + SWA_HEADS * SWA_HD + 2 * SWA_KV * SWA_HD
HALF = D_MODEL // 2

PROJ_TM = 512
GLA_TS = 2048
SWA_TQ = 2048
MOE_BM = 512
MOE_SUB = 512
SC_CHUNK = 64
VMEM_LIMIT = 56 * 1024 * 1024


def _cparams(sem):
    return pltpu.CompilerParams(dimension_semantics=sem, vmem_limit_bytes=VMEM_LIMIT)


def _split2(a):
    hi = a.astype(BF16)
    lo = (a - hi.astype(F32)).astype(BF16)
    return hi, lo


def _dot(a, b):
    return jnp.dot(a, b, preferred_element_type=F32)


def _dot_nt(a, b):
    return lax.dot_general(a, b, (((1,), (1,)), ((), ())), preferred_element_type=F32)


def _sigmoid(x):
    return 0.5 * jnp.tanh(0.5 * x) + 0.5


def _pack_bf16_pair(lo_f32, hi_f32):
    lo = lax.bitcast_convert_type(lo_f32.astype(BF16).astype(F32), U32)
    hi = lax.bitcast_convert_type(hi_f32.astype(BF16).astype(F32), U32)
    return (hi & jnp.uint32(0xFFFF0000)) | (lo >> 16)


def _unpack_bf16_pair(w):
    lo = lax.bitcast_convert_type(w << 16, F32)
    hi = lax.bitcast_convert_type(w & jnp.uint32(0xFFFF0000), F32)
    return lo, hi


def _chunk_cumsum(x, reverse):
    rows = x.shape[0]
    pos = lax.broadcasted_iota(I32, x.shape, 0) & (GLA_CHUNK - 1)
    s = 1
    while s < GLA_CHUNK:
        if reverse:
            x = x + jnp.where(pos < GLA_CHUNK - s, pltpu.roll(x, rows - s, 0), 0.0)
        else:
            x = x + jnp.where(pos >= s, pltpu.roll(x, s, 0), 0.0)
        s *= 2
    return x


def _in_proj_kernel(x_ref, wa_ref, wl_ref, w2_ref, bga_ref, wb_ref, cos_ref, sina_ref, sinb_ref,
                    pa_ref, bd_ref, pb_ref):
    xb = x_ref[...].astype(BF16)
    lrb = _dot(xb, wl_ref[...]).astype(BF16)

    def decay_slab(j):
        cols = slice(j * LANES, (j + 1) * LANES)
        z = _dot(lrb, w2_ref[:, cols]) + bga_ref[:, cols]
        la = (jnp.minimum(z, 0.0) - jnp.log(1.0 + jnp.exp(-jnp.abs(z)))) * (1.0 / GLA_NORMALIZER)
        bd_ref[:, cols] = _chunk_cumsum(la, reverse=j * LANES >= GLA_KEY)

    n_slabs = 2 * GLA_KEY // LANES
    slab = 0
    step = 512
    for c0 in range(0, PA_W, step):
        r = _dot(xb, wa_ref[:, c0:c0 + step])
        if c0 < GLA_KEY:
            r = r * (GLA_DK ** -0.5)
        pa_ref[:, c0:c0 + step] = r.astype(BF16)
        if slab < n_slabs:
            decay_slab(slab)
            slab += 1
    cos = cos_ref[...]
    sina = sina_ref[...]
    sinb = sinb_ref[...]
    rope0 = 2 * D_MODEL
    rope_end = rope0 + (SWA_HEADS + SWA_KV) * SWA_HD
    for c0 in range(0, PB_W, step):
        r = _dot(xb, wb_ref[:, c0:c0 + step])
        if c0 + step > rope0 and c0 < rope_end:
            parts = []
            for j in range(step // SWA_HD):
                t = r[:, j * SWA_HD:(j + 1) * SWA_HD]
                col = c0 + j * SWA_HD
                if rope0 <= col < rope_end:
                    t = (t * cos + pltpu.roll(t, SWA_HD - ROPE_DIM // 2, 1) * sina
                         + pltpu.roll(t, ROPE_DIM // 2, 1) * sinb)
                if rope0 <= col < rope0 + SWA_HEADS * SWA_HD:
                    t = t * (SWA_HD ** -0.5)
                parts.append(t)
            r = jnp.concatenate(parts, axis=1)
        pb_ref[:, c0:c0 + step] = r.astype(BF16)
        if slab < n_slabs:
            decay_slab(slab)
            slab += 1
    assert slab == n_slabs


def _in_proj(x2d, wa, wl, w2f, bgaf, wb, cos, sina, sinb, seq):
    t = x2d.shape[0]
    tm = PROJ_TM
    nseq = seq // tm
    const = lambda i: (0, 0)
    return pl.pallas_call(
        _in_proj_kernel,
        grid=(t // tm,),
        in_specs=[
            pl.BlockSpec((tm, D_MODEL), lambda i: (i, 0)),
            pl.BlockSpec((D_MODEL, PA_W), const),
            pl.BlockSpec((D_MODEL, LANES), const),
            pl.BlockSpec((LANES, 2 * GLA_KEY), const),
            pl.BlockSpec((1, 2 * GLA_KEY), const),
            pl.BlockSpec((D_MODEL, PB_W), const),
            pl.BlockSpec((tm, LANES), lambda i: (i % nseq, 0)),
            pl.BlockSpec((tm, LANES), lambda i: (i % nseq, 0)),
            pl.BlockSpec((tm, LANES), lambda i: (i % nseq, 0)),
        ],
        out_specs=[
            pl.BlockSpec((tm, PA_W), lambda i: (i, 0)),
            pl.BlockSpec((tm, 2 * GLA_KEY), lambda i: (i, 0)),
            pl.BlockSpec((tm, PB_W), lambda i: (i, 0)),
        ],
        out_shape=[
            jax.ShapeDtypeStruct((t, PA_W), BF16),
            jax.ShapeDtypeStruct((t, 2 * GLA_KEY), F32),
            jax.ShapeDtypeStruct((t, PB_W), BF16),
        ],
        compiler_params=_cparams(("parallel",)),
        name="in_proj",
    )(x2d, wa, wl, w2f, bgaf, wb, cos, sina, sinb)


def _gla_kernel(q_ref, k_ref, v_ref, r_ref, b_ref, nw_ref, o_ref,
                state_ref, ofwd_ref, sall_ref, ds_ref, dec_ref, *, ts, nt):
    ph = pl.program_id(2)
    t = pl.program_id(3)
    fwd = ph == 0
    c = GLA_CHUNK
    nc = ts // c
    tile = jnp.where(fwd, t, nt - 1 - t)

    @pl.when(t == 0)
    def _():
        state_ref[...] = jnp.zeros_like(state_ref)

    row = lax.broadcasted_iota(I32, (c, c), 0)
    col = lax.broadcasted_iota(I32, (c, c), 1)
    ahead = (col - row) * jnp.where(fwd, 1, -1)
    keep = ahead <= jnp.where(fwd, 0, -1)

    q3 = q_ref[0].reshape(nc, c, GLA_DK)
    k3 = k_ref[0].reshape(nc, c, GLA_DK)
    v3 = v_ref[0].reshape(nc, c, GLA_DV)
    b3 = b_ref[0].reshape(nc, c, GLA_DK)
    b_mid = jnp.where(fwd, b3[:, c // 2:c // 2 + 1], b3[:, c // 2 - 1:c // 2])
    b_end = jnp.where(fwd, b3[:, c - 1:c], b3[:, 0:1])
    q_in = q3 * jnp.exp(b3 - b_mid).astype(BF16)
    k_in = k3 * jnp.exp(b_mid - b3).astype(BF16)
    k_st = k_in * jnp.exp(b_end - b_mid).astype(BF16)
    q_b = q_in * jnp.exp(b_mid).astype(BF16)
    dec_ref[...] = jnp.exp(b_end)
    scores = jnp.einsum("cid,cjd->cij", q_in, k_in, preferred_element_type=F32)
    scores = jnp.where(keep[None], scores, 0.0).astype(BF16)
    o_intra = jnp.einsum("cij,cjv->civ", scores, v3, preferred_element_type=F32)
    ds_ref[...] = jnp.einsum("cjv,cjd->cvd", v3, k_st, preferred_element_type=F32)
    st = state_ref[...]
    for i in range(nc):
        ci = jnp.where(fwd, i, nc - 1 - i)
        sall_ref[ci] = st.astype(BF16)
        st = st * dec_ref[ci] + ds_ref[ci]
    state_ref[...] = st
    o_inter = jnp.einsum("cid,cvd->civ", q_b, sall_ref[...], preferred_element_type=F32)
    o = (o_intra + o_inter).reshape(ts, GLA_DV)
    off = pl.multiple_of(tile * ts, ts)

    @pl.when(fwd)
    def _():
        ofwd_ref[pl.ds(off, ts), :] = o

    @pl.when(jnp.logical_not(fwd))
    def _():
        tot = o + ofwd_ref[pl.ds(off, ts), :]
        tot = tot * lax.rsqrt(jnp.mean(tot * tot, axis=-1, keepdims=True) + RMS_EPS)
        r = r_ref[0]
        o_ref[0] = (tot * nw_ref[...] * (r * _sigmoid(r)).astype(F32)).astype(BF16)


def _gla(pa, bdec, nw, bsz, seq):
    ts = min(GLA_TS, seq)
    nt = seq // ts
    nc = ts // GLA_CHUNK
    pa3 = pa.reshape(bsz, seq, PA_W)
    bd3 = bdec.reshape(bsz, seq, 2 * GLA_KEY)
    tile = lambda p, t: t + p * (nt - 1 - 2 * t)
    wk, wv = GLA_DK, GLA_DV
    kq = GLA_KEY // wk
    kv = 2 * GLA_KEY // wv
    kr = kv + GLA_VAL // wv
    kern = functools.partial(_gla_kernel, ts=ts, nt=nt)
    return pl.pallas_call(
        kern,
        grid=(bsz, GLA_HEADS, 2, nt),
        in_specs=[
            pl.BlockSpec((1, ts, wk), lambda b, h, p, t: (b, tile(p, t), h)),
            pl.BlockSpec((1, ts, wk), lambda b, h, p, t: (b, tile(p, t), kq + h)),
            pl.BlockSpec((1, ts, wv), lambda b, h, p, t: (b, tile(p, t), kv + h)),
            pl.BlockSpec((1, ts, wv), lambda b, h, p, t: (b, nt - 1 - p * t, kr + h)),
            pl.BlockSpec((1, ts, wk), lambda b, h, p, t: (b, tile(p, t), p * kq + h)),
            pl.BlockSpec((1, wv), lambda b, h, p, t: (0, h)),
        ],
        out_specs=pl.BlockSpec((1, ts, wv), lambda b, h, p, t: (b, nt - 1 - p * t, h)),
        out_shape=jax.ShapeDtypeStruct((bsz, seq, GLA_VAL), BF16),
        scratch_shapes=[
            pltpu.VMEM((GLA_DV, GLA_DK), F32),
            pltpu.VMEM((seq, GLA_DV), F32),
            pltpu.VMEM((nc, GLA_DV, GLA_DK), BF16),
            pltpu.VMEM((nc, GLA_DV, GLA_DK), F32),
            pltpu.VMEM((nc, 1, GLA_DK), F32),
        ],
        compiler_params=_cparams(("parallel", "parallel", "arbitrary", "arbitrary")),
        name="gla",
    )(pa3, pa3, pa3, pa3, bd3, nw)


def _swa_kernel(sink_ref, q_ref, kl_ref, kc_ref, kr_ref, vl_ref, vc_ref, vr_ref, o_ref, *, tq):
    kvh = pl.program_id(1)
    n = pl.program_id(2)
    blk = SWA_WINDOW
    k_all = jnp.concatenate([kl_ref[0], kc_ref[0], kr_ref[0]], axis=0)
    v_all = jnp.concatenate([vl_ref[0], vc_ref[0], vr_ref[0]], axis=0)
    rows = SWA_GROUP * blk
    qi = lax.broadcasted_iota(I32, (rows, 3 * blk), 0) & (blk - 1)
    kj = lax.broadcasted_iota(I32, (rows, 3 * blk), 1) - blk
    band = jnp.where(jnp.abs(kj - qi) <= SWA_WINDOW, 0.0, -jnp.inf)
    head = lax.broadcasted_iota(I32, (rows, 1), 0) // blk
    sink = jnp.zeros((rows, 1), F32)
    for g in range(SWA_GROUP):
        sink = jnp.where(head == g, sink_ref[kvh * SWA_GROUP + g], sink)
    n_sub = tq // blk
    before = jnp.where(n == 0, -jnp.inf, 0.0)
    after = jnp.where(n == pl.num_programs(2) - 1, -jnp.inf, 0.0)
    for jq in range(n_sub):
        bias = band
        if jq == 0:
            bias = bias + jnp.where(kj < 0, before, 0.0)
        if jq == n_sub - 1:
            bias = bias + jnp.where(kj >= blk, after, 0.0)
        qs = q_ref[0, jq * blk:(jq + 1) * blk, :]
        q4 = jnp.concatenate([qs[:, g * SWA_HD:(g + 1) * SWA_HD] for g in range(SWA_GROUP)], axis=0)
        keys = k_all[jq * blk:(jq + 3) * blk]
        vals = v_all[jq * blk:(jq + 3) * blk]
        s = _dot_nt(q4, keys) + bias
        m = jnp.maximum(jnp.max(s, axis=-1, keepdims=True), sink)
        p = jnp.exp(s - m)
        denom = jnp.sum(p, axis=-1, keepdims=True) + jnp.exp(sink - m)
        o = _dot(p.astype(BF16), vals) * (1.0 / denom)
        for g in range(SWA_GROUP):
            o_ref[0, jq * blk:(jq + 1) * blk, g * SWA_HD:(g + 1) * SWA_HD] = (
                o[g * blk:(g + 1) * blk].astype(BF16))


def _swa(pb, sinks, bsz, seq):
    tq = min(SWA_TQ, seq)
    blk = SWA_WINDOW
    r = tq // blk
    nb = seq // blk
    pb3 = pb.reshape(bsz, seq, PB_W)
    gw = SWA_GROUP * SWA_HD
    cq = 2 * D_MODEL // gw
    ck = (2 * D_MODEL + SWA_HEADS * SWA_HD) // SWA_HD
    cv = ck + SWA_KV
    left = lambda n: jnp.maximum(n * r - 1, 0)
    right = lambda n: jnp.minimum(n * r + r, nb - 1)
    kern = functools.partial(_swa_kernel, tq=tq)
    grid_spec = pltpu.PrefetchScalarGridSpec(
        num_scalar_prefetch=1,
        grid=(bsz, SWA_KV, seq // tq),
        in_specs=[
            pl.BlockSpec((1, tq, gw), lambda b, h, n, s: (b, n, cq + h)),
            pl.BlockSpec((1, blk, SWA_HD), lambda b, h, n, s: (b, left(n), ck + h)),
            pl.BlockSpec((1, tq, SWA_HD), lambda b, h, n, s: (b, n, ck + h)),
            pl.BlockSpec((1, blk, SWA_HD), lambda b, h, n, s: (b, right(n), ck + h)),
            pl.BlockSpec((1, blk, SWA_HD), lambda b, h, n, s: (b, left(n), cv + h)),
            pl.BlockSpec((1, tq, SWA_HD), lambda b, h, n, s: (b, n, cv + h)),
            pl.BlockSpec((1, blk, SWA_HD), lambda b, h, n, s: (b, right(n), cv + h)),
        ],
        out_specs=pl.BlockSpec((1, tq, gw), lambda b, h, n, s: (b, n, h)),
    )
    return pl.pallas_call(
        kern,
        grid_spec=grid_spec,
        out_shape=jax.ShapeDtypeStruct((bsz, seq, SWA_HEADS * SWA_HD), BF16),
        compiler_params=_cparams(("parallel", "parallel", "parallel")),
        name="swa",
    )(sinks, pb3, pb3, pb3, pb3, pb3, pb3, pb3)


def _layer_norm(y, g, b):
    mu = jnp.mean(y, axis=-1, keepdims=True)
    d = y - mu
    var = jnp.mean(d * d, axis=-1, keepdims=True)
    return d * lax.rsqrt(var + LN_EPS) * g + b


def _merge_kernel(oa_ref, ob_ref, ga_ref, gb_ref, x_ref, wo_ref, g_ref, b_ref, rw_ref, rb_ref,
                  before_ref, x1_ref, xp_ref, idx_ref, gate_ref, rank_ref, cnt_ref, run_ref, *, tm):
    @pl.when(pl.program_id(0) == 0)
    def _():
        run_ref[...] = jnp.zeros_like(run_ref)

    h = _sigmoid(ga_ref[...]) * oa_ref[...] + _sigmoid(gb_ref[...]) * ob_ref[...]
    y = DEEPNORM_ALPHA * x_ref[...] + _dot(h, wo_ref[...])
    x1 = _layer_norm(y, g_ref[...], b_ref[...])
    x1_ref[...] = x1
    xp_ref[...] = _pack_bf16_pair(x1[:, :HALF], x1[:, HALF:])

    x_hi, x_lo = _split2(x1)
    both = _dot(x_hi, rw_ref[...])
    logits = (both[:, :LANES] + both[:, LANES:] + _dot(x_lo, rw_ref[:, :LANES])
              + rb_ref[...])
    lane_i = lax.broadcasted_iota(I32, (tm, LANES), 1)
    lane = lane_i.astype(F32)
    work = logits
    vals, idxs, hots = [], [], []
    for _ in range(TOP_K):
        m = jnp.max(work, axis=-1, keepdims=True)
        sel = jnp.min(jnp.where(work == m, lane, float(LANES)), axis=-1, keepdims=True)
        hot = lane == sel
        vals.append(m)
        idxs.append(sel)
        hots.append(hot)
        work = jnp.where(hot, -jnp.inf, work)
    e = [jnp.exp(v - vals[0]) for v in vals]
    tot = e[0] + e[1] + e[2] + e[3]

    def spread(cols):
        out = jnp.zeros((tm, LANES), F32)
        for kk, col in enumerate(cols):
            out = jnp.where(lane_i == kk, col, out)
        return out

    gate_ref[...] = spread([ei / tot for ei in e])
    idx_ref[...] = spread(idxs).astype(I32)

    member = (hots[0] | hots[1] | hots[2] | hots[3])
    member_f = member.astype(F32)
    cum = _dot(before_ref[...], member.astype(BF16)) + run_ref[...]
    ranks = [jnp.sum(jnp.where(hot, cum, 0.0), axis=-1, keepdims=True) for hot in hots]
    rank_ref[...] = spread(ranks).astype(I32)
    run = run_ref[...] + jnp.sum(member_f, axis=0, keepdims=True)
    run_ref[...] = run
    cnt_ref[...] = run


def _merge(oa, ob, pb, x2d, wo, g, b, rw, rb):
    t = x2d.shape[0]
    tm = PROJ_TM
    const = lambda i: (0, 0)
    kern = functools.partial(_merge_kernel, tm=tm)
    before = (jnp.arange(tm)[None, :] < jnp.arange(tm)[:, None]).astype(BF16)
    return pl.pallas_call(
        kern,
        grid=(t // tm,),
        in_specs=[
            pl.BlockSpec((tm, D_MODEL), lambda i: (i, 0)),
            pl.BlockSpec((tm, D_MODEL), lambda i: (i, 0)),
            pl.BlockSpec((tm, D_MODEL), lambda i: (i, 0)),
            pl.BlockSpec((tm, D_MODEL), lambda i: (i, 1)),
            pl.BlockSpec((tm, D_MODEL), lambda i: (i, 0)),
            pl.BlockSpec((D_MODEL, D_MODEL), const),
            pl.BlockSpec((1, D_MODEL), const),
            pl.BlockSpec((1, D_MODEL), const),
            pl.BlockSpec((D_MODEL, 2 * LANES), const),
            pl.BlockSpec((1, LANES), const),
            pl.BlockSpec((tm, tm), const),
        ],
        out_specs=[
            pl.BlockSpec((tm, D_MODEL), lambda i: (i, 0)),
            pl.BlockSpec((tm, HALF), lambda i: (i, 0)),
            pl.BlockSpec((tm, LANES), lambda i: (i, 0)),
            pl.BlockSpec((tm, LANES), lambda i: (i, 0)),
            pl.BlockSpec((tm, LANES), lambda i: (i, 0)),
            pl.BlockSpec((1, LANES), const),
        ],
        out_shape=[
            jax.ShapeDtypeStruct((t, D_MODEL), F32),
            jax.ShapeDtypeStruct((t, HALF), U32),
            jax.ShapeDtypeStruct((t, LANES), I32),
            jax.ShapeDtypeStruct((t, LANES), F32),
            jax.ShapeDtypeStruct((t, LANES), I32),
            jax.ShapeDtypeStruct((1, LANES), F32),
        ],
        scratch_shapes=[pltpu.VMEM((1, LANES), F32)],
        compiler_params=_cparams(("arbitrary",)),
        name="merge",
    )(oa, ob, pb, pb, x2d, wo, g, b, rw, rb, before)


def _gather_rows(table, idx):
    n = idx.shape[0]
    d = table.shape[1]
    info = plsc.get_sparse_core_info()
    nw = info.num_cores * info.num_subcores
    per_w = n // nw
    assert per_w * nw == n and per_w % SC_CHUNK == 0
    n_chunks = per_w // SC_CHUNK
    mesh = plsc.VectorSubcoreMesh(core_axis_name="c", subcore_axis_name="s")

    @functools.partial(
        pl.kernel, mesh=mesh,
        out_type=jax.ShapeDtypeStruct((n, d), table.dtype),
        scratch_types=[
            pltpu.VMEM((SC_CHUNK,), I32),
            pltpu.VMEM((SC_CHUNK, d), table.dtype),
            pltpu.SemaphoreType.DMA,
        ],
    )
    def k(table_hbm, idx_hbm, out_hbm, idx_v, rows_v, sem):
        wid = lax.axis_index("s") * info.num_cores + lax.axis_index("c")
        base = wid * per_w

        @pl.loop(0, n_chunks)
        def _(j):
            off = pl.multiple_of(base + j * SC_CHUNK, 8)
            pltpu.sync_copy(idx_hbm.at[pl.ds(off, SC_CHUNK)], idx_v)
            pltpu.async_copy(table_hbm.at[idx_v], rows_v, sem).wait()
            pltpu.sync_copy(rows_v, out_hbm.at[pl.ds(off, SC_CHUNK)])

    return k(table, idx)


def _scatter_rows(src, dest_km, n_rows):
    t, d = src.shape
    info = plsc.get_sparse_core_info()
    nw = info.num_cores * info.num_subcores
    per_w = t // nw
    assert per_w * nw == t and per_w % SC_CHUNK == 0
    n_chunks = per_w // SC_CHUNK
    mesh = plsc.VectorSubcoreMesh(core_axis_name="c", subcore_axis_name="s")

    @functools.partial(
        pl.kernel, mesh=mesh,
        out_type=jax.ShapeDtypeStruct((n_rows, d), src.dtype),
        scratch_types=[
            pltpu.VMEM((SC_CHUNK,), I32),
            pltpu.VMEM((SC_CHUNK, d), src.dtype),
            pltpu.SemaphoreType.DMA,
        ],
    )
    def k(src_hbm, dest_hbm, out_hbm, idx_v, rows_v, sem):
        wid = lax.axis_index("s") * info.num_cores + lax.axis_index("c")
        base = wid * per_w

        @pl.loop(0, n_chunks)
        def _(j):
            off = pl.multiple_of(base + j * SC_CHUNK, 8)
            pltpu.sync_copy(src_hbm.at[pl.ds(off, SC_CHUNK)], rows_v)
            for kk in range(TOP_K):
                pltpu.sync_copy(dest_hbm.at[pl.ds(pl.multiple_of(kk * t + off, 8), SC_CHUNK)], idx_v)
                pltpu.async_copy(rows_v, out_hbm.at[idx_v], sem).wait()

    return k(src, dest_km)


def _expert_kernel(be_ref, nused_ref, nvalid_ref, xs_ref, wgu_ref, bgu_ref, wd_ref, bd_ref, ys_ref,
                   wgu_bf, wd_bf):
    i = pl.program_id(0)

    @pl.when(i < nused_ref[0])
    def _():
        @pl.when((i == 0) | (be_ref[i] != be_ref[jnp.maximum(i - 1, 0)]))
        def _():
            rows = 256
            for r0 in range(0, D_MODEL, rows):
                wgu_bf[r0:r0 + rows, :] = wgu_ref[0, r0:r0 + rows, :].astype(BF16)
            for r0 in range(0, D_EXPERT, rows):
                wd_bf[r0:r0 + rows, :] = wd_ref[0, r0:r0 + rows, :].astype(BF16)

        groups = [slice(g * MOE_SUB, (g + 1) * MOE_SUB) for g in range(MOE_BM // MOE_SUB)]

        def load(rows):
            rowid = lax.broadcasted_iota(I32, (MOE_SUB, 1), 0) + rows.start
            words = jnp.where(rowid < nvalid_ref[i], xs_ref[rows, :], jnp.uint32(0))
            lo, hi = _unpack_bf16_pair(words)
            return jnp.concatenate([lo.astype(BF16), hi.astype(BF16)], axis=1)

        def proj(x, col):
            return _dot(x, wgu_bf[:, col:col + D_EXPERT]) + bgu_ref[0, :, col:col + D_EXPERT]

        def activate(gate, up):
            gate = jnp.minimum(gate, SWIGLU_LIMIT)
            up = jnp.clip(up, -SWIGLU_LIMIT, SWIGLU_LIMIT)
            return ((up + 1.0) * gate * _sigmoid(gate * SWIGLU_ALPHA)).astype(BF16)

        xs = [load(rows) for rows in groups]
        gates = [proj(x, 0) for x in xs]
        ups = [proj(x, D_EXPERT) for x in xs]
        acts = [activate(g, u) for g, u in zip(gates, ups)]
        for rows, act in zip(groups, acts):
            y = _dot(act, wd_bf[...]) + bd_ref[0]
            ys_ref[rows, :] = _pack_bf16_pair(y[:, :HALF], y[:, HALF:])


def _experts(xs, block_expert, n_used, n_valid, wgu, bgu, wd, bd):
    n_rows = xs.shape[0]
    bm = MOE_BM
    n_blocks = n_rows // bm
    rows = lambda i, be, nu, nv: (jnp.minimum(i, nu[0] - 1), 0)
    expert = lambda i, be, nu, nv: (be[i], 0, 0)
    grid_spec = pltpu.PrefetchScalarGridSpec(
        num_scalar_prefetch=3,
        grid=(n_blocks,),
        in_specs=[
            pl.BlockSpec((bm, HALF), rows),
            pl.BlockSpec((1, D_MODEL, 2 * D_EXPERT), expert),
            pl.BlockSpec((1, 1, 2 * D_EXPERT), expert),
            pl.BlockSpec((1, D_EXPERT, D_MODEL), expert),
            pl.BlockSpec((1, 1, D_MODEL), expert),
        ],
        out_specs=pl.BlockSpec((bm, HALF), rows),
        scratch_shapes=[
            pltpu.VMEM((D_MODEL, 2 * D_EXPERT), BF16),
            pltpu.VMEM((D_EXPERT, D_MODEL), BF16),
        ],
    )
    return pl.pallas_call(
        _expert_kernel,
        grid_spec=grid_spec,
        out_shape=jax.ShapeDtypeStruct((n_rows, HALF), U32),
        compiler_params=_cparams(("arbitrary",)),
        name="experts",
    )(block_expert, n_used, n_valid, xs, wgu, bgu, wd, bd)


def _combine_kernel(x1_ref, yg_ref, gate_ref, g_ref, b_ref, o_ref):
    gates = gate_ref[...]
    f_lo = None
    f_hi = None
    for kk in range(TOP_K):
        lo, hi = _unpack_bf16_pair(yg_ref[kk])
        w = gates[:, kk:kk + 1]
        f_lo = lo * w if f_lo is None else f_lo + lo * w
        f_hi = hi * w if f_hi is None else f_hi + hi * w
    x1 = x1_ref[...]
    y_lo = DEEPNORM_ALPHA * x1[:, :HALF] + f_lo
    y_hi = DEEPNORM_ALPHA * x1[:, HALF:] + f_hi
    mu = (jnp.sum(y_lo, axis=-1, keepdims=True) + jnp.sum(y_hi, axis=-1, keepdims=True)) / D_MODEL
    d_lo = y_lo - mu
    d_hi = y_hi - mu
    var = (jnp.sum(d_lo * d_lo, axis=-1, keepdims=True)
           + jnp.sum(d_hi * d_hi, axis=-1, keepdims=True)) / D_MODEL
    inv = lax.rsqrt(var + LN_EPS)
    g = g_ref[...]
    b = b_ref[...]
    o_ref[:, :HALF] = d_lo * inv * g[:, :HALF] + b[:, :HALF]
    o_ref[:, HALF:] = d_hi * inv * g[:, HALF:] + b[:, HALF:]


def _combine(x1, yg, gates, g, b):
    t = x1.shape[0]
    tm = PROJ_TM
    const = lambda i: (0, 0)
    return pl.pallas_call(
        _combine_kernel,
        grid=(t // tm,),
        in_specs=[
            pl.BlockSpec((tm, D_MODEL), lambda i: (i, 0)),
            pl.BlockSpec((TOP_K, tm, HALF), lambda i: (0, i, 0)),
            pl.BlockSpec((tm, LANES), lambda i: (i, 0)),
            pl.BlockSpec((1, D_MODEL), const),
            pl.BlockSpec((1, D_MODEL), const),
        ],
        out_specs=pl.BlockSpec((tm, D_MODEL), lambda i: (i, 0)),
        out_shape=jax.ShapeDtypeStruct((t, D_MODEL), F32),
        compiler_params=_cparams(("parallel",)),
        name="combine",
    )(x1, yg, gates, g, b)


def _rope_tables(seq):
    half = ROPE_DIM // 2
    pos = jnp.arange(seq, dtype=F32)
    inv_freq = ROPE_THETA ** (-jnp.arange(0, ROPE_DIM, 2, dtype=F32) / ROPE_DIM)
    ang = pos[:, None] * inv_freq[None, :]
    cos, sin = jnp.cos(ang), jnp.sin(ang)
    zeros = jnp.zeros((seq, LANES - ROPE_DIM), F32)
    cos_f = jnp.concatenate([cos, cos, jnp.ones((seq, LANES - ROPE_DIM), F32)], axis=1)
    sin_a = jnp.concatenate([-sin, jnp.zeros((seq, half), F32), zeros], axis=1)
    sin_b = jnp.concatenate([jnp.zeros((seq, half), F32), sin, zeros], axis=1)
    return cos_f, sin_a, sin_b


def _prep_weights(w_in, gla_gate_w2, gla_gate_b, gla_norm_w, w_o, ln1_g, ln1_b, router_w, router_b,
                  w_gate_up, b_gate_up, w_down, b_down, ln2_g, ln2_b):
    sizes = (GLA_KEY, GLA_KEY, GLA_VAL, GLA_VAL, 2 * GLA_RANK, SWA_HEADS * SWA_HD,
             SWA_KV * SWA_HD, SWA_KV * SWA_HD, D_MODEL, D_MODEL)
    cols, acc = [], 0
    for s in sizes:
        cols.append(w_in[:, acc:acc + s])
        acc += s
    gq, gk, gv, gr, glr, sq, sk, sv, ga, gb = cols
    wa = jnp.concatenate([gq, gk, gv, gr], axis=1).astype(BF16)
    wl = jnp.pad(glr, ((0, 0), (0, LANES - 2 * GLA_RANK))).astype(BF16)
    wb = jnp.concatenate([ga, gb, sq, sk, sv], axis=1).astype(BF16)
    w2f = jnp.zeros((LANES, 2 * GLA_KEY), F32)
    w2f = w2f.at[:GLA_RANK, :GLA_KEY].set(gla_gate_w2[0])
    w2f = w2f.at[GLA_RANK:2 * GLA_RANK, GLA_KEY:].set(gla_gate_w2[1]).astype(BF16)
    rw_hi, rw_lo = _split2(jnp.pad(router_w, ((0, 0), (0, LANES - N_EXPERTS))))
    rw = jnp.concatenate([rw_hi, rw_lo], axis=1)
    rb = jnp.concatenate([router_b, jnp.full((LANES - N_EXPERTS,), -jnp.inf, F32)]).reshape(1, LANES)
    return dict(
        wa=wa, wl=wl, wb=wb, w2f=w2f, bgaf=gla_gate_b.reshape(1, 2 * GLA_KEY),
        nw=gla_norm_w.reshape(1, GLA_VAL), wo=w_o.astype(BF16),
        ln1_g=ln1_g.reshape(1, D_MODEL), ln1_b=ln1_b.reshape(1, D_MODEL), rw=rw, rb=rb,
        wgu=w_gate_up, bgu=b_gate_up.reshape(N_EXPERTS, 1, 2 * D_EXPERT),
        wd=w_down, bd=b_down.reshape(N_EXPERTS, 1, D_MODEL),
        ln2_g=ln2_g.reshape(1, D_MODEL), ln2_b=ln2_b.reshape(1, D_MODEL))


def _routing_plan(idx, rank, counts):
    t = idx.shape[0]
    bm = MOE_BM
    n_rows = t * TOP_K + N_EXPERTS * bm
    n_blocks = n_rows // bm
    counts = counts[0, :N_EXPERTS].astype(I32)
    padded = (counts + bm - 1) // bm * bm
    pad_ends = jnp.cumsum(padded)
    pad_starts = pad_ends - padded
    hot = idx[:, :, None] == jnp.arange(N_EXPERTS, dtype=I32)[None, None, :]
    dest = jnp.sum(jnp.where(hot, pad_starts[None, None, :], 0), axis=-1) + rank
    block_start = jnp.arange(n_blocks, dtype=I32) * bm
    block_expert = jnp.minimum(
        jnp.sum(block_start[:, None] >= pad_ends[None, :], axis=1), N_EXPERTS - 1).astype(I32)
    mine = block_expert[:, None] == jnp.arange(N_EXPERTS, dtype=I32)[None, :]
    seg_end = jnp.sum(jnp.where(mine, (pad_starts + counts)[None, :], 0), axis=1)
    n_valid = jnp.clip(seg_end - block_start, 0, bm).astype(I32)
    n_used = (pad_ends[-1] // bm).astype(I32).reshape(1)
    return dest.T.reshape(-1), n_rows, block_expert, n_used, n_valid


def _trunk(x, w, sinks, gather_rows, scatter_rows):
    bsz, seq, _ = x.shape
    t = bsz * seq
    x2d = x.reshape(t, D_MODEL)
    cos_f, sin_a, sin_b = _rope_tables(seq)
    pa, bdec, pb = _in_proj(x2d, w["wa"], w["wl"], w["w2f"], w["bgaf"], w["wb"],
                            cos_f, sin_a, sin_b, seq)
    oa = _gla(pa, bdec, w["nw"], bsz, seq).reshape(t, GLA_VAL)
    ob = _swa(pb, sinks, bsz, seq).reshape(t, SWA_HEADS * SWA_HD)
    x1, xp, idx, gates, rank, counts = _merge(
        oa, ob, pb, x2d, w["wo"], w["ln1_g"], w["ln1_b"], w["rw"], w["rb"])
    dest_km, n_rows, block_expert, n_used, n_valid = _routing_plan(
        idx[:, :TOP_K], rank[:, :TOP_K], counts)
    xs = scatter_rows(xp, dest_km, n_rows)
    ys = _experts(xs, block_expert, n_used, n_valid, w["wgu"], w["bgu"], w["wd"], w["bd"])
    yg = gather_rows(ys, dest_km).reshape(TOP_K, t, HALF)
    out = _combine(x1, yg, gates, w["ln2_g"], w["ln2_b"])
    return out.reshape(bsz, seq, D_MODEL)


def _forward(x_prompt, x_sample, w_in, gla_gate_w2, gla_gate_b, gla_norm_w, swa_sinks, w_o, ln1_g,
             ln1_b, router_w, router_b, w_gate_up, b_gate_up, w_down, b_down, ln2_g, ln2_b):
    w = _prep_weights(w_in[0], gla_gate_w2[0], gla_gate_b[0], gla_norm_w[0], w_o[0], ln1_g[0],
                      ln1_b[0], router_w[0], router_b[0], w_gate_up[0], b_gate_up[0], w_down[0],
                      b_down[0], ln2_g[0], ln2_b[0])
    sinks = swa_sinks[0].astype(F32)
    return (_trunk(x_prompt, w, sinks, _gather_rows, _scatter_rows),
            _trunk(x_sample, w, sinks, _gather_rows, _scatter_rows))


def kernel(x_prompt, x_sample, w_in, gla_gate_w2, gla_gate_b, gla_norm_w, swa_sinks, w_o, ln1_g, ln1_b, router_w, router_b, w_gate_up, b_gate_up, w_down, b_down, ln2_g, ln2_b):
    return _forward(x_prompt, x_sample, w_in, gla_gate_w2, gla_gate_b, gla_norm_w, swa_sinks, w_o,
                    ln1_g, ln1_b, router_w, router_b, w_gate_up, b_gate_up, w_down, b_down,
                    ln2_g, ln2_b)
```

```python
import functools

import jax
import jax.numpy as jnp
from jax import lax
from jax.experimental import pallas as pl
from jax.experimental.pallas import tpu as pltpu
from jax.experimental.pallas import tpu_sc as plsc

F32 = jnp.float32
BF16 = jnp.bfloat16
I32 = jnp.int32
U32 = jnp.uint32

D_MODEL = 1024
GLA_HEADS = 4
GLA_DK = 128
GLA_DV = 256
GLA_KEY = GLA_HEADS * GLA_DK
GLA_VAL = GLA_HEADS * GLA_DV
GLA_RANK = 16
GLA_NORMALIZER = 16.0
GLA_CHUNK = 64
SWA_HEADS = 8
SWA_KV = 2
SWA_GROUP = SWA_HEADS // SWA_KV
SWA_HD = 128
SWA_WINDOW = 128
ROPE_THETA = 500000.0
ROPE_DIM = SWA_HD // 4
N_EXPERTS = 32
TOP_K = 4
D_EXPERT = D_MODEL
SWIGLU_LIMIT = 7.0
SWIGLU_ALPHA = 1.702
DEEPNORM_ALPHA = 2.0 ** 0.25
LN_EPS = 1e-5
RMS_EPS = 1e-6

LANES = 128
PA_W = 2 * GLA_KEY + 2 * GLA_VAL
PB_W = 2 * D_MODEL + SWA_HEADS * SWA_HD + 2 * SWA_KV * SWA_HD
HALF = D_MODEL // 2

PROJ_TM = 512
MERGE_GROUP = 512
GLA_TS = 2048
GLA_GROUP = 512
SWA_TQ = 2048
MOE_BM = 512
MOE_SUB = 512
SC_CHUNK = 64
VMEM_LIMIT = 56 * 1024 * 1024


def _cparams(sem):
    return pltpu.CompilerParams(dimension_semantics=sem, vmem_limit_bytes=VMEM_LIMIT)


def _split2(a):
    hi = a.astype(BF16)
    lo = (a - hi.astype(F32)).astype(BF16)
    return hi, lo


def _dot(a, b):
    return jnp.dot(a, b, preferred_element_type=F32)


def _dot_nt(a, b):
    return lax.dot_general(a, b, (((1,), (1,)), ((), ())), preferred_element_type=F32)


def _sigmoid(x):
    return 0.5 * jnp.tanh(0.5 * x) + 0.5


def _pack_bf16_pair(lo_f32, hi_f32):
    lo = lax.bitcast_convert_type(lo_f32.astype(BF16).astype(F32), U32)
    hi = lax.bitcast_convert_type(hi_f32.astype(BF16).astype(F32), U32)
    return (hi & jnp.uint32(0xFFFF0000)) | (lo >> 16)


def _unpack_bf16_pair(w):
    lo = lax.bitcast_convert_type(w << 16, F32)
    hi = lax.bitcast_convert_type(w & jnp.uint32(0xFFFF0000), F32)
    return lo, hi


def _chunk_cumsum(x, reverse):
    rows = x.shape[0]
    pos = lax.broadcasted_iota(I32, x.shape, 0) & (GLA_CHUNK - 1)
    s = 1
    while s < GLA_CHUNK:
        if reverse:
            x = x + jnp.where(pos < GLA_CHUNK - s, pltpu.roll(x, rows - s, 0), 0.0)
        else:
            x = x + jnp.where(pos >= s, pltpu.roll(x, s, 0), 0.0)
        s *= 2
    return x


def _in_proj_kernel(x_ref, wa_ref, wl_ref, w2_ref, bga_ref, wb_ref, cos_ref, sina_ref, sinb_ref,
                    pa_ref, bd_ref, pb_ref):
    xb = x_ref[...].astype(BF16)
    lrb = _dot(xb, wl_ref[...]).astype(BF16)

    def decay_slab(j):
        cols = slice(j * LANES, (j + 1) * LANES)
        z = _dot(lrb, w2_ref[:, cols]) + bga_ref[:, cols]
        la = (jnp.minimum(z, 0.0) - jnp.log(1.0 + jnp.exp(-jnp.abs(z)))) * (1.0 / GLA_NORMALIZER)
        bd_ref[:, cols] = _chunk_cumsum(la, reverse=j * LANES >= GLA_KEY)

    n_slabs = 2 * GLA_KEY // LANES
    slab = 0
    step = 512
    for c0 in range(0, PA_W, step):
        r = _dot(xb, wa_ref[:, c0:c0 + step])
        if c0 < GLA_KEY:
            r = r * (GLA_DK ** -0.5)
        pa_ref[:, c0:c0 + step] = r.astype(BF16)
        if slab < n_slabs:
            decay_slab(slab)
            slab += 1
    cos = cos_ref[...]
    sina = sina_ref[...]
    sinb = sinb_ref[...]
    rope0 = 2 * D_MODEL
    rope_end = rope0 + (SWA_HEADS + SWA_KV) * SWA_HD
    for c0 in range(0, PB_W, step):
        r = _dot(xb, wb_ref[:, c0:c0 + step])
        if c0 + step > rope0 and c0 < rope_end:
            parts = []
            for j in range(step // SWA_HD):
                t = r[:, j * SWA_HD:(j + 1) * SWA_HD]
                col = c0 + j * SWA_HD
                if rope0 <= col < rope_end:
                    t = (t * cos + pltpu.roll(t, SWA_HD - ROPE_DIM // 2, 1) * sina
                         + pltpu.roll(t, ROPE_DIM // 2, 1) * sinb)
                if rope0 <= col < rope0 + SWA_HEADS * SWA_HD:
                    t = t * (SWA_HD ** -0.5)
                parts.append(t)
            r = jnp.concatenate(parts, axis=1)
        pb_ref[:, c0:c0 + step] = r.astype(BF16)
        if slab < n_slabs:
            decay_slab(slab)
            slab += 1
    assert slab == n_slabs


def _in_proj(x2d, wa, wl, w2f, bgaf, wb, cos, sina, sinb, seq):
    t = x2d.shape[0]
    tm = PROJ_TM
    nseq = seq // tm
    const = lambda i: (0, 0)
    return pl.pallas_call(
        _in_proj_kernel,
        grid=(t // tm,),
        in_specs=[
            pl.BlockSpec((tm, D_MODEL), lambda i: (i, 0)),
            pl.BlockSpec((D_MODEL, PA_W), const),
            pl.BlockSpec((D_MODEL, LANES), const),
            pl.BlockSpec((LANES, 2 * GLA_KEY), const),
            pl.BlockSpec((1, 2 * GLA_KEY), const),
            pl.BlockSpec((D_MODEL, PB_W), const),
            pl.BlockSpec((tm, LANES), lambda i: (i % nseq, 0)),
            pl.BlockSpec((tm, LANES), lambda i: (i % nseq, 0)),
            pl.BlockSpec((tm, LANES), lambda i: (i % nseq, 0)),
        ],
        out_specs=[
            pl.BlockSpec((tm, PA_W), lambda i: (i, 0)),
            pl.BlockSpec((tm, 2 * GLA_KEY), lambda i: (i, 0)),
            pl.BlockSpec((tm, PB_W), lambda i: (i, 0)),
        ],
        out_shape=[
            jax.ShapeDtypeStruct((t, PA_W), BF16),
            jax.ShapeDtypeStruct((t, 2 * GLA_KEY), F32),
            jax.ShapeDtypeStruct((t, PB_W), BF16),
        ],
        compiler_params=_cparams(("parallel",)),
        name="in_proj",
    )(x2d, wa, wl, w2f, bgaf, wb, cos, sina, sinb)


def _gla_kernel(q_ref, k_ref, v_ref, r_ref, b_ref, nw_ref, o_ref,
                state_ref, ofwd_ref, *group_scratch, ts, nt):
    ph = pl.program_id(2)
    t = pl.program_id(3)
    c = GLA_CHUNK
    gs = min(GLA_GROUP, ts)
    n_groups = ts // gs
    nc = gs // c

    @pl.when(t == 0)
    def _():
        state_ref[...] = jnp.zeros_like(state_ref)

    row = lax.broadcasted_iota(I32, (c, c), 0)
    col = lax.broadcasted_iota(I32, (c, c), 1)

    def run(forward):
        keep = (col <= row) if forward else (col > row)
        mid = c // 2 if forward else c // 2 - 1
        end = c - 1 if forward else 0
        off = pl.multiple_of((t if forward else nt - 1 - t) * ts, ts)
        group_order = range(n_groups) if forward else reversed(range(n_groups))
        chunk_order = list(range(nc)) if forward else list(reversed(range(nc)))
        st = state_ref[...]
        for g in group_order:
            sall_ref, ds_ref, dec_ref = group_scratch[3 * g:3 * g + 3]
            rows = slice(g * gs, (g + 1) * gs)
            q3 = q_ref[0, rows, :].reshape(nc, c, GLA_DK)
            k3 = k_ref[0, rows, :].reshape(nc, c, GLA_DK)
            v3 = v_ref[0, rows, :].reshape(nc, c, GLA_DV)
            b3 = b_ref[0, rows, :].reshape(nc, c, GLA_DK)
            b_mid = b3[:, mid:mid + 1]
            b_end = b3[:, end:end + 1]
            q_in = q3 * jnp.exp(b3 - b_mid).astype(BF16)
            k_in = k3 * jnp.exp(b_mid - b3).astype(BF16)
            k_st = k_in * jnp.exp(b_end - b_mid).astype(BF16)
            q_b = q_in * jnp.exp(b_mid).astype(BF16)
            dec_ref[...] = jnp.exp(b_end)
            scores = jnp.einsum("cid,cjd->cij", q_in, k_in, preferred_element_type=F32)
            scores = jnp.where(keep[None], scores, 0.0).astype(BF16)
            o_intra = jnp.einsum("cij,cjv->civ", scores, v3, preferred_element_type=F32)
            ds_ref[...] = jnp.einsum("cjv,cjd->cvd", v3, k_st, preferred_element_type=F32)
            for ci in chunk_order:
                sall_ref[ci] = st.astype(BF16)
                st = st * dec_ref[ci] + ds_ref[ci]
            o_inter = jnp.einsum("cid,cvd->civ", q_b, sall_ref[...], preferred_element_type=F32)
            o = (o_intra + o_inter).reshape(gs, GLA_DV)
            orows = pl.ds(pl.multiple_of(off + g * gs, gs), gs)
            if forward:
                ofwd_ref[orows, :] = o
            else:
                tot = o + ofwd_ref[orows, :]
                tot = tot * lax.rsqrt(jnp.mean(tot * tot, axis=-1, keepdims=True) + RMS_EPS)
                r = r_ref[0, rows, :]
                o_ref[0, rows, :] = (
                    tot * nw_ref[...] * (r * _sigmoid(r)).astype(F32)).astype(BF16)
        state_ref[...] = st

    @pl.when(ph == 0)
    def _():
        run(True)

    @pl.when(ph == 1)
    def _():
        run(False)


def _gla(pa, bdec, nw, bsz, seq):
    ts = min(GLA_TS, seq)
    nt = seq // ts
    gs = min(GLA_GROUP, ts)
    nc = gs // GLA_CHUNK
    group_scratch = []
    for _ in range(ts // gs):
        group_scratch += [pltpu.VMEM((nc, GLA_DV, GLA_DK), BF16),
                          pltpu.VMEM((nc, GLA_DV, GLA_DK), F32),
                          pltpu.VMEM((nc, 1, GLA_DK), F32)]
    pa3 = pa.reshape(bsz, seq, PA_W)
    bd3 = bdec.reshape(bsz, seq, 2 * GLA_KEY)
    tile = lambda p, t: t + p * (nt - 1 - 2 * t)
    wk, wv = GLA_DK, GLA_DV
    kq = GLA_KEY // wk
    kv = 2 * GLA_KEY // wv
    kr = kv + GLA_VAL // wv
    kern = functools.partial(_gla_kernel, ts=ts, nt=nt)
    return pl.pallas_call(
        kern,
        grid=(bsz, GLA_HEADS, 2, nt),
        in_specs=[
            pl.BlockSpec((1, ts, wk), lambda b, h, p, t: (b, tile(p, t), h)),
            pl.BlockSpec((1, ts, wk), lambda b, h, p, t: (b, tile(p, t), kq + h)),
            pl.BlockSpec((1, ts, wv), lambda b, h, p, t: (b, tile(p, t), kv + h)),
            pl.BlockSpec((1, ts, wv), lambda b, h, p, t: (b, nt - 1 - p * t, kr + h)),
            pl.BlockSpec((1, ts, wk), lambda b, h, p, t: (b, tile(p, t), p * kq + h)),
            pl.BlockSpec((1, wv), lambda b, h, p, t: (0, h)),
        ],
        out_specs=pl.BlockSpec((1, ts, wv), lambda b, h, p, t: (b, nt - 1 - p * t, h)),
        out_shape=jax.ShapeDtypeStruct((bsz, seq, GLA_VAL), BF16),
        scratch_shapes=[
            pltpu.VMEM((GLA_DV, GLA_DK), F32),
            pltpu.VMEM((seq, GLA_DV), F32),
        ] + group_scratch,
        compiler_params=_cparams(("parallel", "parallel", "arbitrary", "arbitrary")),
        name="gla",
    )(pa3, pa3, pa3, pa3, bd3, nw)


def _swa_kernel(sink_ref, q_ref, kl_ref, kc_ref, kr_ref, vl_ref, vc_ref, vr_ref, o_ref, *, tq):
    kvh = pl.program_id(1)
    n = pl.program_id(2)
    blk = SWA_WINDOW
    k_all = jnp.concatenate([kl_ref[0], kc_ref[0], kr_ref[0]], axis=0)
    v_all = jnp.concatenate([vl_ref[0], vc_ref[0], vr_ref[0]], axis=0)
    rows = SWA_GROUP * blk
    qi = lax.broadcasted_iota(I32, (rows, 3 * blk), 0) & (blk - 1)
    kj = lax.broadcasted_iota(I32, (rows, 3 * blk), 1) - blk
    band = jnp.where(jnp.abs(kj - qi) <= SWA_WINDOW, 0.0, -jnp.inf)
    head = lax.broadcasted_iota(I32, (rows, 1), 0) // blk
    sink = jnp.zeros((rows, 1), F32)
    for g in range(SWA_GROUP):
        sink = jnp.where(head == g, sink_ref[kvh * SWA_GROUP + g], sink)
    n_sub = tq // blk
    before = jnp.where(n == 0, -jnp.inf, 0.0)
    after = jnp.where(n == pl.num_programs(2) - 1, -jnp.inf, 0.0)
    for jq in range(n_sub):
        bias = band
        if jq == 0:
            bias = bias + jnp.where(kj < 0, before, 0.0)
        if jq == n_sub - 1:
            bias = bias + jnp.where(kj >= blk, after, 0.0)
        qs = q_ref[0, jq * blk:(jq + 1) * blk, :]
        q4 = jnp.concatenate([qs[:, g * SWA_HD:(g + 1) * SWA_HD] for g in range(SWA_GROUP)], axis=0)
        keys = k_all[jq * blk:(jq + 3) * blk]
        vals = v_all[jq * blk:(jq + 3) * blk]
        s = _dot_nt(q4, keys) + bias
        m = jnp.maximum(jnp.max(s, axis=-1, keepdims=True), sink)
        p = jnp.exp(s - m)
        denom = jnp.sum(p, axis=-1, keepdims=True) + jnp.exp(sink - m)
        o = _dot(p.astype(BF16), vals) * (1.0 / denom)
        for g in range(SWA_GROUP):
            o_ref[0, jq * blk:(jq + 1) * blk, g * SWA_HD:(g + 1) * SWA_HD] = (
                o[g * blk:(g + 1) * blk].astype(BF16))


def _swa(pb, sinks, bsz, seq):
    tq = min(SWA_TQ, seq)
    blk = SWA_WINDOW
    r = tq // blk
    nb = seq // blk
    pb3 = pb.reshape(bsz, seq, PB_W)
    gw = SWA_GROUP * SWA_HD
    cq = 2 * D_MODEL // gw
    ck = (2 * D_MODEL + SWA_HEADS * SWA_HD) // SWA_HD
    cv = ck + SWA_KV
    left = lambda n: jnp.maximum(n * r - 1, 0)
    right = lambda n: jnp.minimum(n * r + r, nb - 1)
    kern = functools.partial(_swa_kernel, tq=tq)
    grid_spec = pltpu.PrefetchScalarGridSpec(
        num_scalar_prefetch=1,
        grid=(bsz, SWA_KV, seq // tq),
        in_specs=[
            pl.BlockSpec((1, tq, gw), lambda b, h, n, s: (b, n, cq + h)),
            pl.BlockSpec((1, blk, SWA_HD), lambda b, h, n, s: (b, left(n), ck + h)),
            pl.BlockSpec((1, tq, SWA_HD), lambda b, h, n, s: (b, n, ck + h)),
            pl.BlockSpec((1, blk, SWA_HD), lambda b, h, n, s: (b, right(n), ck + h)),
            pl.BlockSpec((1, blk, SWA_HD), lambda b, h, n, s: (b, left(n), cv + h)),
            pl.BlockSpec((1, tq, SWA_HD), lambda b, h, n, s: (b, n, cv + h)),
            pl.BlockSpec((1, blk, SWA_HD), lambda b, h, n, s: (b, right(n), cv + h)),
        ],
        out_specs=pl.BlockSpec((1, tq, gw), lambda b, h, n, s: (b, n, h)),
    )
    return pl.pallas_call(
        kern,
        grid_spec=grid_spec,
        out_shape=jax.ShapeDtypeStruct((bsz, seq, SWA_HEADS * SWA_HD), BF16),
        compiler_params=_cparams(("parallel", "parallel", "parallel")),
        name="swa",
    )(sinks, pb3, pb3, pb3, pb3, pb3, pb3, pb3)


def _layer_norm(y, g, b):
    mu = jnp.mean(y, axis=-1, keepdims=True)
    d = y - mu
    var = jnp.mean(d * d, axis=-1, keepdims=True)
    return d * lax.rsqrt(var + LN_EPS) * g + b


def _merge_kernel(oa_ref, ob_ref, ga_ref, gb_ref, x_ref, wo_ref, g_ref, b_ref, rw_ref, rb_ref,
                  before_ref, x1_ref, xp_ref, gate_ref, route_ref, cnt_ref, run_ref, *, tm):
    @pl.when(pl.program_id(0) == 0)
    def _():
        run_ref[...] = jnp.zeros_like(run_ref)

    gm = MERGE_GROUP
    lane_i = lax.broadcasted_iota(I32, (gm, LANES), 1)
    lane = lane_i.astype(F32)

    def spread(cols):
        out = jnp.zeros((gm, LANES), F32)
        for kk, col in enumerate(cols):
            out = jnp.where(lane_i == kk, col, out)
        return out

    run = run_ref[...]
    for r0 in range(0, tm, gm):
        rows = slice(r0, r0 + gm)
        h = (_sigmoid(ga_ref[rows, :]) * oa_ref[rows, :]
             + _sigmoid(gb_ref[rows, :]) * ob_ref[rows, :])
        y = DEEPNORM_ALPHA * x_ref[rows, :] + _dot(h, wo_ref[...])
        x1 = _layer_norm(y, g_ref[...], b_ref[...])
        x1_ref[rows, :] = x1
        xp_ref[rows, :] = _pack_bf16_pair(x1[:, :HALF], x1[:, HALF:])

        x_hi, x_lo = _split2(x1)
        both = _dot(x_hi, rw_ref[...])
        logits = (both[:, :LANES] + both[:, LANES:] + _dot(x_lo, rw_ref[:, :LANES])
                  + rb_ref[...])
        work = logits
        vals, idxs, hots = [], [], []
        for _ in range(TOP_K):
            m = jnp.max(work, axis=-1, keepdims=True)
            sel = jnp.min(jnp.where(work == m, lane, float(LANES)), axis=-1, keepdims=True)
            hot = lane == sel
            vals.append(m)
            idxs.append(sel)
            hots.append(hot)
            work = jnp.where(hot, -jnp.inf, work)
        e = [jnp.exp(v - vals[0]) for v in vals]
        tot = e[0] + e[1] + e[2] + e[3]
        gate_ref[rows, :] = spread([ei / tot for ei in e])

        member = (hots[0] | hots[1] | hots[2] | hots[3])
        cum = _dot(before_ref[...], member.astype(BF16)) + run
        ranks = [jnp.sum(jnp.where(hot, cum, 0.0), axis=-1, keepdims=True) for hot in hots]
        route = spread(idxs + ranks).T
        route_ref[:, rows] = route[:2 * TOP_K].astype(I32)
        run = run + jnp.sum(member.astype(F32), axis=0, keepdims=True)
    run_ref[...] = run
    cnt_ref[...] = run


def _merge(oa, ob, pb, x2d, wo, g, b, rw, rb):
    t = x2d.shape[0]
    tm = PROJ_TM
    const = lambda i: (0, 0)
    kern = functools.partial(_merge_kernel, tm=tm)
    gm = MERGE_GROUP
    before = (jnp.arange(gm)[None, :] < jnp.arange(gm)[:, None]).astype(BF16)
    return pl.pallas_call(
        kern,
        grid=(t // tm,),
        in_specs=[
            pl.BlockSpec((tm, D_MODEL), lambda i: (i, 0)),
            pl.BlockSpec((tm, D_MODEL), lambda i: (i, 0)),
            pl.BlockSpec((tm, D_MODEL), lambda i: (i, 0)),
            pl.BlockSpec((tm, D_MODEL), lambda i: (i, 1)),
            pl.BlockSpec((tm, D_MODEL), lambda i: (i, 0)),
            pl.BlockSpec((D_MODEL, D_MODEL), const),
            pl.BlockSpec((1, D_MODEL), const),
            pl.BlockSpec((1, D_MODEL), const),
            pl.BlockSpec((D_MODEL, 2 * LANES), const),
            pl.BlockSpec((1, LANES), const),
            pl.BlockSpec((gm, gm), const),
        ],
        out_specs=[
            pl.BlockSpec((tm, D_MODEL), lambda i: (i, 0)),
            pl.BlockSpec((tm, HALF), lambda i: (i, 0)),
            pl.BlockSpec((tm, LANES), lambda i: (i, 0)),
            pl.BlockSpec((2 * TOP_K, tm), lambda i: (0, i)),
            pl.BlockSpec((1, LANES), const),
        ],
        out_shape=[
            jax.ShapeDtypeStruct((t, D_MODEL), F32),
            jax.ShapeDtypeStruct((t, HALF), U32),
            jax.ShapeDtypeStruct((t, LANES), F32),
            jax.ShapeDtypeStruct((2 * TOP_K, t), I32),
            jax.ShapeDtypeStruct((1, LANES), F32),
        ],
        scratch_shapes=[pltpu.VMEM((1, LANES), F32)],
        compiler_params=_cparams(("arbitrary",)),
        name="merge",
    )(oa, ob, pb, pb, x2d, wo, g, b, rw, rb, before)


def _gather_rows(table, idx):
    n = idx.shape[0]
    d = table.shape[1]
    info = plsc.get_sparse_core_info()
    nw = info.num_cores * info.num_subcores
    per_w = n // nw
    assert per_w * nw == n and per_w % SC_CHUNK == 0
    n_chunks = per_w // SC_CHUNK
    mesh = plsc.VectorSubcoreMesh(core_axis_name="c", subcore_axis_name="s")

    @functools.partial(
        pl.kernel, mesh=mesh,
        out_type=jax.ShapeDtypeStruct((n, d), table.dtype),
        scratch_types=[
            pltpu.VMEM((SC_CHUNK,), I32),
            pltpu.VMEM((SC_CHUNK, d), table.dtype),
            pltpu.SemaphoreType.DMA,
        ],
    )
    def k(table_hbm, idx_hbm, out_hbm, idx_v, rows_v, sem):
        wid = lax.axis_index("s") * info.num_cores + lax.axis_index("c")
        base = wid * per_w

        @pl.loop(0, n_chunks)
        def _(j):
            off = pl.multiple_of(base + j * SC_CHUNK, 8)
            pltpu.sync_copy(idx_hbm.at[pl.ds(off, SC_CHUNK)], idx_v)
            pltpu.async_copy(table_hbm.at[idx_v], rows_v, sem).wait()
            pltpu.sync_copy(rows_v, out_hbm.at[pl.ds(off, SC_CHUNK)])

    return k(table, idx)


def _scatter_rows(src, dest_km, n_rows):
    t, d = src.shape
    info = plsc.get_sparse_core_info()
    nw = info.num_cores * info.num_subcores
    per_w = t // nw
    assert per_w * nw == t and per_w % SC_CHUNK == 0
    n_chunks = per_w // SC_CHUNK
    mesh = plsc.VectorSubcoreMesh(core_axis_name="c", subcore_axis_name="s")

    @functools.partial(
        pl.kernel, mesh=mesh,
        out_type=jax.ShapeDtypeStruct((n_rows, d), src.dtype),
        scratch_types=[
            pltpu.VMEM((SC_CHUNK,), I32),
            pltpu.VMEM((SC_CHUNK, d), src.dtype),
            pltpu.SemaphoreType.DMA,
        ],
    )
    def k(src_hbm, dest_hbm, out_hbm, idx_v, rows_v, sem):
        wid = lax.axis_index("s") * info.num_cores + lax.axis_index("c")
        base = wid * per_w

        @pl.loop(0, n_chunks)
        def _(j):
            off = pl.multiple_of(base + j * SC_CHUNK, 8)
            pltpu.sync_copy(src_hbm.at[pl.ds(off, SC_CHUNK)], rows_v)
            for kk in range(TOP_K):
                pltpu.sync_copy(dest_hbm.at[pl.ds(pl.multiple_of(kk * t + off, 8), SC_CHUNK)], idx_v)
                pltpu.async_copy(rows_v, out_hbm.at[idx_v], sem).wait()

    return k(src, dest_km)


def _expert_kernel(be_ref, nused_ref, nvalid_ref, xs_ref, wgu_ref, bgu_ref, wd_ref, bd_ref, ys_ref,
                   wgu_bf, wd_bf):
    i = pl.program_id(0)

    @pl.when(i < nused_ref[0])
    def _():
        @pl.when((i == 0) | (be_ref[i] != be_ref[jnp.maximum(i - 1, 0)]))
        def _():
            rows = 256
            for r0 in range(0, D_MODEL, rows):
                wgu_bf[r0:r0 + rows, :] = wgu_ref[0, r0:r0 + rows, :].astype(BF16)
            for r0 in range(0, D_EXPERT, rows):
                wd_bf[r0:r0 + rows, :] = wd_ref[0, r0:r0 + rows, :].astype(BF16)

        groups = [slice(g * MOE_SUB, (g + 1) * MOE_SUB) for g in range(MOE_BM // MOE_SUB)]

        def load(rows):
            rowid = lax.broadcasted_iota(I32, (MOE_SUB, 1), 0) + rows.start
            words = jnp.where(rowid < nvalid_ref[i], xs_ref[rows, :], jnp.uint32(0))
            lo, hi = _unpack_bf16_pair(words)
            return jnp.concatenate([lo.astype(BF16), hi.astype(BF16)], axis=1)

        def proj(x, col):
            return _dot(x, wgu_bf[:, col:col + D_EXPERT]) + bgu_ref[0, :, col:col + D_EXPERT]

        def activate(gate, up):
            gate = jnp.minimum(gate, SWIGLU_LIMIT)
            up = jnp.clip(up, -SWIGLU_LIMIT, SWIGLU_LIMIT)
            return ((up + 1.0) * gate * _sigmoid(gate * SWIGLU_ALPHA)).astype(BF16)

        xs = [load(rows) for rows in groups]
        gates = [proj(x, 0) for x in xs]
        ups = [proj(x, D_EXPERT) for x in xs]
        acts = [activate(g, u) for g, u in zip(gates, ups)]
        for rows, act in zip(groups, acts):
            y = _dot(act, wd_bf[...]) + bd_ref[0]
            ys_ref[rows, :] = _pack_bf16_pair(y[:, :HALF], y[:, HALF:])


def _experts(xs, block_expert, n_used, n_valid, wgu, bgu, wd, bd):
    n_rows = xs.shape[0]
    bm = MOE_BM
    n_blocks = n_rows // bm
    rows = lambda i, be, nu, nv: (jnp.minimum(i, nu[0] - 1), 0)
    expert = lambda i, be, nu, nv: (be[i], 0, 0)
    grid_spec = pltpu.PrefetchScalarGridSpec(
        num_scalar_prefetch=3,
        grid=(n_blocks,),
        in_specs=[
            pl.BlockSpec((bm, HALF), rows),
            pl.BlockSpec((1, D_MODEL, 2 * D_EXPERT), expert),
            pl.BlockSpec((1, 1, 2 * D_EXPERT), expert),
            pl.BlockSpec((1, D_EXPERT, D_MODEL), expert),
            pl.BlockSpec((1, 1, D_MODEL), expert),
        ],
        out_specs=pl.BlockSpec((bm, HALF), rows),
        scratch_shapes=[
            pltpu.VMEM((D_MODEL, 2 * D_EXPERT), BF16),
            pltpu.VMEM((D_EXPERT, D_MODEL), BF16),
        ],
    )
    return pl.pallas_call(
        _expert_kernel,
        grid_spec=grid_spec,
        out_shape=jax.ShapeDtypeStruct((n_rows, HALF), U32),
        compiler_params=_cparams(("arbitrary",)),
        name="experts",
    )(block_expert, n_used, n_valid, xs, wgu, bgu, wd, bd)


def _combine_kernel(x1_ref, yg_ref, gate_ref, g_ref, b_ref, o_ref):
    gates = gate_ref[...]
    f_lo = None
    f_hi = None
    for kk in range(TOP_K):
        lo, hi = _unpack_bf16_pair(yg_ref[kk])
        w = gates[:, kk:kk + 1]
        f_lo = lo * w if f_lo is None else f_lo + lo * w
        f_hi = hi * w if f_hi is None else f_hi + hi * w
    x1 = x1_ref[...]
    y_lo = DEEPNORM_ALPHA * x1[:, :HALF] + f_lo
    y_hi = DEEPNORM_ALPHA * x1[:, HALF:] + f_hi
    mu = (jnp.sum(y_lo, axis=-1, keepdims=True) + jnp.sum(y_hi, axis=-1, keepdims=True)) / D_MODEL
    d_lo = y_lo - mu
    d_hi = y_hi - mu
    var = (jnp.sum(d_lo * d_lo, axis=-1, keepdims=True)
           + jnp.sum(d_hi * d_hi, axis=-1, keepdims=True)) / D_MODEL
    inv = lax.rsqrt(var + LN_EPS)
    g = g_ref[...]
    b = b_ref[...]
    o_ref[:, :HALF] = d_lo * inv * g[:, :HALF] + b[:, :HALF]
    o_ref[:, HALF:] = d_hi * inv * g[:, HALF:] + b[:, HALF:]


def _combine(x1, yg, gates, g, b):
    t = x1.shape[0]
    tm = PROJ_TM
    const = lambda i: (0, 0)
    return pl.pallas_call(
        _combine_kernel,
        grid=(t // tm,),
        in_specs=[
            pl.BlockSpec((tm, D_MODEL), lambda i: (i, 0)),
            pl.BlockSpec((TOP_K, tm, HALF), lambda i: (0, i, 0)),
            pl.BlockSpec((tm, LANES), lambda i: (i, 0)),
            pl.BlockSpec((1, D_MODEL), const),
            pl.BlockSpec((1, D_MODEL), const),
        ],
        out_specs=pl.BlockSpec((tm, D_MODEL), lambda i: (i, 0)),
        out_shape=jax.ShapeDtypeStruct((t, D_MODEL), F32),
        compiler_params=_cparams(("parallel",)),
        name="combine",
    )(x1, yg, gates, g, b)


def _rope_tables(seq):
    half = ROPE_DIM // 2
    pos = jnp.arange(seq, dtype=F32)
    inv_freq = ROPE_THETA ** (-jnp.arange(0, ROPE_DIM, 2, dtype=F32) / ROPE_DIM)
    ang = pos[:, None] * inv_freq[None, :]
    cos, sin = jnp.cos(ang), jnp.sin(ang)
    zeros = jnp.zeros((seq, LANES - ROPE_DIM), F32)
    cos_f = jnp.concatenate([cos, cos, jnp.ones((seq, LANES - ROPE_DIM), F32)], axis=1)
    sin_a = jnp.concatenate([-sin, jnp.zeros((seq, half), F32), zeros], axis=1)
    sin_b = jnp.concatenate([jnp.zeros((seq, half), F32), sin, zeros], axis=1)
    return cos_f, sin_a, sin_b


def _prep_weights(w_in, gla_gate_w2, gla_gate_b, gla_norm_w, w_o, ln1_g, ln1_b, router_w, router_b,
                  w_gate_up, b_gate_up, w_down, b_down, ln2_g, ln2_b):
    sizes = (GLA_KEY, GLA_KEY, GLA_VAL, GLA_VAL, 2 * GLA_RANK, SWA_HEADS * SWA_HD,
             SWA_KV * SWA_HD, SWA_KV * SWA_HD, D_MODEL, D_MODEL)
    cols, acc = [], 0
    for s in sizes:
        cols.append(w_in[:, acc:acc + s])
        acc += s
    gq, gk, gv, gr, glr, sq, sk, sv, ga, gb = cols
    wa = jnp.concatenate([gq, gk, gv, gr], axis=1).astype(BF16)
    wl = jnp.pad(glr, ((0, 0), (0, LANES - 2 * GLA_RANK))).astype(BF16)
    wb = jnp.concatenate([ga, gb, sq, sk, sv], axis=1).astype(BF16)
    w2f = jnp.zeros((LANES, 2 * GLA_KEY), F32)
    w2f = w2f.at[:GLA_RANK, :GLA_KEY].set(gla_gate_w2[0])
    w2f = w2f.at[GLA_RANK:2 * GLA_RANK, GLA_KEY:].set(gla_gate_w2[1]).astype(BF16)
    rw_hi, rw_lo = _split2(jnp.pad(router_w, ((0, 0), (0, LANES - N_EXPERTS))))
    rw = jnp.concatenate([rw_hi, rw_lo], axis=1)
    rb = jnp.concatenate([router_b, jnp.full((LANES - N_EXPERTS,), -jnp.inf, F32)]).reshape(1, LANES)
    return dict(
        wa=wa, wl=wl, wb=wb, w2f=w2f, bgaf=gla_gate_b.reshape(1, 2 * GLA_KEY),
        nw=gla_norm_w.reshape(1, GLA_VAL), wo=w_o.astype(BF16),
        ln1_g=ln1_g.reshape(1, D_MODEL), ln1_b=ln1_b.reshape(1, D_MODEL), rw=rw, rb=rb,
        wgu=w_gate_up, bgu=b_gate_up.reshape(N_EXPERTS, 1, 2 * D_EXPERT),
        wd=w_down, bd=b_down.reshape(N_EXPERTS, 1, D_MODEL),
        ln2_g=ln2_g.reshape(1, D_MODEL), ln2_b=ln2_b.reshape(1, D_MODEL))


def _routing_plan(route, counts):
    idx, rank = route[:TOP_K], route[TOP_K:]
    t = route.shape[1]
    bm = MOE_BM
    n_rows = t * TOP_K + N_EXPERTS * bm
    n_blocks = n_rows // bm
    counts = counts[0, :N_EXPERTS].astype(I32)
    padded = (counts + bm - 1) // bm * bm
    pad_ends = jnp.cumsum(padded)
    pad_starts = pad_ends - padded
    dest = rank
    for ex in range(N_EXPERTS):
        dest = dest + jnp.where(idx == ex, pad_starts[ex], 0)
    block_start = jnp.arange(n_blocks, dtype=I32) * bm
    block_expert = jnp.minimum(
        jnp.sum(block_start[:, None] >= pad_ends[None, :], axis=1), N_EXPERTS - 1).astype(I32)
    mine = block_expert[:, None] == jnp.arange(N_EXPERTS, dtype=I32)[None, :]
    seg_end = jnp.sum(jnp.where(mine, (pad_starts + counts)[None, :], 0), axis=1)
    n_valid = jnp.clip(seg_end - block_start, 0, bm).astype(I32)
    n_used = (pad_ends[-1] // bm).astype(I32).reshape(1)
    return dest.reshape(-1), n_rows, block_expert, n_used, n_valid


def _trunk(x, w, sinks, gather_rows, scatter_rows):
    bsz, seq, _ = x.shape
    t = bsz * seq
    x2d = x.reshape(t, D_MODEL)
    cos_f, sin_a, sin_b = _rope_tables(seq)
    pa, bdec, pb = _in_proj(x2d, w["wa"], w["wl"], w["w2f"], w["bgaf"], w["wb"],
                            cos_f, sin_a, sin_b, seq)
    oa = _gla(pa, bdec, w["nw"], bsz, seq).reshape(t, GLA_VAL)
    ob = _swa(pb, sinks, bsz, seq).reshape(t, SWA_HEADS * SWA_HD)
    x1, xp, gates, route, counts = _merge(
        oa, ob, pb, x2d, w["wo"], w["ln1_g"], w["ln1_b"], w["rw"], w["rb"])
    dest_km, n_rows, block_expert, n_used, n_valid = _routing_plan(route, counts)
    xs = scatter_rows(xp, dest_km, n_rows)
    ys = _experts(xs, block_expert, n_used, n_valid, w["wgu"], w["bgu"], w["wd"], w["bd"])
    yg = gather_rows(ys, dest_km).reshape(TOP_K, t, HALF)
    out = _combine(x1, yg, gates, w["ln2_g"], w["ln2_b"])
    return out.reshape(bsz, seq, D_MODEL)


def _forward(x_prompt, x_sample, w_in, gla_gate_w2, gla_gate_b, gla_norm_w, swa_sinks, w_o, ln1_g,
             ln1_b, router_w, router_b, w_gate_up, b_gate_up, w_down, b_down, ln2_g, ln2_b):
    w = _prep_weights(w_in[0], gla_gate_w2[0], gla_gate_b[0], gla_norm_w[0], w_o[0], ln1_g[0],
                      ln1_b[0], router_w[0], router_b[0], w_gate_up[0], b_gate_up[0], w_down[0],
                      b_down[0], ln2_g[0], ln2_b[0])
    sinks = swa_sinks[0].astype(F32)
    return (_trunk(x_prompt, w, sinks, _gather_rows, _scatter_rows),
            _trunk(x_sample, w, sinks, _gather_rows, _scatter_rows))


def kernel(x_prompt, x_sample, w_in, gla_gate_w2, gla_gate_b, gla_norm_w, swa_sinks, w_o, ln1_g, ln1_b, router_w, router_b, w_gate_up, b_gate_up, w_down, b_down, ln2_g, ln2_b):
    return _forward(x_prompt, x_sample, w_in, gla_gate_w2, gla_gate_b, gla_norm_w, swa_sinks, w_o,
                    ln1_g, ln1_b, router_w, router_b, w_gate_up, b_gate_up, w_down, b_down,
                    ln2_g, ln2_b)
```

```python
import functools

import jax
import jax.numpy as jnp
from jax import lax
from jax.experimental import pallas as pl
from jax.experimental.pallas import tpu as pltpu
from jax.experimental.pallas import tpu_sc as plsc

F32 = jnp.float32
BF16 = jnp.bfloat16
I32 = jnp.int32
U32 = jnp.uint32

D_MODEL = 1024
GLA_HEADS = 4
GLA_DK = 128
GLA_DV = 256
GLA_KEY = GLA_HEADS * GLA_DK
GLA_VAL = GLA_HEADS * GLA_DV
GLA_RANK = 16
GLA_NORMALIZER = 16.0
GLA_CHUNK = 64
SWA_HEADS = 8
SWA_KV = 2
SWA_GROUP = SWA_HEADS // SWA_KV
SWA_HD = 128
SWA_WINDOW = 128
ROPE_THETA = 500000.0
ROPE_DIM = SWA_HD // 4
N_EXPERTS = 32
TOP_K = 4
D_EXPERT = D_MODEL
SWIGLU_LIMIT = 7.0
SWIGLU_ALPHA = 1.702
DEEPNORM_ALPHA = 2.0 ** 0.25
LN_EPS = 1e-5
RMS_EPS = 1e-6

LANES = 128
PA_W = 2 * GLA_KEY + 2 * GLA_VAL
PB_W = 2 * D_MODEL + SWA_HEADS * SWA_HD + 2 * SWA_KV * SWA_HD
HALF = D_MODEL // 2

PROJ_TM = 512
MERGE_GROUP = 512
GLA_TS = 2048
GLA_GROUP = 512
SWA_TQ = 2048
MOE_BM = 1024
MOE_SUB = 1024
SC_CHUNK = 64
VMEM_LIMIT = 56 * 1024 * 1024


def _cparams(sem):
    return pltpu.CompilerParams(dimension_semantics=sem, vmem_limit_bytes=VMEM_LIMIT)


def _split2(a):
    hi = a.astype(BF16)
    lo = (a - hi.astype(F32)).astype(BF16)
    return hi, lo


def _dot(a, b):
    return jnp.dot(a, b, preferred_element_type=F32)


def _dot_nt(a, b):
    return lax.dot_general(a, b, (((1,), (1,)), ((), ())), preferred_element_type=F32)


def _sigmoid(x):
    return 0.5 * jnp.tanh(0.5 * x) + 0.5


def _pack_bf16_pair(lo_f32, hi_f32):
    lo = lax.bitcast_convert_type(lo_f32.astype(BF16).astype(F32), U32)
    hi = lax.bitcast_convert_type(hi_f32.astype(BF16).astype(F32), U32)
    return (hi & jnp.uint32(0xFFFF0000)) | (lo >> 16)


def _unpack_bf16_pair(w):
    lo = lax.bitcast_convert_type(w << 16, F32)
    hi = lax.bitcast_convert_type(w & jnp.uint32(0xFFFF0000), F32)
    return lo, hi


def _chunk_cumsum(x, reverse):
    rows = x.shape[0]
    pos = lax.broadcasted_iota(I32, x.shape, 0) & (GLA_CHUNK - 1)
    s = 1
    while s < GLA_CHUNK:
        if reverse:
            x = x + jnp.where(pos < GLA_CHUNK - s, pltpu.roll(x, rows - s, 0), 0.0)
        else:
            x = x + jnp.where(pos >= s, pltpu.roll(x, s, 0), 0.0)
        s *= 2
    return x


def _in_proj_kernel(x_ref, wa_ref, wl_ref, w2_ref, bga_ref, wb_ref, cos_ref, sina_ref, sinb_ref,
                    pa_ref, bd_ref, pb_ref):
    xb = x_ref[...].astype(BF16)
    lrb = _dot(xb, wl_ref[...]).astype(BF16)

    def decay_slab(j):
        cols = slice(j * LANES, (j + 1) * LANES)
        z = _dot(lrb, w2_ref[:, cols]) + bga_ref[:, cols]
        la = (jnp.minimum(z, 0.0) - jnp.log(1.0 + jnp.exp(-jnp.abs(z)))) * (1.0 / GLA_NORMALIZER)
        bd_ref[:, cols] = _chunk_cumsum(la, reverse=j * LANES >= GLA_KEY)

    n_slabs = 2 * GLA_KEY // LANES
    slab = 0
    step = 512
    for c0 in range(0, PA_W, step):
        r = _dot(xb, wa_ref[:, c0:c0 + step])
        if c0 < GLA_KEY:
            r = r * (GLA_DK ** -0.5)
        pa_ref[:, c0:c0 + step] = r.astype(BF16)
        if slab < n_slabs:
            decay_slab(slab)
            slab += 1
    cos = cos_ref[...]
    sina = sina_ref[...]
    sinb = sinb_ref[...]
    rope0 = 2 * D_MODEL
    rope_end = rope0 + (SWA_HEADS + SWA_KV) * SWA_HD
    for c0 in range(0, PB_W, step):
        r = _dot(xb, wb_ref[:, c0:c0 + step])
        if c0 + step > rope0 and c0 < rope_end:
            parts = []
            for j in range(step // SWA_HD):
                t = r[:, j * SWA_HD:(j + 1) * SWA_HD]
                col = c0 + j * SWA_HD
                if rope0 <= col < rope_end:
                    t = (t * cos + pltpu.roll(t, SWA_HD - ROPE_DIM // 2, 1) * sina
                         + pltpu.roll(t, ROPE_DIM // 2, 1) * sinb)
                if rope0 <= col < rope0 + SWA_HEADS * SWA_HD:
                    t = t * (SWA_HD ** -0.5)
                parts.append(t)
            r = jnp.concatenate(parts, axis=1)
        pb_ref[:, c0:c0 + step] = r.astype(BF16)
        if slab < n_slabs:
            decay_slab(slab)
            slab += 1
    assert slab == n_slabs


def _in_proj(x2d, wa, wl, w2f, bgaf, wb, cos, sina, sinb, seq):
    t = x2d.shape[0]
    tm = PROJ_TM
    nseq = seq // tm
    const = lambda i: (0, 0)
    return pl.pallas_call(
        _in_proj_kernel,
        grid=(t // tm,),
        in_specs=[
            pl.BlockSpec((tm, D_MODEL), lambda i: (i, 0)),
            pl.BlockSpec((D_MODEL, PA_W), const),
            pl.BlockSpec((D_MODEL, LANES), const),
            pl.BlockSpec((LANES, 2 * GLA_KEY), const),
            pl.BlockSpec((1, 2 * GLA_KEY), const),
            pl.BlockSpec((D_MODEL, PB_W), const),
            pl.BlockSpec((tm, LANES), lambda i: (i % nseq, 0)),
            pl.BlockSpec((tm, LANES), lambda i: (i % nseq, 0)),
            pl.BlockSpec((tm, LANES), lambda i: (i % nseq, 0)),
        ],
        out_specs=[
            pl.BlockSpec((tm, PA_W), lambda i: (i, 0)),
            pl.BlockSpec((tm, 2 * GLA_KEY), lambda i: (i, 0)),
            pl.BlockSpec((tm, PB_W), lambda i: (i, 0)),
        ],
        out_shape=[
            jax.ShapeDtypeStruct((t, PA_W), BF16),
            jax.ShapeDtypeStruct((t, 2 * GLA_KEY), F32),
            jax.ShapeDtypeStruct((t, PB_W), BF16),
        ],
        compiler_params=_cparams(("parallel",)),
        name="in_proj",
    )(x2d, wa, wl, w2f, bgaf, wb, cos, sina, sinb)


def _gla_kernel(q_ref, k_ref, v_ref, r_ref, b_ref, nw_ref, o_ref,
                state_ref, ofwd_ref, *group_scratch, ts, nt):
    ph = pl.program_id(2)
    t = pl.program_id(3)
    c = GLA_CHUNK
    gs = min(GLA_GROUP, ts)
    n_groups = ts // gs
    nc = gs // c

    @pl.when(t == 0)
    def _():
        state_ref[...] = jnp.zeros_like(state_ref)

    row = lax.broadcasted_iota(I32, (c, c), 0)
    col = lax.broadcasted_iota(I32, (c, c), 1)

    def run(forward):
        keep = (col <= row) if forward else (col > row)
        mid = c // 2 if forward else c // 2 - 1
        end = c - 1 if forward else 0
        off = pl.multiple_of((t if forward else nt - 1 - t) * ts, ts)
        group_order = range(n_groups) if forward else reversed(range(n_groups))
        chunk_order = list(range(nc)) if forward else list(reversed(range(nc)))
        st = state_ref[...]
        for g in group_order:
            sall_ref, ds_ref, dec_ref = group_scratch[3 * g:3 * g + 3]
            rows = slice(g * gs, (g + 1) * gs)
            q3 = q_ref[0, rows, :].reshape(nc, c, GLA_DK)
            k3 = k_ref[0, rows, :].reshape(nc, c, GLA_DK)
            v3 = v_ref[0, rows, :].reshape(nc, c, GLA_DV)
            b3 = b_ref[0, rows, :].reshape(nc, c, GLA_DK)
            b_mid = b3[:, mid:mid + 1]
            b_end = b3[:, end:end + 1]
            q_in = q3 * jnp.exp(b3 - b_mid).astype(BF16)
            k_in = k3 * jnp.exp(b_mid - b3).astype(BF16)
            k_st = k_in * jnp.exp(b_end - b_mid).astype(BF16)
            q_b = q_in * jnp.exp(b_mid).astype(BF16)
            dec_ref[...] = jnp.exp(b_end)
            scores = jnp.einsum("cid,cjd->cij", q_in, k_in, preferred_element_type=F32)
            scores = jnp.where(keep[None], scores, 0.0).astype(BF16)
            o_intra = jnp.einsum("cij,cjv->civ", scores, v3, preferred_element_type=F32)
            ds_ref[...] = jnp.einsum("cjv,cjd->cvd", v3, k_st, preferred_element_type=F32)
            for ci in chunk_order:
                sall_ref[ci] = st.astype(BF16)
                st = st * dec_ref[ci] + ds_ref[ci]
            o_inter = jnp.einsum("cid,cvd->civ", q_b, sall_ref[...], preferred_element_type=F32)
            o = (o_intra + o_inter).reshape(gs, GLA_DV)
            orows = pl.ds(pl.multiple_of(off + g * gs, gs), gs)
            if forward:
                ofwd_ref[orows, :] = o
            else:
                tot = o + ofwd_ref[orows, :]
                tot = tot * lax.rsqrt(jnp.mean(tot * tot, axis=-1, keepdims=True) + RMS_EPS)
                r = r_ref[0, rows, :]
                o_ref[0, rows, :] = (
                    tot * nw_ref[...] * (r * _sigmoid(r)).astype(F32)).astype(BF16)
        state_ref[...] = st

    @pl.when(ph == 0)
    def _():
        run(True)

    @pl.when(ph == 1)
    def _():
        run(False)


def _gla(pa, bdec, nw, bsz, seq):
    ts = min(GLA_TS, seq)
    nt = seq // ts
    gs = min(GLA_GROUP, ts)
    nc = gs // GLA_CHUNK
    group_scratch = []
    for _ in range(ts // gs):
        group_scratch += [pltpu.VMEM((nc, GLA_DV, GLA_DK), BF16),
                          pltpu.VMEM((nc, GLA_DV, GLA_DK), F32),
                          pltpu.VMEM((nc, 1, GLA_DK), F32)]
    pa3 = pa.reshape(bsz, seq, PA_W)
    bd3 = bdec.reshape(bsz, seq, 2 * GLA_KEY)
    tile = lambda p, t: t + p * (nt - 1 - 2 * t)
    wk, wv = GLA_DK, GLA_DV
    kq = GLA_KEY // wk
    kv = 2 * GLA_KEY // wv
    kr = kv + GLA_VAL // wv
    kern = functools.partial(_gla_kernel, ts=ts, nt=nt)
    return pl.pallas_call(
        kern,
        grid=(bsz, GLA_HEADS, 2, nt),
        in_specs=[
            pl.BlockSpec((1, ts, wk), lambda b, h, p, t: (b, tile(p, t), h)),
            pl.BlockSpec((1, ts, wk), lambda b, h, p, t: (b, tile(p, t), kq + h)),
            pl.BlockSpec((1, ts, wv), lambda b, h, p, t: (b, tile(p, t), kv + h)),
            pl.BlockSpec((1, ts, wv), lambda b, h, p, t: (b, nt - 1 - p * t, kr + h)),
            pl.BlockSpec((1, ts, wk), lambda b, h, p, t: (b, tile(p, t), p * kq + h)),
            pl.BlockSpec((1, wv), lambda b, h, p, t: (0, h)),
        ],
        out_specs=pl.BlockSpec((1, ts, wv), lambda b, h, p, t: (b, nt - 1 - p * t, h)),
        out_shape=jax.ShapeDtypeStruct((bsz, seq, GLA_VAL), BF16),
        scratch_shapes=[
            pltpu.VMEM((GLA_DV, GLA_DK), F32),
            pltpu.VMEM((seq, GLA_DV), F32),
        ] + group_scratch,
        compiler_params=_cparams(("parallel", "parallel", "arbitrary", "arbitrary")),
        name="gla",
    )(pa3, pa3, pa3, pa3, bd3, nw)


def _swa_kernel(sink_ref, q_ref, kl_ref, kc_ref, kr_ref, vl_ref, vc_ref, vr_ref, o_ref, *, tq):
    kvh = pl.program_id(1)
    n = pl.program_id(2)
    blk = SWA_WINDOW
    k_all = jnp.concatenate([kl_ref[0], kc_ref[0], kr_ref[0]], axis=0)
    v_all = jnp.concatenate([vl_ref[0], vc_ref[0], vr_ref[0]], axis=0)
    rows = SWA_GROUP * blk
    qi = lax.broadcasted_iota(I32, (rows, 3 * blk), 0) & (blk - 1)
    kj = lax.broadcasted_iota(I32, (rows, 3 * blk), 1) - blk
    band = jnp.where(jnp.abs(kj - qi) <= SWA_WINDOW, 0.0, -jnp.inf)
    head = lax.broadcasted_iota(I32, (rows, 1), 0) // blk
    sink = jnp.zeros((rows, 1), F32)
    for g in range(SWA_GROUP):
        sink = jnp.where(head == g, sink_ref[kvh * SWA_GROUP + g], sink)
    n_sub = tq // blk
    before = jnp.where(n == 0, -jnp.inf, 0.0)
    after = jnp.where(n == pl.num_programs(2) - 1, -jnp.inf, 0.0)
    for jq in range(n_sub):
        bias = band
        if jq == 0:
            bias = bias + jnp.where(kj < 0, before, 0.0)
        if jq == n_sub - 1:
            bias = bias + jnp.where(kj >= blk, after, 0.0)
        qs = q_ref[0, jq * blk:(jq + 1) * blk, :]
        q4 = jnp.concatenate([qs[:, g * SWA_HD:(g + 1) * SWA_HD] for g in range(SWA_GROUP)], axis=0)
        keys = k_all[jq * blk:(jq + 3) * blk]
        vals = v_all[jq * blk:(jq + 3) * blk]
        s = _dot_nt(q4, keys) + bias
        m = jnp.maximum(jnp.max(s, axis=-1, keepdims=True), sink)
        p = jnp.exp(s - m)
        denom = jnp.sum(p, axis=-1, keepdims=True) + jnp.exp(sink - m)
        o = _dot(p.astype(BF16), vals) * (1.0 / denom)
        for g in range(SWA_GROUP):
            o_ref[0, jq * blk:(jq + 1) * blk, g * SWA_HD:(g + 1) * SWA_HD] = (
                o[g * blk:(g + 1) * blk].astype(BF16))


def _swa(pb, sinks, bsz, seq):
    tq = min(SWA_TQ, seq)
    blk = SWA_WINDOW
    r = tq // blk
    nb = seq // blk
    pb3 = pb.reshape(bsz, seq, PB_W)
    gw = SWA_GROUP * SWA_HD
    cq = 2 * D_MODEL // gw
    ck = (2 * D_MODEL + SWA_HEADS * SWA_HD) // SWA_HD
    cv = ck + SWA_KV
    left = lambda n: jnp.maximum(n * r - 1, 0)
    right = lambda n: jnp.minimum(n * r + r, nb - 1)
    kern = functools.partial(_swa_kernel, tq=tq)
    grid_spec = pltpu.PrefetchScalarGridSpec(
        num_scalar_prefetch=1,
        grid=(bsz, SWA_KV, seq // tq),
        in_specs=[
            pl.BlockSpec((1, tq, gw), lambda b, h, n, s: (b, n, cq + h)),
            pl.BlockSpec((1, blk, SWA_HD), lambda b, h, n, s: (b, left(n), ck + h)),
            pl.BlockSpec((1, tq, SWA_HD), lambda b, h, n, s: (b, n, ck + h)),
            pl.BlockSpec((1, blk, SWA_HD), lambda b, h, n, s: (b, right(n), ck + h)),
            pl.BlockSpec((1, blk, SWA_HD), lambda b, h, n, s: (b, left(n), cv + h)),
            pl.BlockSpec((1, tq, SWA_HD), lambda b, h, n, s: (b, n, cv + h)),
            pl.BlockSpec((1, blk, SWA_HD), lambda b, h, n, s: (b, right(n), cv + h)),
        ],
        out_specs=pl.BlockSpec((1, tq, gw), lambda b, h, n, s: (b, n, h)),
    )
    return pl.pallas_call(
        kern,
        grid_spec=grid_spec,
        out_shape=jax.ShapeDtypeStruct((bsz, seq, SWA_HEADS * SWA_HD), BF16),
        compiler_params=_cparams(("parallel", "parallel", "parallel")),
        name="swa",
    )(sinks, pb3, pb3, pb3, pb3, pb3, pb3, pb3)


def _layer_norm(y, g, b):
    mu = jnp.mean(y, axis=-1, keepdims=True)
    d = y - mu
    var = jnp.mean(d * d, axis=-1, keepdims=True)
    return d * lax.rsqrt(var + LN_EPS) * g + b


def _merge_kernel(oa_ref, ob_ref, ga_ref, gb_ref, x_ref, wo_ref, g_ref, b_ref, rw_ref, rb_ref,
                  before_ref, xp_ref, gate_ref, route_ref, cnt_ref, run_ref, *, tm):
    @pl.when(pl.program_id(0) == 0)
    def _():
        run_ref[...] = jnp.zeros_like(run_ref)

    gm = MERGE_GROUP
    lane_i = lax.broadcasted_iota(I32, (gm, LANES), 1)
    lane = lane_i.astype(F32)

    def spread(cols):
        out = jnp.zeros((gm, LANES), F32)
        for kk, col in enumerate(cols):
            out = jnp.where(lane_i == kk, col, out)
        return out

    run = run_ref[...]
    for r0 in range(0, tm, gm):
        rows = slice(r0, r0 + gm)
        h = (_sigmoid(ga_ref[rows, :]) * oa_ref[rows, :]
             + _sigmoid(gb_ref[rows, :]) * ob_ref[rows, :])
        y = DEEPNORM_ALPHA * x_ref[rows, :] + _dot(h, wo_ref[...])
        x1 = _layer_norm(y, g_ref[...], b_ref[...])
        xp_ref[rows, :] = _pack_bf16_pair(x1[:, :HALF], x1[:, HALF:])

        x_hi, x_lo = _split2(x1)
        both = _dot(x_hi, rw_ref[...])
        logits = (both[:, :LANES] + both[:, LANES:] + _dot(x_lo, rw_ref[:, :LANES])
                  + rb_ref[...])
        work = logits
        vals, idxs, hots = [], [], []
        for _ in range(TOP_K):
            m = jnp.max(work, axis=-1, keepdims=True)
            sel = jnp.min(jnp.where(work == m, lane, float(LANES)), axis=-1, keepdims=True)
            hot = lane == sel
            vals.append(m)
            idxs.append(sel)
            hots.append(hot)
            work = jnp.where(hot, -jnp.inf, work)
        e = [jnp.exp(v - vals[0]) for v in vals]
        tot = e[0] + e[1] + e[2] + e[3]
        gate_ref[rows, :] = spread([ei / tot for ei in e])

        member = (hots[0] | hots[1] | hots[2] | hots[3])
        cum = _dot(before_ref[...], member.astype(BF16)) + run
        ranks = [jnp.sum(jnp.where(hot, cum, 0.0), axis=-1, keepdims=True) for hot in hots]
        route = spread(idxs + ranks).T
        route_ref[:, rows] = route[:2 * TOP_K].astype(I32)
        run = run + jnp.sum(member.astype(F32), axis=0, keepdims=True)
    run_ref[...] = run
    cnt_ref[...] = run


def _merge(oa, ob, pb, x2d, wo, g, b, rw, rb):
    t = x2d.shape[0]
    tm = PROJ_TM
    const = lambda i: (0, 0)
    kern = functools.partial(_merge_kernel, tm=tm)
    gm = MERGE_GROUP
    before = (jnp.arange(gm)[None, :] < jnp.arange(gm)[:, None]).astype(BF16)
    return pl.pallas_call(
        kern,
        grid=(t // tm,),
        in_specs=[
            pl.BlockSpec((tm, D_MODEL), lambda i: (i, 0)),
            pl.BlockSpec((tm, D_MODEL), lambda i: (i, 0)),
            pl.BlockSpec((tm, D_MODEL), lambda i: (i, 0)),
            pl.BlockSpec((tm, D_MODEL), lambda i: (i, 1)),
            pl.BlockSpec((tm, D_MODEL), lambda i: (i, 0)),
            pl.BlockSpec((D_MODEL, D_MODEL), const),
            pl.BlockSpec((1, D_MODEL), const),
            pl.BlockSpec((1, D_MODEL), const),
            pl.BlockSpec((D_MODEL, 2 * LANES), const),
            pl.BlockSpec((1, LANES), const),
            pl.BlockSpec((gm, gm), const),
        ],
        out_specs=[
            pl.BlockSpec((tm, HALF), lambda i: (i, 0)),
            pl.BlockSpec((tm, LANES), lambda i: (i, 0)),
            pl.BlockSpec((2 * TOP_K, tm), lambda i: (0, i)),
            pl.BlockSpec((1, LANES), const),
        ],
        out_shape=[
            jax.ShapeDtypeStruct((t, HALF), U32),
            jax.ShapeDtypeStruct((t, LANES), F32),
            jax.ShapeDtypeStruct((2 * TOP_K, t), I32),
            jax.ShapeDtypeStruct((1, LANES), F32),
        ],
        scratch_shapes=[pltpu.VMEM((1, LANES), F32)],
        compiler_params=_cparams(("arbitrary",)),
        name="merge",
    )(oa, ob, pb, pb, x2d, wo, g, b, rw, rb, before)


def _gather_rows(table, idx):
    n = idx.shape[0]
    d = table.shape[1]
    info = plsc.get_sparse_core_info()
    nw = info.num_cores * info.num_subcores
    per_w = n // nw
    assert per_w * nw == n and per_w % SC_CHUNK == 0
    n_chunks = per_w // SC_CHUNK
    mesh = plsc.VectorSubcoreMesh(core_axis_name="c", subcore_axis_name="s")

    @functools.partial(
        pl.kernel, mesh=mesh,
        out_type=jax.ShapeDtypeStruct((n, d), table.dtype),
        scratch_types=[
            pltpu.VMEM((SC_CHUNK,), I32),
            pltpu.VMEM((SC_CHUNK, d), table.dtype),
            pltpu.SemaphoreType.DMA,
        ],
    )
    def k(table_hbm, idx_hbm, out_hbm, idx_v, rows_v, sem):
        wid = lax.axis_index("s") * info.num_cores + lax.axis_index("c")
        base = wid * per_w

        @pl.loop(0, n_chunks)
        def _(j):
            off = pl.multiple_of(base + j * SC_CHUNK, 8)
            pltpu.sync_copy(idx_hbm.at[pl.ds(off, SC_CHUNK)], idx_v)
            pltpu.async_copy(table_hbm.at[idx_v], rows_v, sem).wait()
            pltpu.sync_copy(rows_v, out_hbm.at[pl.ds(off, SC_CHUNK)])

    return k(table, idx)


def _scatter_rows(src, dest_km, n_rows):
    t, d = src.shape
    info = plsc.get_sparse_core_info()
    nw = info.num_cores * info.num_subcores
    per_w = t // nw
    assert per_w * nw == t and per_w % SC_CHUNK == 0
    n_chunks = per_w // SC_CHUNK
    mesh = plsc.VectorSubcoreMesh(core_axis_name="c", subcore_axis_name="s")

    @functools.partial(
        pl.kernel, mesh=mesh,
        out_type=jax.ShapeDtypeStruct((n_rows, d), src.dtype),
        scratch_types=[
            pltpu.VMEM((SC_CHUNK,), I32),
            pltpu.VMEM((SC_CHUNK, d), src.dtype),
            pltpu.SemaphoreType.DMA,
        ],
    )
    def k(src_hbm, dest_hbm, out_hbm, idx_v, rows_v, sem):
        wid = lax.axis_index("s") * info.num_cores + lax.axis_index("c")
        base = wid * per_w

        @pl.loop(0, n_chunks)
        def _(j):
            off = pl.multiple_of(base + j * SC_CHUNK, 8)
            pltpu.sync_copy(src_hbm.at[pl.ds(off, SC_CHUNK)], rows_v)
            for kk in range(TOP_K):
                pltpu.sync_copy(dest_hbm.at[pl.ds(pl.multiple_of(kk * t + off, 8), SC_CHUNK)], idx_v)
                pltpu.async_copy(rows_v, out_hbm.at[idx_v], sem).wait()

    return k(src, dest_km)


def _expert_kernel(be_ref, nused_ref, nvalid_ref, xs_ref, wgu_ref, bgu_ref, wd_ref, bd_ref, ys_ref,
                   wgu_bf, wd_bf):
    i = pl.program_id(0)

    @pl.when(i < nused_ref[0])
    def _():
        @pl.when((i == 0) | (be_ref[i] != be_ref[jnp.maximum(i - 1, 0)]))
        def _():
            rows = 256
            for r0 in range(0, D_MODEL, rows):
                wgu_bf[r0:r0 + rows, :] = wgu_ref[0, r0:r0 + rows, :].astype(BF16)
            for r0 in range(0, D_EXPERT, rows):
                wd_bf[r0:r0 + rows, :] = wd_ref[0, r0:r0 + rows, :].astype(BF16)

        groups = [slice(g * MOE_SUB, (g + 1) * MOE_SUB) for g in range(MOE_BM // MOE_SUB)]

        def load(rows):
            rowid = lax.broadcasted_iota(I32, (MOE_SUB, 1), 0) + rows.start
            words = jnp.where(rowid < nvalid_ref[i], xs_ref[rows, :], jnp.uint32(0))
            lo, hi = _unpack_bf16_pair(words)
            return jnp.concatenate([lo.astype(BF16), hi.astype(BF16)], axis=1)

        def proj(x, col):
            return _dot(x, wgu_bf[:, col:col + D_EXPERT]) + bgu_ref[0, :, col:col + D_EXPERT]

        def activate(gate, up):
            gate = jnp.minimum(gate, SWIGLU_LIMIT)
            up = jnp.clip(up, -SWIGLU_LIMIT, SWIGLU_LIMIT)
            return ((up + 1.0) * gate * _sigmoid(gate * SWIGLU_ALPHA)).astype(BF16)

        xs = [load(rows) for rows in groups]
        gates = [proj(x, 0) for x in xs]
        ups = [proj(x, D_EXPERT) for x in xs]
        acts = [activate(g, u) for g, u in zip(gates, ups)]
        for rows, act in zip(groups, acts):
            y = _dot(act, wd_bf[...]) + bd_ref[0]
            ys_ref[rows, :] = _pack_bf16_pair(y[:, :HALF], y[:, HALF:])


def _experts(xs, block_expert, n_used, n_valid, wgu, bgu, wd, bd):
    n_rows = xs.shape[0]
    bm = MOE_BM
    n_blocks = n_rows // bm
    rows = lambda i, be, nu, nv: (jnp.minimum(i, nu[0] - 1), 0)
    expert = lambda i, be, nu, nv: (be[i], 0, 0)
    grid_spec = pltpu.PrefetchScalarGridSpec(
        num_scalar_prefetch=3,
        grid=(n_blocks,),
        in_specs=[
            pl.BlockSpec((bm, HALF), rows),
            pl.BlockSpec((1, D_MODEL, 2 * D_EXPERT), expert),
            pl.BlockSpec((1, 1, 2 * D_EXPERT), expert),
            pl.BlockSpec((1, D_EXPERT, D_MODEL), expert),
            pl.BlockSpec((1, 1, D_MODEL), expert),
        ],
        out_specs=pl.BlockSpec((bm, HALF), rows),
        scratch_shapes=[
            pltpu.VMEM((D_MODEL, 2 * D_EXPERT), BF16),
            pltpu.VMEM((D_EXPERT, D_MODEL), BF16),
        ],
    )
    return pl.pallas_call(
        _expert_kernel,
        grid_spec=grid_spec,
        out_shape=jax.ShapeDtypeStruct((n_rows, HALF), U32),
        compiler_params=_cparams(("arbitrary",)),
        name="experts",
    )(block_expert, n_used, n_valid, xs, wgu, bgu, wd, bd)


def _combine_kernel(xp_ref, yg_ref, gate_ref, g_ref, b_ref, o_ref):
    gates = gate_ref[...]
    f_lo = None
    f_hi = None
    for kk in range(TOP_K):
        lo, hi = _unpack_bf16_pair(yg_ref[kk])
        w = gates[:, kk:kk + 1]
        f_lo = lo * w if f_lo is None else f_lo + lo * w
        f_hi = hi * w if f_hi is None else f_hi + hi * w
    x_lo, x_hi = _unpack_bf16_pair(xp_ref[...])
    y_lo = DEEPNORM_ALPHA * x_lo + f_lo
    y_hi = DEEPNORM_ALPHA * x_hi + f_hi
    mu = (jnp.sum(y_lo, axis=-1, keepdims=True) + jnp.sum(y_hi, axis=-1, keepdims=True)) / D_MODEL
    d_lo = y_lo - mu
    d_hi = y_hi - mu
    var = (jnp.sum(d_lo * d_lo, axis=-1, keepdims=True)
           + jnp.sum(d_hi * d_hi, axis=-1, keepdims=True)) / D_MODEL
    inv = lax.rsqrt(var + LN_EPS)
    g = g_ref[...]
    b = b_ref[...]
    o_ref[:, :HALF] = d_lo * inv * g[:, :HALF] + b[:, :HALF]
    o_ref[:, HALF:] = d_hi * inv * g[:, HALF:] + b[:, HALF:]


def _combine(xp, yg, gates, g, b):
    t = xp.shape[0]
    tm = PROJ_TM
    const = lambda i: (0, 0)
    return pl.pallas_call(
        _combine_kernel,
        grid=(t // tm,),
        in_specs=[
            pl.BlockSpec((tm, HALF), lambda i: (i, 0)),
            pl.BlockSpec((TOP_K, tm, HALF), lambda i: (0, i, 0)),
            pl.BlockSpec((tm, LANES), lambda i: (i, 0)),
            pl.BlockSpec((1, D_MODEL), const),
            pl.BlockSpec((1, D_MODEL), const),
        ],
        out_specs=pl.BlockSpec((tm, D_MODEL), lambda i: (i, 0)),
        out_shape=jax.ShapeDtypeStruct((t, D_MODEL), F32),
        compiler_params=_cparams(("parallel",)),
        name="combine",
    )(xp, yg, gates, g, b)


def _rope_tables(seq):
    half = ROPE_DIM // 2
    pos = jnp.arange(seq, dtype=F32)
    inv_freq = ROPE_THETA ** (-jnp.arange(0, ROPE_DIM, 2, dtype=F32) / ROPE_DIM)
    ang = pos[:, None] * inv_freq[None, :]
    cos, sin = jnp.cos(ang), jnp.sin(ang)
    zeros = jnp.zeros((seq, LANES - ROPE_DIM), F32)
    cos_f = jnp.concatenate([cos, cos, jnp.ones((seq, LANES - ROPE_DIM), F32)], axis=1)
    sin_a = jnp.concatenate([-sin, jnp.zeros((seq, half), F32), zeros], axis=1)
    sin_b = jnp.concatenate([jnp.zeros((seq, half), F32), sin, zeros], axis=1)
    return cos_f, sin_a, sin_b


def _prep_weights(w_in, gla_gate_w2, gla_gate_b, gla_norm_w, w_o, ln1_g, ln1_b, router_w, router_b,
                  w_gate_up, b_gate_up, w_down, b_down, ln2_g, ln2_b):
    sizes = (GLA_KEY, GLA_KEY, GLA_VAL, GLA_VAL, 2 * GLA_RANK, SWA_HEADS * SWA_HD,
             SWA_KV * SWA_HD, SWA_KV * SWA_HD, D_MODEL, D_MODEL)
    cols, acc = [], 0
    for s in sizes:
        cols.append(w_in[:, acc:acc + s])
        acc += s
    gq, gk, gv, gr, glr, sq, sk, sv, ga, gb = cols
    wa = jnp.concatenate([gq, gk, gv, gr], axis=1).astype(BF16)
    wl = jnp.pad(glr, ((0, 0), (0, LANES - 2 * GLA_RANK))).astype(BF16)
    wb = jnp.concatenate([ga, gb, sq, sk, sv], axis=1).astype(BF16)
    w2f = jnp.zeros((LANES, 2 * GLA_KEY), F32)
    w2f = w2f.at[:GLA_RANK, :GLA_KEY].set(gla_gate_w2[0])
    w2f = w2f.at[GLA_RANK:2 * GLA_RANK, GLA_KEY:].set(gla_gate_w2[1]).astype(BF16)
    rw_hi, rw_lo = _split2(jnp.pad(router_w, ((0, 0), (0, LANES - N_EXPERTS))))
    rw = jnp.concatenate([rw_hi, rw_lo], axis=1)
    rb = jnp.concatenate([router_b, jnp.full((LANES - N_EXPERTS,), -jnp.inf, F32)]).reshape(1, LANES)
    return dict(
        wa=wa, wl=wl, wb=wb, w2f=w2f, bgaf=gla_gate_b.reshape(1, 2 * GLA_KEY),
        nw=gla_norm_w.reshape(1, GLA_VAL), wo=w_o.astype(BF16),
        ln1_g=ln1_g.reshape(1, D_MODEL), ln1_b=ln1_b.reshape(1, D_MODEL), rw=rw, rb=rb,
        wgu=w_gate_up, bgu=b_gate_up.reshape(N_EXPERTS, 1, 2 * D_EXPERT),
        wd=w_down, bd=b_down.reshape(N_EXPERTS, 1, D_MODEL),
        ln2_g=ln2_g.reshape(1, D_MODEL), ln2_b=ln2_b.reshape(1, D_MODEL))


def _routing_plan(route, counts):
    idx, rank = route[:TOP_K], route[TOP_K:]
    t = route.shape[1]
    bm = MOE_BM
    n_rows = t * TOP_K + N_EXPERTS * bm
    n_blocks = n_rows // bm
    counts = counts[0, :N_EXPERTS].astype(I32)
    padded = (counts + bm - 1) // bm * bm
    pad_ends = jnp.cumsum(padded)
    pad_starts = pad_ends - padded
    dest = rank
    for ex in range(N_EXPERTS):
        dest = dest + jnp.where(idx == ex, pad_starts[ex], 0)
    block_start = jnp.arange(n_blocks, dtype=I32) * bm
    block_expert = jnp.minimum(
        jnp.sum(block_start[:, None] >= pad_ends[None, :], axis=1), N_EXPERTS - 1).astype(I32)
    mine = block_expert[:, None] == jnp.arange(N_EXPERTS, dtype=I32)[None, :]
    seg_end = jnp.sum(jnp.where(mine, (pad_starts + counts)[None, :], 0), axis=1)
    n_valid = jnp.clip(seg_end - block_start, 0, bm).astype(I32)
    n_used = (pad_ends[-1] // bm).astype(I32).reshape(1)
    return dest.reshape(-1), n_rows, block_expert, n_used, n_valid


def _trunk(x, w, sinks, gather_rows, scatter_rows):
    bsz, seq, _ = x.shape
    t = bsz * seq
    x2d = x.reshape(t, D_MODEL)
    cos_f, sin_a, sin_b = _rope_tables(seq)
    pa, bdec, pb = _in_proj(x2d, w["wa"], w["wl"], w["w2f"], w["bgaf"], w["wb"],
                            cos_f, sin_a, sin_b, seq)
    oa = _gla(pa, bdec, w["nw"], bsz, seq).reshape(t, GLA_VAL)
    ob = _swa(pb, sinks, bsz, seq).reshape(t, SWA_HEADS * SWA_HD)
    xp, gates, route, counts = _merge(
        oa, ob, pb, x2d, w["wo"], w["ln1_g"], w["ln1_b"], w["rw"], w["rb"])
    dest_km, n_rows, block_expert, n_used, n_valid = _routing_plan(route, counts)
    xs = scatter_rows(xp, dest_km, n_rows)
    ys = _experts(xs, block_expert, n_used, n_valid, w["wgu"], w["bgu"], w["wd"], w["bd"])
    yg = gather_rows(ys, dest_km).reshape(TOP_K, t, HALF)
    out = _combine(xp, yg, gates, w["ln2_g"], w["ln2_b"])
    return out.reshape(bsz, seq, D_MODEL)


def _forward(x_prompt, x_sample, w_in, gla_gate_w2, gla_gate_b, gla_norm_w, swa_sinks, w_o, ln1_g,
             ln1_b, router_w, router_b, w_gate_up, b_gate_up, w_down, b_down, ln2_g, ln2_b):
    w = _prep_weights(w_in[0], gla_gate_w2[0], gla_gate_b[0], gla_norm_w[0], w_o[0], ln1_g[0],
                      ln1_b[0], router_w[0], router_b[0], w_gate_up[0], b_gate_up[0], w_down[0],
                      b_down[0], ln2_g[0], ln2_b[0])
    sinks = swa_sinks[0].astype(F32)
    return (_trunk(x_prompt, w, sinks, _gather_rows, _scatter_rows),
            _trunk(x_sample, w, sinks, _gather_rows, _scatter_rows))


def kernel(x_prompt, x_sample, w_in, gla_gate_w2, gla_gate_b, gla_norm_w, swa_sinks, w_o, ln1_g, ln1_b, router_w, router_b, w_gate_up, b_gate_up, w_down, b_down, ln2_g, ln2_b):
    return _forward(x_prompt, x_sample, w_in, gla_gate_w2, gla_gate_b, gla_norm_w, swa_sinks, w_o,
                    ln1_g, ln1_b, router_w, router_b, w_gate_up, b_gate_up, w_down, b_down,
                    ln2_g, ln2_b)
```

```python
import functools

import jax
import jax.numpy as jnp
from jax import lax
from jax.experimental import pallas as pl
from jax.experimental.pallas import tpu as pltpu
from jax.experimental.pallas import tpu_sc as plsc

F32 = jnp.float32
BF16 = jnp.bfloat16
I32 = jnp.int32
U32 = jnp.uint32

D_MODEL = 1024
GLA_HEADS = 4
GLA_DK = 128
GLA_DV = 256
GLA_KEY = GLA_HEADS * GLA_DK
GLA_VAL = GLA_HEADS * GLA_DV
GLA_RANK = 16
GLA_NORMALIZER = 16.0
GLA_CHUNK = 64
SWA_HEADS = 8
SWA_KV = 2
SWA_GROUP = SWA_HEADS // SWA_KV
SWA_HD = 128
SWA_WINDOW = 128
ROPE_THETA = 500000.0
ROPE_DIM = SWA_HD // 4
N_EXPERTS = 32
TOP_K = 4
D_EXPERT = D_MODEL
SWIGLU_LIMIT = 7.0
SWIGLU_ALPHA = 1.702
DEEPNORM_ALPHA = 2.0 ** 0.25
LN_EPS = 1e-5
RMS_EPS = 1e-6

LANES = 128
PA_W = 2 * GLA_KEY + 2 * GLA_VAL
PB_W = 2 * D_MODEL + SWA_HEADS * SWA_HD + 2 * SWA_KV * SWA_HD
HALF = D_MODEL // 2

PROJ_TM = 512
MERGE_TM = 1024
MERGE_GROUP = 512
GLA_TS = 4096
GLA_GROUP = 512
SWA_TQ = 2048
MOE_BM = 1024
MOE_SUB = 1024
SC_CHUNK = 64
VMEM_LIMIT = 56 * 1024 * 1024


def _cparams(sem):
    return pltpu.CompilerParams(dimension_semantics=sem, vmem_limit_bytes=VMEM_LIMIT)


def _split2(a):
    hi = a.astype(BF16)
    lo = (a - hi.astype(F32)).astype(BF16)
    return hi, lo


def _dot(a, b):
    return jnp.dot(a, b, preferred_element_type=F32)


def _dot_nt(a, b):
    return lax.dot_general(a, b, (((1,), (1,)), ((), ())), preferred_element_type=F32)


def _sigmoid(x):
    return 0.5 * jnp.tanh(0.5 * x) + 0.5


def _pack_bf16_pair(lo_f32, hi_f32):
    lo = lax.bitcast_convert_type(lo_f32.astype(BF16).astype(F32), U32)
    hi = lax.bitcast_convert_type(hi_f32.astype(BF16).astype(F32), U32)
    return (hi & jnp.uint32(0xFFFF0000)) | (lo >> 16)


def _unpack_bf16_pair(w):
    lo = lax.bitcast_convert_type(w << 16, F32)
    hi = lax.bitcast_convert_type(w & jnp.uint32(0xFFFF0000), F32)
    return lo, hi


def _chunk_cumsum(x, reverse):
    rows = x.shape[0]
    pos = lax.broadcasted_iota(I32, x.shape, 0) & (GLA_CHUNK - 1)
    s = 1
    while s < GLA_CHUNK:
        if reverse:
            x = x + jnp.where(pos < GLA_CHUNK - s, pltpu.roll(x, rows - s, 0), 0.0)
        else:
            x = x + jnp.where(pos >= s, pltpu.roll(x, s, 0), 0.0)
        s *= 2
    return x


def _in_proj_kernel(x_ref, wa_ref, wl_ref, w2_ref, bga_ref, wb_ref, cos_ref, sina_ref, sinb_ref,
                    pa_ref, bd_ref, pb_ref):
    xb = x_ref[...].astype(BF16)
    lrb = _dot(xb, wl_ref[...]).astype(BF16)

    tm = x_ref.shape[0]
    piece_rows = 256

    def decay_piece(j, r0):
        cols = slice(j * LANES, (j + 1) * LANES)
        rows = slice(r0, r0 + piece_rows)
        z = _dot(lrb[rows], w2_ref[:, cols]) + bga_ref[:, cols]
        la = (jnp.minimum(z, 0.0) - jnp.log(1.0 + jnp.exp(-jnp.abs(z)))) * (1.0 / GLA_NORMALIZER)
        bd_ref[rows, cols] = _chunk_cumsum(la, reverse=j * LANES >= GLA_KEY)

    pieces = [(j, r0) for j in range(2 * GLA_KEY // LANES) for r0 in range(0, tm, piece_rows)]
    step = 256
    for c0 in range(0, PA_W, step):
        r = _dot(xb, wa_ref[:, c0:c0 + step])
        if c0 < GLA_KEY:
            r = r * (GLA_DK ** -0.5)
        pa_ref[:, c0:c0 + step] = r.astype(BF16)
        if pieces:
            decay_piece(*pieces.pop(0))
    cos = cos_ref[...]
    sina = sina_ref[...]
    sinb = sinb_ref[...]
    rope0 = 2 * D_MODEL
    rope_end = rope0 + (SWA_HEADS + SWA_KV) * SWA_HD
    for c0 in range(0, PB_W, step):
        r = _dot(xb, wb_ref[:, c0:c0 + step])
        if c0 + step > rope0 and c0 < rope_end:
            parts = []
            for j in range(step // SWA_HD):
                t = r[:, j * SWA_HD:(j + 1) * SWA_HD]
                col = c0 + j * SWA_HD
                if rope0 <= col < rope_end:
                    t = (t * cos + pltpu.roll(t, SWA_HD - ROPE_DIM // 2, 1) * sina
                         + pltpu.roll(t, ROPE_DIM // 2, 1) * sinb)
                if rope0 <= col < rope0 + SWA_HEADS * SWA_HD:
                    t = t * (SWA_HD ** -0.5)
                parts.append(t)
            r = jnp.concatenate(parts, axis=1)
        pb_ref[:, c0:c0 + step] = r.astype(BF16)
        if pieces:
            decay_piece(*pieces.pop(0))
    assert not pieces


def _in_proj(x2d, wa, wl, w2f, bgaf, wb, cos, sina, sinb, seq):
    t = x2d.shape[0]
    tm = PROJ_TM
    nseq = seq // tm
    const = lambda i: (0, 0)
    return pl.pallas_call(
        _in_proj_kernel,
        grid=(t // tm,),
        in_specs=[
            pl.BlockSpec((tm, D_MODEL), lambda i: (i, 0)),
            pl.BlockSpec((D_MODEL, PA_W), const),
            pl.BlockSpec((D_MODEL, LANES), const),
            pl.BlockSpec((LANES, 2 * GLA_KEY), const),
            pl.BlockSpec((1, 2 * GLA_KEY), const),
            pl.BlockSpec((D_MODEL, PB_W), const),
            pl.BlockSpec((tm, LANES), lambda i: (i % nseq, 0)),
            pl.BlockSpec((tm, LANES), lambda i: (i % nseq, 0)),
            pl.BlockSpec((tm, LANES), lambda i: (i % nseq, 0)),
        ],
        out_specs=[
            pl.BlockSpec((tm, PA_W), lambda i: (i, 0)),
            pl.BlockSpec((tm, 2 * GLA_KEY), lambda i: (i, 0)),
            pl.BlockSpec((tm, PB_W), lambda i: (i, 0)),
        ],
        out_shape=[
            jax.ShapeDtypeStruct((t, PA_W), BF16),
            jax.ShapeDtypeStruct((t, 2 * GLA_KEY), F32),
            jax.ShapeDtypeStruct((t, PB_W), BF16),
        ],
        compiler_params=_cparams(("parallel",)),
        name="in_proj",
    )(x2d, wa, wl, w2f, bgaf, wb, cos, sina, sinb)


def _gla_kernel(q_ref, k_ref, v_ref, r_ref, b_ref, nw_ref, o_ref,
                state_ref, ofwd_ref, *group_scratch, ts, nt):
    ph = pl.program_id(2)
    t = pl.program_id(3)
    c = GLA_CHUNK
    gs = min(GLA_GROUP, ts)
    n_groups = ts // gs
    nc = gs // c

    @pl.when(t == 0)
    def _():
        state_ref[...] = jnp.zeros_like(state_ref)

    row = lax.broadcasted_iota(I32, (c, c), 0)
    col = lax.broadcasted_iota(I32, (c, c), 1)

    def run(forward):
        keep = (col <= row) if forward else (col > row)
        mid = c // 2 if forward else c // 2 - 1
        end = c - 1 if forward else 0
        off = pl.multiple_of((t if forward else nt - 1 - t) * ts, ts)
        group_order = range(n_groups) if forward else reversed(range(n_groups))
        chunk_order = list(range(nc)) if forward else list(reversed(range(nc)))
        st = state_ref[...]
        for g in group_order:
            sall_ref, ds_ref, dec_ref = group_scratch[3 * g:3 * g + 3]
            rows = slice(g * gs, (g + 1) * gs)
            q3 = q_ref[0, rows, :].reshape(nc, c, GLA_DK)
            k3 = k_ref[0, rows, :].reshape(nc, c, GLA_DK)
            v3 = v_ref[0, rows, :].reshape(nc, c, GLA_DV)
            b3 = b_ref[0, rows, :].reshape(nc, c, GLA_DK)
            b_mid = b3[:, mid:mid + 1]
            b_end = b3[:, end:end + 1]
            q_in = q3 * jnp.exp(b3 - b_mid).astype(BF16)
            k_in = k3 * jnp.exp(b_mid - b3).astype(BF16)
            k_st = k_in * jnp.exp(b_end - b_mid).astype(BF16)
            q_b = q_in * jnp.exp(b_mid).astype(BF16)
            dec_ref[...] = jnp.exp(b_end)
            scores = jnp.einsum("cid,cjd->cij", q_in, k_in, preferred_element_type=F32)
            scores = jnp.where(keep[None], scores, 0.0).astype(BF16)
            o_intra = jnp.einsum("cij,cjv->civ", scores, v3, preferred_element_type=F32)
            ds_ref[...] = jnp.einsum("cjv,cjd->cvd", v3, k_st, preferred_element_type=F32)
            for ci in chunk_order:
                sall_ref[ci] = st.astype(BF16)
                st = st * dec_ref[ci] + ds_ref[ci]
            o_inter = jnp.einsum("cid,cvd->civ", q_b, sall_ref[...], preferred_element_type=F32)
            o = (o_intra + o_inter).reshape(gs, GLA_DV)
            orows = pl.ds(pl.multiple_of(off + g * gs, gs), gs)
            if forward:
                ofwd_ref[orows, :] = o
            else:
                tot = o + ofwd_ref[orows, :]
                tot = tot * lax.rsqrt(jnp.mean(tot * tot, axis=-1, keepdims=True) + RMS_EPS)
                r = r_ref[0, rows, :]
                o_ref[0, rows, :] = (
                    tot * nw_ref[...] * (r * _sigmoid(r)).astype(F32)).astype(BF16)
        state_ref[...] = st

    @pl.when(ph == 0)
    def _():
        run(True)

    @pl.when(ph == 1)
    def _():
        run(False)


def _gla(pa, bdec, nw, bsz, seq):
    ts = min(GLA_TS, seq)
    nt = seq // ts
    gs = min(GLA_GROUP, ts)
    nc = gs // GLA_CHUNK
    group_scratch = []
    for _ in range(ts // gs):
        group_scratch += [pltpu.VMEM((nc, GLA_DV, GLA_DK), BF16),
                          pltpu.VMEM((nc, GLA_DV, GLA_DK), F32),
                          pltpu.VMEM((nc, 1, GLA_DK), F32)]
    pa3 = pa.reshape(bsz, seq, PA_W)
    bd3 = bdec.reshape(bsz, seq, 2 * GLA_KEY)
    tile = lambda p, t: t + p * (nt - 1 - 2 * t)
    wk, wv = GLA_DK, GLA_DV
    kq = GLA_KEY // wk
    kv = 2 * GLA_KEY // wv
    kr = kv + GLA_VAL // wv
    kern = functools.partial(_gla_kernel, ts=ts, nt=nt)
    return pl.pallas_call(
        kern,
        grid=(bsz, GLA_HEADS, 2, nt),
        in_specs=[
            pl.BlockSpec((1, ts, wk), lambda b, h, p, t: (b, tile(p, t), h)),
            pl.BlockSpec((1, ts, wk), lambda b, h, p, t: (b, tile(p, t), kq + h)),
            pl.BlockSpec((1, ts, wv), lambda b, h, p, t: (b, tile(p, t), kv + h)),
            pl.BlockSpec((1, ts, wv), lambda b, h, p, t: (b, nt - 1 - p * t, kr + h)),
            pl.BlockSpec((1, ts, wk), lambda b, h, p, t: (b, tile(p, t), p * kq + h)),
            pl.BlockSpec((1, wv), lambda b, h, p, t: (0, h)),
        ],
        out_specs=pl.BlockSpec((1, ts, wv), lambda b, h, p, t: (b, nt - 1 - p * t, h)),
        out_shape=jax.ShapeDtypeStruct((bsz, seq, GLA_VAL), BF16),
        scratch_shapes=[
            pltpu.VMEM((GLA_DV, GLA_DK), F32),
            pltpu.VMEM((seq, GLA_DV), F32),
        ] + group_scratch,
        compiler_params=_cparams(("parallel", "parallel", "arbitrary", "arbitrary")),
        name="gla",
    )(pa3, pa3, pa3, pa3, bd3, nw)


def _swa_kernel(sink_ref, q_ref, kl_ref, kc_ref, kr_ref, vl_ref, vc_ref, vr_ref, o_ref, *, tq):
    kvh = pl.program_id(1)
    n = pl.program_id(2)
    blk = SWA_WINDOW
    k_all = jnp.concatenate([kl_ref[0], kc_ref[0], kr_ref[0]], axis=0)
    v_all = jnp.concatenate([vl_ref[0], vc_ref[0], vr_ref[0]], axis=0)
    rows = SWA_GROUP * blk
    qi = lax.broadcasted_iota(I32, (rows, 3 * blk), 0) & (blk - 1)
    kj = lax.broadcasted_iota(I32, (rows, 3 * blk), 1) - blk
    band = jnp.where(jnp.abs(kj - qi) <= SWA_WINDOW, 0.0, -jnp.inf)
    head = lax.broadcasted_iota(I32, (rows, 1), 0) // blk
    sink = jnp.zeros((rows, 1), F32)
    for g in range(SWA_GROUP):
        sink = jnp.where(head == g, sink_ref[kvh * SWA_GROUP + g], sink)
    n_sub = tq // blk
    before = jnp.where(n == 0, -jnp.inf, 0.0)
    after = jnp.where(n == pl.num_programs(2) - 1, -jnp.inf, 0.0)
    for jq in range(n_sub):
        bias = band
        if jq == 0:
            bias = bias + jnp.where(kj < 0, before, 0.0)
        if jq == n_sub - 1:
            bias = bias + jnp.where(kj >= blk, after, 0.0)
        qs = q_ref[0, jq * blk:(jq + 1) * blk, :]
        q4 = jnp.concatenate([qs[:, g * SWA_HD:(g + 1) * SWA_HD] for g in range(SWA_GROUP)], axis=0)
        keys = k_all[jq * blk:(jq + 3) * blk]
        vals = v_all[jq * blk:(jq + 3) * blk]
        s = _dot_nt(q4, keys) + bias
        m = jnp.maximum(jnp.max(s, axis=-1, keepdims=True), sink)
        p = jnp.exp(s - m)
        denom = jnp.sum(p, axis=-1, keepdims=True) + jnp.exp(sink - m)
        o = _dot(p.astype(BF16), vals) * (1.0 / denom)
        for g in range(SWA_GROUP):
            o_ref[0, jq * blk:(jq + 1) * blk, g * SWA_HD:(g + 1) * SWA_HD] = (
                o[g * blk:(g + 1) * blk].astype(BF16))


def _swa(pb, sinks, bsz, seq):
    tq = min(SWA_TQ, seq)
    blk = SWA_WINDOW
    r = tq // blk
    nb = seq // blk
    pb3 = pb.reshape(bsz, seq, PB_W)
    gw = SWA_GROUP * SWA_HD
    cq = 2 * D_MODEL // gw
    ck = (2 * D_MODEL + SWA_HEADS * SWA_HD) // SWA_HD
    cv = ck + SWA_KV
    left = lambda n: jnp.maximum(n * r - 1, 0)
    right = lambda n: jnp.minimum(n * r + r, nb - 1)
    kern = functools.partial(_swa_kernel, tq=tq)
    grid_spec = pltpu.PrefetchScalarGridSpec(
        num_scalar_prefetch=1,
        grid=(bsz, SWA_KV, seq // tq),
        in_specs=[
            pl.BlockSpec((1, tq, gw), lambda b, h, n, s: (b, n, cq + h)),
            pl.BlockSpec((1, blk, SWA_HD), lambda b, h, n, s: (b, left(n), ck + h)),
            pl.BlockSpec((1, tq, SWA_HD), lambda b, h, n, s: (b, n, ck + h)),
            pl.BlockSpec((1, blk, SWA_HD), lambda b, h, n, s: (b, right(n), ck + h)),
            pl.BlockSpec((1, blk, SWA_HD), lambda b, h, n, s: (b, left(n), cv + h)),
            pl.BlockSpec((1, tq, SWA_HD), lambda b, h, n, s: (b, n, cv + h)),
            pl.BlockSpec((1, blk, SWA_HD), lambda b, h, n, s: (b, right(n), cv + h)),
        ],
        out_specs=pl.BlockSpec((1, tq, gw), lambda b, h, n, s: (b, n, h)),
    )
    return pl.pallas_call(
        kern,
        grid_spec=grid_spec,
        out_shape=jax.ShapeDtypeStruct((bsz, seq, SWA_HEADS * SWA_HD), BF16),
        compiler_params=_cparams(("parallel", "parallel", "parallel")),
        name="swa",
    )(sinks, pb3, pb3, pb3, pb3, pb3, pb3, pb3)


def _layer_norm(y, g, b):
    mu = jnp.mean(y, axis=-1, keepdims=True)
    d = y - mu
    var = jnp.mean(d * d, axis=-1, keepdims=True)
    return d * lax.rsqrt(var + LN_EPS) * g + b


def _merge_kernel(oa_ref, ob_ref, ga_ref, gb_ref, x_ref, wo_ref, g_ref, b_ref, rw_ref, rb_ref,
                  before_ref, xp_ref, gate_ref, route_ref, cnt_ref, run_ref, *, tm):
    @pl.when(pl.program_id(0) == 0)
    def _():
        run_ref[...] = jnp.zeros_like(run_ref)

    gm = MERGE_GROUP
    lane_i = lax.broadcasted_iota(I32, (gm, LANES), 1)
    lane = lane_i.astype(F32)

    def spread(cols):
        out = jnp.zeros((gm, LANES), F32)
        for kk, col in enumerate(cols):
            out = jnp.where(lane_i == kk, col, out)
        return out

    run = run_ref[...]
    for r0 in range(0, tm, gm):
        rows = slice(r0, r0 + gm)
        h = (_sigmoid(ga_ref[rows, :]) * oa_ref[rows, :]
             + _sigmoid(gb_ref[rows, :]) * ob_ref[rows, :])
        y = DEEPNORM_ALPHA * x_ref[rows, :] + _dot(h, wo_ref[...])
        x1 = _layer_norm(y, g_ref[...], b_ref[...])
        xp_ref[rows, :] = _pack_bf16_pair(x1[:, :HALF], x1[:, HALF:])

        x_hi, x_lo = _split2(x1)
        both = _dot(x_hi, rw_ref[...])
        logits = (both[:, :LANES] + both[:, LANES:] + _dot(x_lo, rw_ref[:, :LANES])
                  + rb_ref[...])
        work = logits
        vals, idxs, hots = [], [], []
        for _ in range(TOP_K):
            m = jnp.max(work, axis=-1, keepdims=True)
            sel = jnp.min(jnp.where(work == m, lane, float(LANES)), axis=-1, keepdims=True)
            hot = lane == sel
            vals.append(m)
            idxs.append(sel)
            hots.append(hot)
            work = jnp.where(hot, -jnp.inf, work)
        e = [jnp.exp(v - vals[0]) for v in vals]
        tot = e[0] + e[1] + e[2] + e[3]
        gate_ref[rows, :] = spread([ei / tot for ei in e])

        member = (hots[0] | hots[1] | hots[2] | hots[3])
        cum = _dot(before_ref[...], member.astype(BF16)) + run
        ranks = [jnp.sum(jnp.where(hot, cum, 0.0), axis=-1, keepdims=True) for hot in hots]
        route = spread(idxs + ranks).T
        route_ref[:, rows] = route[:2 * TOP_K].astype(I32)
        run = run + jnp.sum(member.astype(F32), axis=0, keepdims=True)
    run_ref[...] = run
    cnt_ref[...] = run


def _merge(oa, ob, pb, x2d, wo, g, b, rw, rb):
    t = x2d.shape[0]
    tm = MERGE_TM
    const = lambda i: (0, 0)
    kern = functools.partial(_merge_kernel, tm=tm)
    gm = MERGE_GROUP
    before = (jnp.arange(gm)[None, :] < jnp.arange(gm)[:, None]).astype(BF16)
    return pl.pallas_call(
        kern,
        grid=(t // tm,),
        in_specs=[
            pl.BlockSpec((tm, D_MODEL), lambda i: (i, 0)),
            pl.BlockSpec((tm, D_MODEL), lambda i: (i, 0)),
            pl.BlockSpec((tm, D_MODEL), lambda i: (i, 0)),
            pl.BlockSpec((tm, D_MODEL), lambda i: (i, 1)),
            pl.BlockSpec((tm, D_MODEL), lambda i: (i, 0)),
            pl.BlockSpec((D_MODEL, D_MODEL), const),
            pl.BlockSpec((1, D_MODEL), const),
            pl.BlockSpec((1, D_MODEL), const),
            pl.BlockSpec((D_MODEL, 2 * LANES), const),
            pl.BlockSpec((1, LANES), const),
            pl.BlockSpec((gm, gm), const),
        ],
        out_specs=[
            pl.BlockSpec((tm, HALF), lambda i: (i, 0)),
            pl.BlockSpec((tm, LANES), lambda i: (i, 0)),
            pl.BlockSpec((2 * TOP_K, tm), lambda i: (0, i)),
            pl.BlockSpec((1, LANES), const),
        ],
        out_shape=[
            jax.ShapeDtypeStruct((t, HALF), U32),
            jax.ShapeDtypeStruct((t, LANES), F32),
            jax.ShapeDtypeStruct((2 * TOP_K, t), I32),
            jax.ShapeDtypeStruct((1, LANES), F32),
        ],
        scratch_shapes=[pltpu.VMEM((1, LANES), F32)],
        compiler_params=_cparams(("arbitrary",)),
        name="merge",
    )(oa, ob, pb, pb, x2d, wo, g, b, rw, rb, before)


def _gather_rows(table, idx):
    n = idx.shape[0]
    d = table.shape[1]
    info = plsc.get_sparse_core_info()
    nw = info.num_cores * info.num_subcores
    per_w = n // nw
    assert per_w * nw == n and per_w % SC_CHUNK == 0
    n_chunks = per_w // SC_CHUNK
    mesh = plsc.VectorSubcoreMesh(core_axis_name="c", subcore_axis_name="s")

    @functools.partial(
        pl.kernel, mesh=mesh,
        out_type=jax.ShapeDtypeStruct((n, d), table.dtype),
        scratch_types=[
            pltpu.VMEM((SC_CHUNK,), I32),
            pltpu.VMEM((SC_CHUNK, d), table.dtype),
            pltpu.SemaphoreType.DMA,
        ],
    )
    def k(table_hbm, idx_hbm, out_hbm, idx_v, rows_v, sem):
        wid = lax.axis_index("s") * info.num_cores + lax.axis_index("c")
        base = wid * per_w

        @pl.loop(0, n_chunks)
        def _(j):
            off = pl.multiple_of(base + j * SC_CHUNK, 8)
            pltpu.sync_copy(idx_hbm.at[pl.ds(off, SC_CHUNK)], idx_v)
            pltpu.async_copy(table_hbm.at[idx_v], rows_v, sem).wait()
            pltpu.sync_copy(rows_v, out_hbm.at[pl.ds(off, SC_CHUNK)])

    return k(table, idx)


def _scatter_rows(src, dest_km, n_rows):
    t, d = src.shape
    info = plsc.get_sparse_core_info()
    nw = info.num_cores * info.num_subcores
    per_w = t // nw
    assert per_w * nw == t and per_w % SC_CHUNK == 0
    n_chunks = per_w // SC_CHUNK
    mesh = plsc.VectorSubcoreMesh(core_axis_name="c", subcore_axis_name="s")

    @functools.partial(
        pl.kernel, mesh=mesh,
        out_type=jax.ShapeDtypeStruct((n_rows, d), src.dtype),
        scratch_types=[
            pltpu.VMEM((SC_CHUNK,), I32),
            pltpu.VMEM((SC_CHUNK, d), src.dtype),
            pltpu.SemaphoreType.DMA,
        ],
    )
    def k(src_hbm, dest_hbm, out_hbm, idx_v, rows_v, sem):
        wid = lax.axis_index("s") * info.num_cores + lax.axis_index("c")
        base = wid * per_w

        @pl.loop(0, n_chunks)
        def _(j):
            off = pl.multiple_of(base + j * SC_CHUNK, 8)
            pltpu.sync_copy(src_hbm.at[pl.ds(off, SC_CHUNK)], rows_v)
            for kk in range(TOP_K):
                pltpu.sync_copy(dest_hbm.at[pl.ds(pl.multiple_of(kk * t + off, 8), SC_CHUNK)], idx_v)
                pltpu.async_copy(rows_v, out_hbm.at[idx_v], sem).wait()

    return k(src, dest_km)


def _expert_kernel(be_ref, nused_ref, nvalid_ref, xs_ref, wgu_ref, bgu_ref, wd_ref, bd_ref, ys_ref,
                   wgu_bf, wd_bf):
    i = pl.program_id(0)

    @pl.when(i < nused_ref[0])
    def _():
        @pl.when((i == 0) | (be_ref[i] != be_ref[jnp.maximum(i - 1, 0)]))
        def _():
            rows = 256
            for r0 in range(0, D_MODEL, rows):
                wgu_bf[r0:r0 + rows, :] = wgu_ref[0, r0:r0 + rows, :].astype(BF16)
            for r0 in range(0, D_EXPERT, rows):
                wd_bf[r0:r0 + rows, :] = wd_ref[0, r0:r0 + rows, :].astype(BF16)

        groups = [slice(g * MOE_SUB, (g + 1) * MOE_SUB) for g in range(MOE_BM // MOE_SUB)]

        def load(rows):
            rowid = lax.broadcasted_iota(I32, (MOE_SUB, 1), 0) + rows.start
            words = jnp.where(rowid < nvalid_ref[i], xs_ref[rows, :], jnp.uint32(0))
            lo, hi = _unpack_bf16_pair(words)
            return jnp.concatenate([lo.astype(BF16), hi.astype(BF16)], axis=1)

        def proj(x, col):
            return _dot(x, wgu_bf[:, col:col + D_EXPERT]) + bgu_ref[0, :, col:col + D_EXPERT]

        def activate(gate, up):
            gate = jnp.minimum(gate, SWIGLU_LIMIT)
            up = jnp.clip(up, -SWIGLU_LIMIT, SWIGLU_LIMIT)
            return ((up + 1.0) * gate * _sigmoid(gate * SWIGLU_ALPHA)).astype(BF16)

        xs = [load(rows) for rows in groups]
        gates = [proj(x, 0) for x in xs]
        ups = [proj(x, D_EXPERT) for x in xs]
        acts = [activate(g, u) for g, u in zip(gates, ups)]
        for rows, act in zip(groups, acts):
            y = _dot(act, wd_bf[...]) + bd_ref[0]
            ys_ref[rows, :] = _pack_bf16_pair(y[:, :HALF], y[:, HALF:])


def _experts(xs, block_expert, n_used, n_valid, wgu, bgu, wd, bd):
    n_rows = xs.shape[0]
    bm = MOE_BM
    n_blocks = n_rows // bm
    rows = lambda i, be, nu, nv: (jnp.minimum(i, nu[0] - 1), 0)
    expert = lambda i, be, nu, nv: (be[i], 0, 0)
    grid_spec = pltpu.PrefetchScalarGridSpec(
        num_scalar_prefetch=3,
        grid=(n_blocks,),
        in_specs=[
            pl.BlockSpec((bm, HALF), rows),
            pl.BlockSpec((1, D_MODEL, 2 * D_EXPERT), expert),
            pl.BlockSpec((1, 1, 2 * D_EXPERT), expert),
            pl.BlockSpec((1, D_EXPERT, D_MODEL), expert),
            pl.BlockSpec((1, 1, D_MODEL), expert),
        ],
        out_specs=pl.BlockSpec((bm, HALF), rows),
        scratch_shapes=[
            pltpu.VMEM((D_MODEL, 2 * D_EXPERT), BF16),
            pltpu.VMEM((D_EXPERT, D_MODEL), BF16),
        ],
    )
    return pl.pallas_call(
        _expert_kernel,
        grid_spec=grid_spec,
        out_shape=jax.ShapeDtypeStruct((n_rows, HALF), U32),
        compiler_params=_cparams(("arbitrary",)),
        name="experts",
    )(block_expert, n_used, n_valid, xs, wgu, bgu, wd, bd)


def _combine_kernel(xp_ref, yg_ref, gate_ref, g_ref, b_ref, o_ref):
    gates = gate_ref[...]
    f_lo = None
    f_hi = None
    for kk in range(TOP_K):
        lo, hi = _unpack_bf16_pair(yg_ref[kk])
        w = gates[:, kk:kk + 1]
        f_lo = lo * w if f_lo is None else f_lo + lo * w
        f_hi = hi * w if f_hi is None else f_hi + hi * w
    x_lo, x_hi = _unpack_bf16_pair(xp_ref[...])
    y_lo = DEEPNORM_ALPHA * x_lo + f_lo
    y_hi = DEEPNORM_ALPHA * x_hi + f_hi
    mu = (jnp.sum(y_lo, axis=-1, keepdims=True) + jnp.sum(y_hi, axis=-1, keepdims=True)) / D_MODEL
    d_lo = y_lo - mu
    d_hi = y_hi - mu
    var = (jnp.sum(d_lo * d_lo, axis=-1, keepdims=True)
           + jnp.sum(d_hi * d_hi, axis=-1, keepdims=True)) / D_MODEL
    inv = lax.rsqrt(var + LN_EPS)
    g = g_ref[...]
    b = b_ref[...]
    o_ref[:, :HALF] = d_lo * inv * g[:, :HALF] + b[:, :HALF]
    o_ref[:, HALF:] = d_hi * inv * g[:, HALF:] + b[:, HALF:]


def _combine(xp, yg, gates, g, b):
    t = xp.shape[0]
    tm = MERGE_TM
    const = lambda i: (0, 0)
    return pl.pallas_call(
        _combine_kernel,
        grid=(t // tm,),
        in_specs=[
            pl.BlockSpec((tm, HALF), lambda i: (i, 0)),
            pl.BlockSpec((TOP_K, tm, HALF), lambda i: (0, i, 0)),
            pl.BlockSpec((tm, LANES), lambda i: (i, 0)),
            pl.BlockSpec((1, D_MODEL), const),
            pl.BlockSpec((1, D_MODEL), const),
        ],
        out_specs=pl.BlockSpec((tm, D_MODEL), lambda i: (i, 0)),
        out_shape=jax.ShapeDtypeStruct((t, D_MODEL), F32),
        compiler_params=_cparams(("parallel",)),
        name="combine",
    )(xp, yg, gates, g, b)


def _rope_tables(seq):
    half = ROPE_DIM // 2
    pos = jnp.arange(seq, dtype=F32)
    inv_freq = ROPE_THETA ** (-jnp.arange(0, ROPE_DIM, 2, dtype=F32) / ROPE_DIM)
    ang = pos[:, None] * inv_freq[None, :]
    cos, sin = jnp.cos(ang), jnp.sin(ang)
    zeros = jnp.zeros((seq, LANES - ROPE_DIM), F32)
    cos_f = jnp.concatenate([cos, cos, jnp.ones((seq, LANES - ROPE_DIM), F32)], axis=1)
    sin_a = jnp.concatenate([-sin, jnp.zeros((seq, half), F32), zeros], axis=1)
    sin_b = jnp.concatenate([jnp.zeros((seq, half), F32), sin, zeros], axis=1)
    return cos_f, sin_a, sin_b


def _prep_weights(w_in, gla_gate_w2, gla_gate_b, gla_norm_w, w_o, ln1_g, ln1_b, router_w, router_b,
                  w_gate_up, b_gate_up, w_down, b_down, ln2_g, ln2_b):
    sizes = (GLA_KEY, GLA_KEY, GLA_VAL, GLA_VAL, 2 * GLA_RANK, SWA_HEADS * SWA_HD,
             SWA_KV * SWA_HD, SWA_KV * SWA_HD, D_MODEL, D_MODEL)
    cols, acc = [], 0
    for s in sizes:
        cols.append(w_in[:, acc:acc + s])
        acc += s
    gq, gk, gv, gr, glr, sq, sk, sv, ga, gb = cols
    wa = jnp.concatenate([gq, gk, gv, gr], axis=1).astype(BF16)
    wl = jnp.pad(glr, ((0, 0), (0, LANES - 2 * GLA_RANK))).astype(BF16)
    wb = jnp.concatenate([ga, gb, sq, sk, sv], axis=1).astype(BF16)
    w2f = jnp.zeros((LANES, 2 * GLA_KEY), F32)
    w2f = w2f.at[:GLA_RANK, :GLA_KEY].set(gla_gate_w2[0])
    w2f = w2f.at[GLA_RANK:2 * GLA_RANK, GLA_KEY:].set(gla_gate_w2[1]).astype(BF16)
    rw_hi, rw_lo = _split2(jnp.pad(router_w, ((0, 0), (0, LANES - N_EXPERTS))))
    rw = jnp.concatenate([rw_hi, rw_lo], axis=1)
    rb = jnp.concatenate([router_b, jnp.full((LANES - N_EXPERTS,), -jnp.inf, F32)]).reshape(1, LANES)
    return dict(
        wa=wa, wl=wl, wb=wb, w2f=w2f, bgaf=gla_gate_b.reshape(1, 2 * GLA_KEY),
        nw=gla_norm_w.reshape(1, GLA_VAL), wo=w_o.astype(BF16),
        ln1_g=ln1_g.reshape(1, D_MODEL), ln1_b=ln1_b.reshape(1, D_MODEL), rw=rw, rb=rb,
        wgu=w_gate_up, bgu=b_gate_up.reshape(N_EXPERTS, 1, 2 * D_EXPERT),
        wd=w_down, bd=b_down.reshape(N_EXPERTS, 1, D_MODEL),
        ln2_g=ln2_g.reshape(1, D_MODEL), ln2_b=ln2_b.reshape(1, D_MODEL))


def _routing_plan(route, counts):
    idx, rank = route[:TOP_K], route[TOP_K:]
    t = route.shape[1]
    bm = MOE_BM
    n_rows = t * TOP_K + N_EXPERTS * bm
    n_blocks = n_rows // bm
    counts = counts[0, :N_EXPERTS].astype(I32)
    padded = (counts + bm - 1) // bm * bm
    pad_ends = jnp.cumsum(padded)
    pad_starts = pad_ends - padded
    dest = rank
    for ex in range(N_EXPERTS):
        dest = dest + jnp.where(idx == ex, pad_starts[ex], 0)
    block_start = jnp.arange(n_blocks, dtype=I32) * bm
    block_expert = jnp.minimum(
        jnp.sum(block_start[:, None] >= pad_ends[None, :], axis=1), N_EXPERTS - 1).astype(I32)
    mine = block_expert[:, None] == jnp.arange(N_EXPERTS, dtype=I32)[None, :]
    seg_end = jnp.sum(jnp.where(mine, (pad_starts + counts)[None, :], 0), axis=1)
    n_valid = jnp.clip(seg_end - block_start, 0, bm).astype(I32)
    n_used = (pad_ends[-1] // bm).astype(I32).reshape(1)
    return dest.reshape(-1), n_rows, block_expert, n_used, n_valid


def _trunk(x, w, sinks, gather_rows, scatter_rows):
    bsz, seq, _ = x.shape
    t = bsz * seq
    x2d = x.reshape(t, D_MODEL)
    cos_f, sin_a, sin_b = _rope_tables(seq)
    pa, bdec, pb = _in_proj(x2d, w["wa"], w["wl"], w["w2f"], w["bgaf"], w["wb"],
                            cos_f, sin_a, sin_b, seq)
    oa = _gla(pa, bdec, w["nw"], bsz, seq).reshape(t, GLA_VAL)
    ob = _swa(pb, sinks, bsz, seq).reshape(t, SWA_HEADS * SWA_HD)
    xp, gates, route, counts = _merge(
        oa, ob, pb, x2d, w["wo"], w["ln1_g"], w["ln1_b"], w["rw"], w["rb"])
    dest_km, n_rows, block_expert, n_used, n_valid = _routing_plan(route, counts)
    xs = scatter_rows(xp, dest_km, n_rows)
    ys = _experts(xs, block_expert, n_used, n_valid, w["wgu"], w["bgu"], w["wd"], w["bd"])
    yg = gather_rows(ys, dest_km).reshape(TOP_K, t, HALF)
    out = _combine(xp, yg, gates, w["ln2_g"], w["ln2_b"])
    return out.reshape(bsz, seq, D_MODEL)


def _forward(x_prompt, x_sample, w_in, gla_gate_w2, gla_gate_b, gla_norm_w, swa_sinks, w_o, ln1_g,
             ln1_b, router_w, router_b, w_gate_up, b_gate_up, w_down, b_down, ln2_g, ln2_b):
    w = _prep_weights(w_in[0], gla_gate_w2[0], gla_gate_b[0], gla_norm_w[0], w_o[0], ln1_g[0],
                      ln1_b[0], router_w[0], router_b[0], w_gate_up[0], b_gate_up[0], w_down[0],
                      b_down[0], ln2_g[0], ln2_b[0])
    sinks = swa_sinks[0].astype(F32)
    return (_trunk(x_prompt, w, sinks, _gather_rows, _scatter_rows),
            _trunk(x_sample, w, sinks, _gather_rows, _scatter_rows))


def kernel(x_prompt, x_sample, w_in, gla_gate_w2, gla_gate_b, gla_norm_w, swa_sinks, w_o, ln1_g, ln1_b, router_w, router_b, w_gate_up, b_gate_up, w_down, b_down, ln2_g, ln2_b):
    return _forward(x_prompt, x_sample, w_in, gla_gate_w2, gla_gate_b, gla_norm_w, swa_sinks, w_o,
                    ln1_g, ln1_b, router_w, router_b, w_gate_up, b_gate_up, w_down, b_down,
                    ln2_g, ln2_b)
```

```python
import functools

import jax
import jax.numpy as jnp
from jax import lax
from jax.experimental import pallas as pl
from jax.experimental.pallas import tpu as pltpu
from jax.experimental.pallas import tpu_sc as plsc

F32 = jnp.float32
BF16 = jnp.bfloat16
I32 = jnp.int32
U32 = jnp.uint32

D_MODEL = 1024
GLA_HEADS = 4
GLA_DK = 128
GLA_DV = 256
GLA_KEY = GLA_HEADS * GLA_DK
GLA_VAL = GLA_HEADS * GLA_DV
GLA_RANK = 16
GLA_NORMALIZER = 16.0
GLA_CHUNK = 64
SWA_HEADS = 8
SWA_KV = 2
SWA_GROUP = SWA_HEADS // SWA_KV
SWA_HD = 128
SWA_WINDOW = 128
ROPE_THETA = 500000.0
ROPE_DIM = SWA_HD // 4
N_EXPERTS = 32
TOP_K = 4
D_EXPERT = D_MODEL
SWIGLU_LIMIT = 7.0
SWIGLU_ALPHA = 1.702
DEEPNORM_ALPHA = 2.0 ** 0.25
LN_EPS = 1e-5
RMS_EPS = 1e-6

LANES = 128
PA_W = 2 * GLA_KEY + 2 * GLA_VAL
PB_W = 2 * D_MODEL + SWA_HEADS * SWA_HD + 2 * SWA_KV * SWA_HD
HALF = D_MODEL // 2

PROJ_TM = 512
MERGE_TM = 1024
MERGE_GROUP = 512
GLA_TS = 4096
GLA_GROUP = 512
SWA_TQ = 2048
MOE_BM = 1024
MOE_HC = 256
SC_CHUNK = 64
VMEM_LIMIT = 56 * 1024 * 1024


def _cparams(sem):
    return pltpu.CompilerParams(dimension_semantics=sem, vmem_limit_bytes=VMEM_LIMIT)


def _split2(a):
    hi = a.astype(BF16)
    lo = (a - hi.astype(F32)).astype(BF16)
    return hi, lo


def _dot(a, b):
    return jnp.dot(a, b, preferred_element_type=F32)


def _dot_nt(a, b):
    return lax.dot_general(a, b, (((1,), (1,)), ((), ())), preferred_element_type=F32)


def _sigmoid(x):
    return 0.5 * jnp.tanh(0.5 * x) + 0.5


def _pack_bf16_pair(lo_f32, hi_f32):
    lo = lax.bitcast_convert_type(lo_f32.astype(BF16).astype(F32), U32)
    hi = lax.bitcast_convert_type(hi_f32.astype(BF16).astype(F32), U32)
    return (hi & jnp.uint32(0xFFFF0000)) | (lo >> 16)


def _unpack_bf16_pair(w):
    lo = lax.bitcast_convert_type(w << 16, F32)
    hi = lax.bitcast_convert_type(w & jnp.uint32(0xFFFF0000), F32)
    return lo, hi


def _chunk_cumsum(x, reverse):
    rows = x.shape[0]
    pos = lax.broadcasted_iota(I32, x.shape, 0) & (GLA_CHUNK - 1)
    s = 1
    while s < GLA_CHUNK:
        if reverse:
            x = x + jnp.where(pos < GLA_CHUNK - s, pltpu.roll(x, rows - s, 0), 0.0)
        else:
            x = x + jnp.where(pos >= s, pltpu.roll(x, s, 0), 0.0)
        s *= 2
    return x


def _in_proj_kernel(x_ref, wa_ref, wl_ref, w2_ref, bga_ref, wb_ref, cos_ref, sina_ref, sinb_ref,
                    pa_ref, bd_ref, pb_ref):
    xb = x_ref[...].astype(BF16)
    lrb = _dot(xb, wl_ref[...]).astype(BF16)

    tm = x_ref.shape[0]
    piece_rows = 128

    def decay_piece(j, r0):
        cols = slice(j * LANES, (j + 1) * LANES)
        rows = slice(r0, r0 + piece_rows)
        z = _dot(lrb[rows], w2_ref[:, cols]) + bga_ref[:, cols]
        la = (jnp.minimum(z, 0.0) - jnp.log(1.0 + jnp.exp(-jnp.abs(z)))) * (1.0 / GLA_NORMALIZER)
        bd_ref[rows, cols] = _chunk_cumsum(la, reverse=j * LANES >= GLA_KEY)

    pieces = [(j, r0) for j in range(2 * GLA_KEY // LANES) for r0 in range(0, tm, piece_rows)]
    step = 256
    chunks_left = [(PA_W + PB_W) // step]

    def emit_pieces():
        for _ in range(-(-len(pieces) // chunks_left[0])):
            decay_piece(*pieces.pop(0))
        chunks_left[0] -= 1

    for c0 in range(0, PA_W, step):
        r = _dot(xb, wa_ref[:, c0:c0 + step])
        if c0 < GLA_KEY:
            r = r * (GLA_DK ** -0.5)
        pa_ref[:, c0:c0 + step] = r.astype(BF16)
        emit_pieces()
    cos = cos_ref[...]
    sina = sina_ref[...]
    sinb = sinb_ref[...]
    rope0 = 2 * D_MODEL
    rope_end = rope0 + (SWA_HEADS + SWA_KV) * SWA_HD
    for c0 in range(0, PB_W, step):
        r = _dot(xb, wb_ref[:, c0:c0 + step])
        if c0 + step > rope0 and c0 < rope_end:
            parts = []
            for j in range(step // SWA_HD):
                t = r[:, j * SWA_HD:(j + 1) * SWA_HD]
                col = c0 + j * SWA_HD
                if rope0 <= col < rope_end:
                    t = (t * cos + pltpu.roll(t, SWA_HD - ROPE_DIM // 2, 1) * sina
                         + pltpu.roll(t, ROPE_DIM // 2, 1) * sinb)
                if rope0 <= col < rope0 + SWA_HEADS * SWA_HD:
                    t = t * (SWA_HD ** -0.5)
                parts.append(t)
            r = jnp.concatenate(parts, axis=1)
        pb_ref[:, c0:c0 + step] = r.astype(BF16)
        emit_pieces()
    assert not pieces


def _in_proj(x2d, wa, wl, w2f, bgaf, wb, cos, sina, sinb, seq):
    t = x2d.shape[0]
    tm = PROJ_TM
    nseq = seq // tm
    const = lambda i: (0, 0)
    return pl.pallas_call(
        _in_proj_kernel,
        grid=(t // tm,),
        in_specs=[
            pl.BlockSpec((tm, D_MODEL), lambda i: (i, 0)),
            pl.BlockSpec((D_MODEL, PA_W), const),
            pl.BlockSpec((D_MODEL, LANES), const),
            pl.BlockSpec((LANES, 2 * GLA_KEY), const),
            pl.BlockSpec((1, 2 * GLA_KEY), const),
            pl.BlockSpec((D_MODEL, PB_W), const),
            pl.BlockSpec((tm, LANES), lambda i: (i % nseq, 0)),
            pl.BlockSpec((tm, LANES), lambda i: (i % nseq, 0)),
            pl.BlockSpec((tm, LANES), lambda i: (i % nseq, 0)),
        ],
        out_specs=[
            pl.BlockSpec((tm, PA_W), lambda i: (i, 0)),
            pl.BlockSpec((tm, 2 * GLA_KEY), lambda i: (i, 0)),
            pl.BlockSpec((tm, PB_W), lambda i: (i, 0)),
        ],
        out_shape=[
            jax.ShapeDtypeStruct((t, PA_W), BF16),
            jax.ShapeDtypeStruct((t, 2 * GLA_KEY), F32),
            jax.ShapeDtypeStruct((t, PB_W), BF16),
        ],
        compiler_params=_cparams(("parallel",)),
        name="in_proj",
    )(x2d, wa, wl, w2f, bgaf, wb, cos, sina, sinb)


def _gla_kernel(q_ref, k_ref, v_ref, r_ref, b_ref, nw_ref, o_ref,
                state_ref, ofwd_ref, *group_scratch, ts, nt):
    ph = pl.program_id(2)
    t = pl.program_id(3)
    c = GLA_CHUNK
    gs = min(GLA_GROUP, ts)
    n_groups = ts // gs
    nc = gs // c

    @pl.when(t == 0)
    def _():
        state_ref[...] = jnp.zeros_like(state_ref)

    row = lax.broadcasted_iota(I32, (c, c), 0)
    col = lax.broadcasted_iota(I32, (c, c), 1)

    def run(forward):
        keep = (col <= row) if forward else (col > row)
        mid = c // 2 if forward else c // 2 - 1
        end = c - 1 if forward else 0
        off = pl.multiple_of((t if forward else nt - 1 - t) * ts, ts)
        group_order = range(n_groups) if forward else reversed(range(n_groups))
        chunk_order = list(range(nc)) if forward else list(reversed(range(nc)))
        st = state_ref[...]
        for g in group_order:
            sall_ref, ds_ref, dec_ref = group_scratch[3 * g:3 * g + 3]
            rows = slice(g * gs, (g + 1) * gs)
            q3 = q_ref[0, rows, :].reshape(nc, c, GLA_DK)
            k3 = k_ref[0, rows, :].reshape(nc, c, GLA_DK)
            v3 = v_ref[0, rows, :].reshape(nc, c, GLA_DV)
            b3 = b_ref[0, rows, :].reshape(nc, c, GLA_DK)
            b_mid = b3[:, mid:mid + 1]
            b_end = b3[:, end:end + 1]
            q_in = q3 * jnp.exp(b3 - b_mid).astype(BF16)
            k_in = k3 * jnp.exp(b_mid - b3).astype(BF16)
            k_st = k_in * jnp.exp(b_end - b_mid).astype(BF16)
            q_b = q_in * jnp.exp(b_mid).astype(BF16)
            dec_ref[...] = jnp.exp(b_end)
            scores = jnp.einsum("cid,cjd->cij", q_in, k_in, preferred_element_type=F32)
            scores = jnp.where(keep[None], scores, 0.0).astype(BF16)
            o_intra = jnp.einsum("cij,cjv->civ", scores, v3, preferred_element_type=F32)
            ds_ref[...] = jnp.einsum("cjv,cjd->cvd", v3, k_st, preferred_element_type=F32)
            for ci in chunk_order:
                sall_ref[ci] = st.astype(BF16)
                st = st * dec_ref[ci] + ds_ref[ci]
            o_inter = jnp.einsum("cid,cvd->civ", q_b, sall_ref[...], preferred_element_type=F32)
            o = (o_intra + o_inter).reshape(gs, GLA_DV)
            orows = pl.ds(pl.multiple_of(off + g * gs, gs), gs)
            if forward:
                ofwd_ref[orows, :] = o
            else:
                tot = o + ofwd_ref[orows, :]
                tot = tot * lax.rsqrt(jnp.mean(tot * tot, axis=-1, keepdims=True) + RMS_EPS)
                r = r_ref[0, rows, :]
                o_ref[0, rows, :] = (
                    tot * nw_ref[...] * (r * _sigmoid(r)).astype(F32)).astype(BF16)
        state_ref[...] = st

    @pl.when(ph == 0)
    def _():
        run(True)

    @pl.when(ph == 1)
    def _():
        run(False)


def _gla(pa, bdec, nw, bsz, seq):
    ts = min(GLA_TS, seq)
    nt = seq // ts
    gs = min(GLA_GROUP, ts)
    nc = gs // GLA_CHUNK
    group_scratch = []
    for _ in range(ts // gs):
        group_scratch += [pltpu.VMEM((nc, GLA_DV, GLA_DK), BF16),
                          pltpu.VMEM((nc, GLA_DV, GLA_DK), F32),
                          pltpu.VMEM((nc, 1, GLA_DK), F32)]
    pa3 = pa.reshape(bsz, seq, PA_W)
    bd3 = bdec.reshape(bsz, seq, 2 * GLA_KEY)
    tile = lambda p, t: t + p * (nt - 1 - 2 * t)
    wk, wv = GLA_DK, GLA_DV
    kq = GLA_KEY // wk
    kv = 2 * GLA_KEY // wv
    kr = kv + GLA_VAL // wv
    kern = functools.partial(_gla_kernel, ts=ts, nt=nt)
    return pl.pallas_call(
        kern,
        grid=(bsz, GLA_HEADS, 2, nt),
        in_specs=[
            pl.BlockSpec((1, ts, wk), lambda b, h, p, t: (b, tile(p, t), h)),
            pl.BlockSpec((1, ts, wk), lambda b, h, p, t: (b, tile(p, t), kq + h)),
            pl.BlockSpec((1, ts, wv), lambda b, h, p, t: (b, tile(p, t), kv + h)),
            pl.BlockSpec((1, ts, wv), lambda b, h, p, t: (b, nt - 1 - p * t, kr + h)),
            pl.BlockSpec((1, ts, wk), lambda b, h, p, t: (b, tile(p, t), p * kq + h)),
            pl.BlockSpec((1, wv), lambda b, h, p, t: (0, h)),
        ],
        out_specs=pl.BlockSpec((1, ts, wv), lambda b, h, p, t: (b, nt - 1 - p * t, h)),
        out_shape=jax.ShapeDtypeStruct((bsz, seq, GLA_VAL), BF16),
        scratch_shapes=[
            pltpu.VMEM((GLA_DV, GLA_DK), F32),
            pltpu.VMEM((seq, GLA_DV), F32),
        ] + group_scratch,
        compiler_params=_cparams(("parallel", "parallel", "arbitrary", "arbitrary")),
        name="gla",
    )(pa3, pa3, pa3, pa3, bd3, nw)


def _swa_kernel(sink_ref, q_ref, kl_ref, kc_ref, kr_ref, vl_ref, vc_ref, vr_ref, o_ref, *, tq):
    kvh = pl.program_id(1)
    n = pl.program_id(2)
    blk = SWA_WINDOW
    k_all = jnp.concatenate([kl_ref[0], kc_ref[0], kr_ref[0]], axis=0)
    v_all = jnp.concatenate([vl_ref[0], vc_ref[0], vr_ref[0]], axis=0)
    rows = SWA_GROUP * blk
    qi = lax.broadcasted_iota(I32, (rows, 3 * blk), 0) & (blk - 1)
    kj = lax.broadcasted_iota(I32, (rows, 3 * blk), 1) - blk
    band = jnp.where(jnp.abs(kj - qi) <= SWA_WINDOW, 0.0, -jnp.inf)
    head = lax.broadcasted_iota(I32, (rows, 1), 0) // blk
    sink = jnp.zeros((rows, 1), F32)
    for g in range(SWA_GROUP):
        sink = jnp.where(head == g, sink_ref[kvh * SWA_GROUP + g], sink)
    n_sub = tq // blk
    before = jnp.where(n == 0, -jnp.inf, 0.0)
    after = jnp.where(n == pl.num_programs(2) - 1, -jnp.inf, 0.0)
    for jq in range(n_sub):
        bias = band
        if jq == 0:
            bias = bias + jnp.where(kj < 0, before, 0.0)
        if jq == n_sub - 1:
            bias = bias + jnp.where(kj >= blk, after, 0.0)
        qs = q_ref[0, jq * blk:(jq + 1) * blk, :]
        q4 = jnp.concatenate([qs[:, g * SWA_HD:(g + 1) * SWA_HD] for g in range(SWA_GROUP)], axis=0)
        keys = k_all[jq * blk:(jq + 3) * blk]
        vals = v_all[jq * blk:(jq + 3) * blk]
        s = _dot_nt(q4, keys) + bias
        m = jnp.maximum(jnp.max(s, axis=-1, keepdims=True), sink)
        p = jnp.exp(s - m)
        denom = jnp.sum(p, axis=-1, keepdims=True) + jnp.exp(sink - m)
        o = _dot(p.astype(BF16), vals) * (1.0 / denom)
        for g in range(SWA_GROUP):
            o_ref[0, jq * blk:(jq + 1) * blk, g * SWA_HD:(g + 1) * SWA_HD] = (
                o[g * blk:(g + 1) * blk].astype(BF16))


def _swa(pb, sinks, bsz, seq):
    tq = min(SWA_TQ, seq)
    blk = SWA_WINDOW
    r = tq // blk
    nb = seq // blk
    pb3 = pb.reshape(bsz, seq, PB_W)
    gw = SWA_GROUP * SWA_HD
    cq = 2 * D_MODEL // gw
    ck = (2 * D_MODEL + SWA_HEADS * SWA_HD) // SWA_HD
    cv = ck + SWA_KV
    left = lambda n: jnp.maximum(n * r - 1, 0)
    right = lambda n: jnp.minimum(n * r + r, nb - 1)
    kern = functools.partial(_swa_kernel, tq=tq)
    grid_spec = pltpu.PrefetchScalarGridSpec(
        num_scalar_prefetch=1,
        grid=(bsz, SWA_KV, seq // tq),
        in_specs=[
            pl.BlockSpec((1, tq, gw), lambda b, h, n, s: (b, n, cq + h)),
            pl.BlockSpec((1, blk, SWA_HD), lambda b, h, n, s: (b, left(n), ck + h)),
            pl.BlockSpec((1, tq, SWA_HD), lambda b, h, n, s: (b, n, ck + h)),
            pl.BlockSpec((1, blk, SWA_HD), lambda b, h, n, s: (b, right(n), ck + h)),
            pl.BlockSpec((1, blk, SWA_HD), lambda b, h, n, s: (b, left(n), cv + h)),
            pl.BlockSpec((1, tq, SWA_HD), lambda b, h, n, s: (b, n, cv + h)),
            pl.BlockSpec((1, blk, SWA_HD), lambda b, h, n, s: (b, right(n), cv + h)),
        ],
        out_specs=pl.BlockSpec((1, tq, gw), lambda b, h, n, s: (b, n, h)),
    )
    return pl.pallas_call(
        kern,
        grid_spec=grid_spec,
        out_shape=jax.ShapeDtypeStruct((bsz, seq, SWA_HEADS * SWA_HD), BF16),
        compiler_params=_cparams(("parallel", "parallel", "parallel")),
        name="swa",
    )(sinks, pb3, pb3, pb3, pb3, pb3, pb3, pb3)


def _layer_norm(y, g, b):
    mu = jnp.mean(y, axis=-1, keepdims=True)
    d = y - mu
    var = jnp.mean(d * d, axis=-1, keepdims=True)
    return d * lax.rsqrt(var + LN_EPS) * g + b


def _merge_kernel(oa_ref, ob_ref, ga_ref, gb_ref, x_ref, wo_ref, g_ref, b_ref, rw_ref, rb_ref,
                  before_ref, xp_ref, gate_ref, route_ref, cnt_ref, run_ref, *, tm):
    @pl.when(pl.program_id(0) == 0)
    def _():
        run_ref[...] = jnp.zeros_like(run_ref)

    gm = MERGE_GROUP
    lane_i = lax.broadcasted_iota(I32, (gm, LANES), 1)
    lane = lane_i.astype(F32)

    def spread(cols):
        out = jnp.zeros((gm, LANES), F32)
        for kk, col in enumerate(cols):
            out = jnp.where(lane_i == kk, col, out)
        return out

    run = run_ref[...]
    for r0 in range(0, tm, gm):
        rows = slice(r0, r0 + gm)
        h = (_sigmoid(ga_ref[rows, :]) * oa_ref[rows, :]
             + _sigmoid(gb_ref[rows, :]) * ob_ref[rows, :])
        y = DEEPNORM_ALPHA * x_ref[rows, :] + _dot(h, wo_ref[...])
        x1 = _layer_norm(y, g_ref[...], b_ref[...])
        xp_ref[rows, :] = _pack_bf16_pair(x1[:, :HALF], x1[:, HALF:])

        x_hi, x_lo = _split2(x1)
        both = _dot(x_hi, rw_ref[...])
        logits = (both[:, :LANES] + both[:, LANES:] + _dot(x_lo, rw_ref[:, :LANES])
                  + rb_ref[...])
        work = logits
        vals, idxs, hots = [], [], []
        for _ in range(TOP_K):
            m = jnp.max(work, axis=-1, keepdims=True)
            sel = jnp.min(jnp.where(work == m, lane, float(LANES)), axis=-1, keepdims=True)
            hot = lane == sel
            vals.append(m)
            idxs.append(sel)
            hots.append(hot)
            work = jnp.where(hot, -jnp.inf, work)
        e = [jnp.exp(v - vals[0]) for v in vals]
        tot = e[0] + e[1] + e[2] + e[3]
        gate_ref[rows, :] = spread([ei / tot for ei in e])

        member = (hots[0] | hots[1] | hots[2] | hots[3])
        cum = _dot(before_ref[...], member.astype(BF16)) + run
        ranks = [jnp.sum(jnp.where(hot, cum, 0.0), axis=-1, keepdims=True) for hot in hots]
        route = spread(idxs + ranks).T
        route_ref[:, rows] = route[:2 * TOP_K].astype(I32)
        run = run + jnp.sum(member.astype(F32), axis=0, keepdims=True)
    run_ref[...] = run
    cnt_ref[...] = run


def _merge(oa, ob, pb, x2d, wo, g, b, rw, rb):
    t = x2d.shape[0]
    tm = MERGE_TM
    const = lambda i: (0, 0)
    kern = functools.partial(_merge_kernel, tm=tm)
    gm = MERGE_GROUP
    before = (jnp.arange(gm)[None, :] < jnp.arange(gm)[:, None]).astype(BF16)
    return pl.pallas_call(
        kern,
        grid=(t // tm,),
        in_specs=[
            pl.BlockSpec((tm, D_MODEL), lambda i: (i, 0)),
            pl.BlockSpec((tm, D_MODEL), lambda i: (i, 0)),
            pl.BlockSpec((tm, D_MODEL), lambda i: (i, 0)),
            pl.BlockSpec((tm, D_MODEL), lambda i: (i, 1)),
            pl.BlockSpec((tm, D_MODEL), lambda i: (i, 0)),
            pl.BlockSpec((D_MODEL, D_MODEL), const),
            pl.BlockSpec((1, D_MODEL), const),
            pl.BlockSpec((1, D_MODEL), const),
            pl.BlockSpec((D_MODEL, 2 * LANES), const),
            pl.BlockSpec((1, LANES), const),
            pl.BlockSpec((gm, gm), const),
        ],
        out_specs=[
            pl.BlockSpec((tm, HALF), lambda i: (i, 0)),
            pl.BlockSpec((tm, LANES), lambda i: (i, 0)),
            pl.BlockSpec((2 * TOP_K, tm), lambda i: (0, i)),
            pl.BlockSpec((1, LANES), const),
        ],
        out_shape=[
            jax.ShapeDtypeStruct((t, HALF), U32),
            jax.ShapeDtypeStruct((t, LANES), F32),
            jax.ShapeDtypeStruct((2 * TOP_K, t), I32),
            jax.ShapeDtypeStruct((1, LANES), F32),
        ],
        scratch_shapes=[pltpu.VMEM((1, LANES), F32)],
        compiler_params=_cparams(("arbitrary",)),
        name="merge",
    )(oa, ob, pb, pb, x2d, wo, g, b, rw, rb, before)


def _gather_rows(table, idx):
    n = idx.shape[0]
    d = table.shape[1]
    info = plsc.get_sparse_core_info()
    nw = info.num_cores * info.num_subcores
    per_w = n // nw
    assert per_w * nw == n and per_w % SC_CHUNK == 0
    n_chunks = per_w // SC_CHUNK
    mesh = plsc.VectorSubcoreMesh(core_axis_name="c", subcore_axis_name="s")

    @functools.partial(
        pl.kernel, mesh=mesh,
        out_type=jax.ShapeDtypeStruct((n, d), table.dtype),
        scratch_types=[
            pltpu.VMEM((SC_CHUNK,), I32),
            pltpu.VMEM((SC_CHUNK, d), table.dtype),
            pltpu.SemaphoreType.DMA,
        ],
    )
    def k(table_hbm, idx_hbm, out_hbm, idx_v, rows_v, sem):
        wid = lax.axis_index("s") * info.num_cores + lax.axis_index("c")
        base = wid * per_w

        @pl.loop(0, n_chunks)
        def _(j):
            off = pl.multiple_of(base + j * SC_CHUNK, 8)
            pltpu.sync_copy(idx_hbm.at[pl.ds(off, SC_CHUNK)], idx_v)
            pltpu.async_copy(table_hbm.at[idx_v], rows_v, sem).wait()
            pltpu.sync_copy(rows_v, out_hbm.at[pl.ds(off, SC_CHUNK)])

    return k(table, idx)


def _scatter_rows(src, dest_km, n_rows):
    t, d = src.shape
    info = plsc.get_sparse_core_info()
    nw = info.num_cores * info.num_subcores
    per_w = t // nw
    assert per_w * nw == t and per_w % SC_CHUNK == 0
    n_chunks = per_w // SC_CHUNK
    mesh = plsc.VectorSubcoreMesh(core_axis_name="c", subcore_axis_name="s")

    @functools.partial(
        pl.kernel, mesh=mesh,
        out_type=jax.ShapeDtypeStruct((n_rows, d), src.dtype),
        scratch_types=[
            pltpu.VMEM((SC_CHUNK,), I32),
            pltpu.VMEM((SC_CHUNK, d), src.dtype),
            pltpu.SemaphoreType.DMA,
        ],
    )
    def k(src_hbm, dest_hbm, out_hbm, idx_v, rows_v, sem):
        wid = lax.axis_index("s") * info.num_cores + lax.axis_index("c")
        base = wid * per_w

        @pl.loop(0, n_chunks)
        def _(j):
            off = pl.multiple_of(base + j * SC_CHUNK, 8)
            pltpu.sync_copy(src_hbm.at[pl.ds(off, SC_CHUNK)], rows_v)
            for kk in range(TOP_K):
                pltpu.sync_copy(dest_hbm.at[pl.ds(pl.multiple_of(kk * t + off, 8), SC_CHUNK)], idx_v)
                pltpu.async_copy(rows_v, out_hbm.at[idx_v], sem).wait()

    return k(src, dest_km)


def _expert_kernel(be_ref, nused_ref, nvalid_ref, xs_ref, wgu_ref, bgu_ref, wd_ref, bd_ref, ys_ref,
                   wgu_bf, wd_bf):
    i = pl.program_id(0)

    @pl.when(i < nused_ref[0])
    def _():
        @pl.when((i == 0) | (be_ref[i] != be_ref[jnp.maximum(i - 1, 0)]))
        def _():
            rows = 256
            for r0 in range(0, D_MODEL, rows):
                wgu_bf[r0:r0 + rows, :] = wgu_ref[0, r0:r0 + rows, :].astype(BF16)
            for r0 in range(0, D_EXPERT, rows):
                wd_bf[r0:r0 + rows, :] = wd_ref[0, r0:r0 + rows, :].astype(BF16)

        def proj(x, col):
            return _dot(x, wgu_bf[:, col:col + MOE_HC]) + bgu_ref[0, :, col:col + MOE_HC]

        rowid = lax.broadcasted_iota(I32, (MOE_BM, 1), 0)
        words = jnp.where(rowid < nvalid_ref[i], xs_ref[...], jnp.uint32(0))
        lo, hi = _unpack_bf16_pair(words)
        x = jnp.concatenate([lo.astype(BF16), hi.astype(BF16)], axis=1)
        acts = []
        for c0 in range(0, D_EXPERT, MOE_HC):
            gate = jnp.minimum(proj(x, c0), SWIGLU_LIMIT)
            up = jnp.clip(proj(x, D_EXPERT + c0), -SWIGLU_LIMIT, SWIGLU_LIMIT)
            acts.append(((up + 1.0) * gate * _sigmoid(gate * SWIGLU_ALPHA)).astype(BF16))
        y = _dot(jnp.concatenate(acts, axis=1), wd_bf[...]) + bd_ref[0]
        ys_ref[...] = _pack_bf16_pair(y[:, :HALF], y[:, HALF:])


def _experts(xs, block_expert, n_used, n_valid, wgu, bgu, wd, bd):
    n_rows = xs.shape[0]
    bm = MOE_BM
    n_blocks = n_rows // bm
    rows = lambda i, be, nu, nv: (jnp.minimum(i, nu[0] - 1), 0)
    expert = lambda i, be, nu, nv: (be[i], 0, 0)
    grid_spec = pltpu.PrefetchScalarGridSpec(
        num_scalar_prefetch=3,
        grid=(n_blocks,),
        in_specs=[
            pl.BlockSpec((bm, HALF), rows),
            pl.BlockSpec((1, D_MODEL, 2 * D_EXPERT), expert),
            pl.BlockSpec((1, 1, 2 * D_EXPERT), expert),
            pl.BlockSpec((1, D_EXPERT, D_MODEL), expert),
            pl.BlockSpec((1, 1, D_MODEL), expert),
        ],
        out_specs=pl.BlockSpec((bm, HALF), rows),
        scratch_shapes=[
            pltpu.VMEM((D_MODEL, 2 * D_EXPERT), BF16),
            pltpu.VMEM((D_EXPERT, D_MODEL), BF16),
        ],
    )
    return pl.pallas_call(
        _expert_kernel,
        grid_spec=grid_spec,
        out_shape=jax.ShapeDtypeStruct((n_rows, HALF), U32),
        compiler_params=_cparams(("arbitrary",)),
        name="experts",
    )(block_expert, n_used, n_valid, xs, wgu, bgu, wd, bd)


def _combine_kernel(xp_ref, yg_ref, gate_ref, g_ref, b_ref, o_ref):
    gates = gate_ref[...]
    f_lo = None
    f_hi = None
    for kk in range(TOP_K):
        lo, hi = _unpack_bf16_pair(yg_ref[kk])
        w = gates[:, kk:kk + 1]
        f_lo = lo * w if f_lo is None else f_lo + lo * w
        f_hi = hi * w if f_hi is None else f_hi + hi * w
    x_lo, x_hi = _unpack_bf16_pair(xp_ref[...])
    y_lo = DEEPNORM_ALPHA * x_lo + f_lo
    y_hi = DEEPNORM_ALPHA * x_hi + f_hi
    mu = (jnp.sum(y_lo, axis=-1, keepdims=True) + jnp.sum(y_hi, axis=-1, keepdims=True)) / D_MODEL
    d_lo = y_lo - mu
    d_hi = y_hi - mu
    var = (jnp.sum(d_lo * d_lo, axis=-1, keepdims=True)
           + jnp.sum(d_hi * d_hi, axis=-1, keepdims=True)) / D_MODEL
    inv = lax.rsqrt(var + LN_EPS)
    g = g_ref[...]
    b = b_ref[...]
    o_ref[:, :HALF] = d_lo * inv * g[:, :HALF] + b[:, :HALF]
    o_ref[:, HALF:] = d_hi * inv * g[:, HALF:] + b[:, HALF:]


def _combine(xp, yg, gates, g, b):
    t = xp.shape[0]
    tm = MERGE_TM
    const = lambda i: (0, 0)
    return pl.pallas_call(
        _combine_kernel,
        grid=(t // tm,),
        in_specs=[
            pl.BlockSpec((tm, HALF), lambda i: (i, 0)),
            pl.BlockSpec((TOP_K, tm, HALF), lambda i: (0, i, 0)),
            pl.BlockSpec((tm, LANES), lambda i: (i, 0)),
            pl.BlockSpec((1, D_MODEL), const),
            pl.BlockSpec((1, D_MODEL), const),
        ],
        out_specs=pl.BlockSpec((tm, D_MODEL), lambda i: (i, 0)),
        out_shape=jax.ShapeDtypeStruct((t, D_MODEL), F32),
        compiler_params=_cparams(("parallel",)),
        name="combine",
    )(xp, yg, gates, g, b)


def _rope_tables(seq):
    half = ROPE_DIM // 2
    pos = jnp.arange(seq, dtype=F32)
    inv_freq = ROPE_THETA ** (-jnp.arange(0, ROPE_DIM, 2, dtype=F32) / ROPE_DIM)
    ang = pos[:, None] * inv_freq[None, :]
    cos, sin = jnp.cos(ang), jnp.sin(ang)
    zeros = jnp.zeros((seq, LANES - ROPE_DIM), F32)
    cos_f = jnp.concatenate([cos, cos, jnp.ones((seq, LANES - ROPE_DIM), F32)], axis=1)
    sin_a = jnp.concatenate([-sin, jnp.zeros((seq, half), F32), zeros], axis=1)
    sin_b = jnp.concatenate([jnp.zeros((seq, half), F32), sin, zeros], axis=1)
    return cos_f, sin_a, sin_b


def _prep_weights(w_in, gla_gate_w2, gla_gate_b, gla_norm_w, w_o, ln1_g, ln1_b, router_w, router_b,
                  w_gate_up, b_gate_up, w_down, b_down, ln2_g, ln2_b):
    sizes = (GLA_KEY, GLA_KEY, GLA_VAL, GLA_VAL, 2 * GLA_RANK, SWA_HEADS * SWA_HD,
             SWA_KV * SWA_HD, SWA_KV * SWA_HD, D_MODEL, D_MODEL)
    cols, acc = [], 0
    for s in sizes:
        cols.append(w_in[:, acc:acc + s])
        acc += s
    gq, gk, gv, gr, glr, sq, sk, sv, ga, gb = cols
    wa = jnp.concatenate([gq, gk, gv, gr], axis=1).astype(BF16)
    wl = jnp.pad(glr, ((0, 0), (0, LANES - 2 * GLA_RANK))).astype(BF16)
    wb = jnp.concatenate([ga, gb, sq, sk, sv], axis=1).astype(BF16)
    w2f = jnp.zeros((LANES, 2 * GLA_KEY), F32)
    w2f = w2f.at[:GLA_RANK, :GLA_KEY].set(gla_gate_w2[0])
    w2f = w2f.at[GLA_RANK:2 * GLA_RANK, GLA_KEY:].set(gla_gate_w2[1]).astype(BF16)
    rw_hi, rw_lo = _split2(jnp.pad(router_w, ((0, 0), (0, LANES - N_EXPERTS))))
    rw = jnp.concatenate([rw_hi, rw_lo], axis=1)
    rb = jnp.concatenate([router_b, jnp.full((LANES - N_EXPERTS,), -jnp.inf, F32)]).reshape(1, LANES)
    return dict(
        wa=wa, wl=wl, wb=wb, w2f=w2f, bgaf=gla_gate_b.reshape(1, 2 * GLA_KEY),
        nw=gla_norm_w.reshape(1, GLA_VAL), wo=w_o.astype(BF16),
        ln1_g=ln1_g.reshape(1, D_MODEL), ln1_b=ln1_b.reshape(1, D_MODEL), rw=rw, rb=rb,
        wgu=w_gate_up, bgu=b_gate_up.reshape(N_EXPERTS, 1, 2 * D_EXPERT),
        wd=w_down, bd=b_down.reshape(N_EXPERTS, 1, D_MODEL),
        ln2_g=ln2_g.reshape(1, D_MODEL), ln2_b=ln2_b.reshape(1, D_MODEL))


def _routing_plan(route, counts):
    idx, rank = route[:TOP_K], route[TOP_K:]
    t = route.shape[1]
    bm = MOE_BM
    n_rows = t * TOP_K + N_EXPERTS * bm
    n_blocks = n_rows // bm
    counts = counts[0, :N_EXPERTS].astype(I32)
    padded = (counts + bm - 1) // bm * bm
    pad_ends = jnp.cumsum(padded)
    pad_starts = pad_ends - padded
    dest = rank
    for ex in range(N_EXPERTS):
        dest = dest + jnp.where(idx == ex, pad_starts[ex], 0)
    block_start = jnp.arange(n_blocks, dtype=I32) * bm
    block_expert = jnp.minimum(
        jnp.sum(block_start[:, None] >= pad_ends[None, :], axis=1), N_EXPERTS - 1).astype(I32)
    mine = block_expert[:, None] == jnp.arange(N_EXPERTS, dtype=I32)[None, :]
    seg_end = jnp.sum(jnp.where(mine, (pad_starts + counts)[None, :], 0), axis=1)
    n_valid = jnp.clip(seg_end - block_start, 0, bm).astype(I32)
    n_used = (pad_ends[-1] // bm).astype(I32).reshape(1)
    return dest.reshape(-1), n_rows, block_expert, n_used, n_valid


def _trunk(x, w, sinks, gather_rows, scatter_rows):
    bsz, seq, _ = x.shape
    t = bsz * seq
    x2d = x.reshape(t, D_MODEL)
    cos_f, sin_a, sin_b = _rope_tables(seq)
    pa, bdec, pb = _in_proj(x2d, w["wa"], w["wl"], w["w2f"], w["bgaf"], w["wb"],
                            cos_f, sin_a, sin_b, seq)
    oa = _gla(pa, bdec, w["nw"], bsz, seq).reshape(t, GLA_VAL)
    ob = _swa(pb, sinks, bsz, seq).reshape(t, SWA_HEADS * SWA_HD)
    xp, gates, route, counts = _merge(
        oa, ob, pb, x2d, w["wo"], w["ln1_g"], w["ln1_b"], w["rw"], w["rb"])
    dest_km, n_rows, block_expert, n_used, n_valid = _routing_plan(route, counts)
    xs = scatter_rows(xp, dest_km, n_rows)
    ys = _experts(xs, block_expert, n_used, n_valid, w["wgu"], w["bgu"], w["wd"], w["bd"])
    yg = gather_rows(ys, dest_km).reshape(TOP_K, t, HALF)
    out = _combine(xp, yg, gates, w["ln2_g"], w["ln2_b"])
    return out.reshape(bsz, seq, D_MODEL)


def _forward(x_prompt, x_sample, w_in, gla_gate_w2, gla_gate_b, gla_norm_w, swa_sinks, w_o, ln1_g,
             ln1_b, router_w, router_b, w_gate_up, b_gate_up, w_down, b_down, ln2_g, ln2_b):
    w = _prep_weights(w_in[0], gla_gate_w2[0], gla_gate_b[0], gla_norm_w[0], w_o[0], ln1_g[0],
                      ln1_b[0], router_w[0], router_b[0], w_gate_up[0], b_gate_up[0], w_down[0],
                      b_down[0], ln2_g[0], ln2_b[0])
    sinks = swa_sinks[0].astype(F32)
    return (_trunk(x_prompt, w, sinks, _gather_rows, _scatter_rows),
            _trunk(x_sample, w, sinks, _gather_rows, _scatter_rows))


def kernel(x_prompt, x_sample, w_in, gla_gate_w2, gla_gate_b, gla_norm_w, swa_sinks, w_o, ln1_g, ln1_b, router_w, router_b, w_gate_up, b_gate_up, w_down, b_down, ln2_g, ln2_b):
    return _forward(x_prompt, x_sample, w_in, gla_gate_w2, gla_gate_b, gla_norm_w, swa_sinks, w_o,
                    ln1_g, ln1_b, router_w, router_b, w_gate_up, b_gate_up, w_down, b_down,
                    ln2_g, ln2_b)
```

```python
import functools

import jax
import jax.numpy as jnp
from jax import lax
from jax.experimental import pallas as pl
from jax.experimental.pallas import tpu as pltpu
from jax.experimental.pallas import tpu_sc as plsc

F32 = jnp.float32
BF16 = jnp.bfloat16
I32 = jnp.int32
U32 = jnp.uint32

D_MODEL = 1024
GLA_HEADS = 4
GLA_DK = 128
GLA_DV = 256
GLA_KEY = GLA_HEADS * GLA_DK
GLA_VAL = GLA_HEADS * GLA_DV
GLA_RANK = 16
GLA_NORMALIZER = 16.0
GLA_CHUNK = 64
SWA_HEADS = 8
SWA_KV = 2
SWA_GROUP = SWA_HEADS // SWA_KV
SWA_HD = 128
SWA_WINDOW = 128
ROPE_THETA = 500000.0
ROPE_DIM = SWA_HD // 4
N_EXPERTS = 32
TOP_K = 4
D_EXPERT = D_MODEL
SWIGLU_LIMIT = 7.0
SWIGLU_ALPHA = 1.702
DEEPNORM_ALPHA = 2.0 ** 0.25
LN_EPS = 1e-5
RMS_EPS = 1e-6

LANES = 128
PA_W = 2 * GLA_KEY + 2 * GLA_VAL
PB_W = 2 * D_MODEL + SWA_HEADS * SWA_HD + 2 * SWA_KV * SWA_HD
HALF = D_MODEL // 2

PROJ_TM = 512
MERGE_TM = 1024
MERGE_GROUP = 512
GLA_TS = 4096
GLA_GROUP = 512
SWA_TQ = 2048
MOE_BM = 1024
MOE_HC = 256
SC_CHUNK = 64
VMEM_LIMIT = 56 * 1024 * 1024


def _cparams(sem):
    return pltpu.CompilerParams(dimension_semantics=sem, vmem_limit_bytes=VMEM_LIMIT)


def _split2(a):
    hi = a.astype(BF16)
    lo = (a - hi.astype(F32)).astype(BF16)
    return hi, lo


def _dot(a, b):
    return jnp.dot(a, b, preferred_element_type=F32)


def _dot_nt(a, b):
    return lax.dot_general(a, b, (((1,), (1,)), ((), ())), preferred_element_type=F32)


def _sigmoid(x):
    return 0.5 * jnp.tanh(0.5 * x) + 0.5


def _pack_bf16_pair(lo_f32, hi_f32):
    lo = lax.bitcast_convert_type(lo_f32.astype(BF16).astype(F32), U32)
    hi = lax.bitcast_convert_type(hi_f32.astype(BF16).astype(F32), U32)
    return (hi & jnp.uint32(0xFFFF0000)) | (lo >> 16)


def _unpack_bf16_pair(w):
    lo = lax.bitcast_convert_type(w << 16, F32)
    hi = lax.bitcast_convert_type(w & jnp.uint32(0xFFFF0000), F32)
    return lo, hi


def _chunk_cumsum(x, reverse):
    rows = x.shape[0]
    pos = lax.broadcasted_iota(I32, x.shape, 0) & (GLA_CHUNK - 1)
    s = 1
    while s < GLA_CHUNK:
        if reverse:
            x = x + jnp.where(pos < GLA_CHUNK - s, pltpu.roll(x, rows - s, 0), 0.0)
        else:
            x = x + jnp.where(pos >= s, pltpu.roll(x, s, 0), 0.0)
        s *= 2
    return x


def _in_proj_kernel(x_ref, wa_ref, wl_ref, w2_ref, bga_ref, wb_ref, cos_ref, sina_ref, sinb_ref,
                    pa_ref, bd_ref, pb_ref):
    xb = x_ref[...].astype(BF16)
    lrb = _dot(xb, wl_ref[...]).astype(BF16)

    tm = x_ref.shape[0]
    piece_rows = 128

    def decay_piece(j, r0):
        cols = slice(j * LANES, (j + 1) * LANES)
        rows = slice(r0, r0 + piece_rows)
        z = _dot(lrb[rows], w2_ref[:, cols]) + bga_ref[:, cols]
        la = (jnp.minimum(z, 0.0) - jnp.log(1.0 + jnp.exp(-jnp.abs(z)))) * (1.0 / GLA_NORMALIZER)
        bd_ref[rows, cols] = _chunk_cumsum(la, reverse=j * LANES >= GLA_KEY)

    pieces = [(j, r0) for j in range(2 * GLA_KEY // LANES) for r0 in range(0, tm, piece_rows)]
    step = 256
    chunks_left = [(PA_W + PB_W) // step]

    def emit_pieces():
        for _ in range(-(-len(pieces) // chunks_left[0])):
            decay_piece(*pieces.pop(0))
        chunks_left[0] -= 1

    for c0 in range(0, PA_W, step):
        r = _dot(xb, wa_ref[:, c0:c0 + step])
        if c0 < GLA_KEY:
            r = r * (GLA_DK ** -0.5)
        pa_ref[:, c0:c0 + step] = r.astype(BF16)
        emit_pieces()
    cos = cos_ref[...]
    sina = sina_ref[...]
    sinb = sinb_ref[...]
    rope0 = 2 * D_MODEL
    rope_end = rope0 + (SWA_HEADS + SWA_KV) * SWA_HD
    for c0 in range(0, PB_W, step):
        r = _dot(xb, wb_ref[:, c0:c0 + step])
        if c0 + step > rope0 and c0 < rope_end:
            parts = []
            for j in range(step // SWA_HD):
                t = r[:, j * SWA_HD:(j + 1) * SWA_HD]
                col = c0 + j * SWA_HD
                if rope0 <= col < rope_end:
                    t = (t * cos + pltpu.roll(t, SWA_HD - ROPE_DIM // 2, 1) * sina
                         + pltpu.roll(t, ROPE_DIM // 2, 1) * sinb)
                if rope0 <= col < rope0 + SWA_HEADS * SWA_HD:
                    t = t * (SWA_HD ** -0.5)
                parts.append(t)
            r = jnp.concatenate(parts, axis=1)
        pb_ref[:, c0:c0 + step] = r.astype(BF16)
        emit_pieces()
    assert not pieces


def _in_proj(x2d, wa, wl, w2f, bgaf, wb, cos, sina, sinb, seq):
    t = x2d.shape[0]
    tm = PROJ_TM
    nseq = seq // tm
    const = lambda i: (0, 0)
    return pl.pallas_call(
        _in_proj_kernel,
        grid=(t // tm,),
        in_specs=[
            pl.BlockSpec((tm, D_MODEL), lambda i: (i, 0)),
            pl.BlockSpec((D_MODEL, PA_W), const),
            pl.BlockSpec((D_MODEL, LANES), const),
            pl.BlockSpec((LANES, 2 * GLA_KEY), const),
            pl.BlockSpec((1, 2 * GLA_KEY), const),
            pl.BlockSpec((D_MODEL, PB_W), const),
            pl.BlockSpec((tm, LANES), lambda i: (i % nseq, 0)),
            pl.BlockSpec((tm, LANES), lambda i: (i % nseq, 0)),
            pl.BlockSpec((tm, LANES), lambda i: (i % nseq, 0)),
        ],
        out_specs=[
            pl.BlockSpec((tm, PA_W), lambda i: (i, 0)),
            pl.BlockSpec((tm, 2 * GLA_KEY), lambda i: (i, 0)),
            pl.BlockSpec((tm, PB_W), lambda i: (i, 0)),
        ],
        out_shape=[
            jax.ShapeDtypeStruct((t, PA_W), BF16),
            jax.ShapeDtypeStruct((t, 2 * GLA_KEY), F32),
            jax.ShapeDtypeStruct((t, PB_W), BF16),
        ],
        compiler_params=_cparams(("parallel",)),
        name="in_proj",
    )(x2d, wa, wl, w2f, bgaf, wb, cos, sina, sinb)


def _gla_kernel(q_ref, k_ref, v_ref, r_ref, b_ref, nw_ref, o_ref,
                state_ref, ofwd_ref, *group_scratch, ts, nt):
    ph = pl.program_id(2)
    t = pl.program_id(3)
    c = GLA_CHUNK
    gs = min(GLA_GROUP, ts)
    n_groups = ts // gs
    nc = gs // c

    @pl.when(t == 0)
    def _():
        state_ref[...] = jnp.zeros_like(state_ref)

    row = lax.broadcasted_iota(I32, (c, c), 0)
    col = lax.broadcasted_iota(I32, (c, c), 1)

    def run(forward):
        keep = (col <= row) if forward else (col > row)
        mid = c // 2 if forward else c // 2 - 1
        end = c - 1 if forward else 0
        off = pl.multiple_of((t if forward else nt - 1 - t) * ts, ts)
        group_order = range(n_groups) if forward else reversed(range(n_groups))
        chunk_order = list(range(nc)) if forward else list(reversed(range(nc)))
        st = state_ref[...]
        for g in group_order:
            sall_ref, ds_ref, dec_ref = group_scratch[3 * g:3 * g + 3]
            rows = slice(g * gs, (g + 1) * gs)
            q3 = q_ref[0, rows, :].reshape(nc, c, GLA_DK)
            k3 = k_ref[0, rows, :].reshape(nc, c, GLA_DK)
            v3 = v_ref[0, rows, :].reshape(nc, c, GLA_DV)
            b3 = b_ref[0, rows, :].reshape(nc, c, GLA_DK)
            b_mid = b3[:, mid:mid + 1]
            b_end = b3[:, end:end + 1]
            q_in = q3 * jnp.exp(b3 - b_mid).astype(BF16)
            k_in = k3 * jnp.exp(b_mid - b3).astype(BF16)
            k_st = k_in * jnp.exp(b_end - b_mid).astype(BF16)
            q_b = q_in * jnp.exp(b_mid).astype(BF16)
            dec_ref[...] = jnp.exp(b_end)
            scores = jnp.einsum("cid,cjd->cij", q_in, k_in, preferred_element_type=F32)
            scores = jnp.where(keep[None], scores, 0.0).astype(BF16)
            o_intra = jnp.einsum("cij,cjv->civ", scores, v3, preferred_element_type=F32)
            ds_ref[...] = jnp.einsum("cjv,cjd->cvd", v3, k_st, preferred_element_type=F32)
            for ci in chunk_order:
                sall_ref[ci] = st.astype(BF16)
                st = st * dec_ref[ci] + ds_ref[ci]
            o_inter = jnp.einsum("cid,cvd->civ", q_b, sall_ref[...], preferred_element_type=F32)
            o = (o_intra + o_inter).reshape(gs, GLA_DV)
            orows = pl.ds(pl.multiple_of(off + g * gs, gs), gs)
            if forward:
                ofwd_ref[orows, :] = o
            else:
                tot = o + ofwd_ref[orows, :]
                tot = tot * lax.rsqrt(jnp.mean(tot * tot, axis=-1, keepdims=True) + RMS_EPS)
                r = r_ref[0, rows, :]
                o_ref[0, rows, :] = (
                    tot * nw_ref[...] * (r * _sigmoid(r)).astype(F32)).astype(BF16)
        state_ref[...] = st

    @pl.when(ph == 0)
    def _():
        run(True)

    @pl.when(ph == 1)
    def _():
        run(False)


def _gla(pa, bdec, nw, bsz, seq):
    ts = min(GLA_TS, seq)
    nt = seq // ts
    gs = min(GLA_GROUP, ts)
    nc = gs // GLA_CHUNK
    group_scratch = []
    for _ in range(ts // gs):
        group_scratch += [pltpu.VMEM((nc, GLA_DV, GLA_DK), BF16),
                          pltpu.VMEM((nc, GLA_DV, GLA_DK), F32),
                          pltpu.VMEM((nc, 1, GLA_DK), F32)]
    pa3 = pa.reshape(bsz, seq, PA_W)
    bd3 = bdec.reshape(bsz, seq, 2 * GLA_KEY)
    tile = lambda p, t: t + p * (nt - 1 - 2 * t)
    wk, wv = GLA_DK, GLA_DV
    kq = GLA_KEY // wk
    kv = 2 * GLA_KEY // wv
    kr = kv + GLA_VAL // wv
    kern = functools.partial(_gla_kernel, ts=ts, nt=nt)
    return pl.pallas_call(
        kern,
        grid=(bsz, GLA_HEADS, 2, nt),
        in_specs=[
            pl.BlockSpec((1, ts, wk), lambda b, h, p, t: (b, tile(p, t), h)),
            pl.BlockSpec((1, ts, wk), lambda b, h, p, t: (b, tile(p, t), kq + h)),
            pl.BlockSpec((1, ts, wv), lambda b, h, p, t: (b, tile(p, t), kv + h)),
            pl.BlockSpec((1, ts, wv), lambda b, h, p, t: (b, nt - 1 - p * t, kr + h)),
            pl.BlockSpec((1, ts, wk), lambda b, h, p, t: (b, tile(p, t), p * kq + h)),
            pl.BlockSpec((1, wv), lambda b, h, p, t: (0, h)),
        ],
        out_specs=pl.BlockSpec((1, ts, wv), lambda b, h, p, t: (b, nt - 1 - p * t, h)),
        out_shape=jax.ShapeDtypeStruct((bsz, seq, GLA_VAL), BF16),
        scratch_shapes=[
            pltpu.VMEM((GLA_DV, GLA_DK), F32),
            pltpu.VMEM((seq, GLA_DV), F32),
        ] + group_scratch,
        compiler_params=_cparams(("parallel", "parallel", "arbitrary", "arbitrary")),
        name="gla",
    )(pa3, pa3, pa3, pa3, bd3, nw)


def _swa_kernel(sink_ref, q_ref, kl_ref, kc_ref, kr_ref, vl_ref, vc_ref, vr_ref, o_ref, *, tq):
    kvh = pl.program_id(1)
    n = pl.program_id(2)
    blk = SWA_WINDOW
    k_all = jnp.concatenate([kl_ref[0], kc_ref[0], kr_ref[0]], axis=0)
    v_all = jnp.concatenate([vl_ref[0], vc_ref[0], vr_ref[0]], axis=0)
    rows = SWA_GROUP * blk
    qi = lax.broadcasted_iota(I32, (rows, 3 * blk), 0) & (blk - 1)
    kj = lax.broadcasted_iota(I32, (rows, 3 * blk), 1) - blk
    band = jnp.where(jnp.abs(kj - qi) <= SWA_WINDOW, 0.0, -jnp.inf)
    head = lax.broadcasted_iota(I32, (rows, 1), 0) // blk
    sink = jnp.zeros((rows, 1), F32)
    for g in range(SWA_GROUP):
        sink = jnp.where(head == g, sink_ref[kvh * SWA_GROUP + g], sink)
    n_sub = tq // blk
    before = jnp.where(n == 0, -jnp.inf, 0.0)
    after = jnp.where(n == pl.num_programs(2) - 1, -jnp.inf, 0.0)
    for jq in range(n_sub):
        bias = band
        if jq == 0:
            bias = bias + jnp.where(kj < 0, before, 0.0)
        if jq == n_sub - 1:
            bias = bias + jnp.where(kj >= blk, after, 0.0)
        qs = q_ref[0, jq * blk:(jq + 1) * blk, :]
        q4 = jnp.concatenate([qs[:, g * SWA_HD:(g + 1) * SWA_HD] for g in range(SWA_GROUP)], axis=0)
        keys = k_all[jq * blk:(jq + 3) * blk]
        vals = v_all[jq * blk:(jq + 3) * blk]
        s = _dot_nt(q4, keys) + bias
        m = jnp.maximum(jnp.max(s, axis=-1, keepdims=True), sink)
        p = jnp.exp(s - m)
        denom = jnp.sum(p, axis=-1, keepdims=True) + jnp.exp(sink - m)
        o = _dot(p.astype(BF16), vals) * (1.0 / denom)
        for g in range(SWA_GROUP):
            o_ref[0, jq * blk:(jq + 1) * blk, g * SWA_HD:(g + 1) * SWA_HD] = (
                o[g * blk:(g + 1) * blk].astype(BF16))


def _swa(pb, sinks, bsz, seq):
    tq = min(SWA_TQ, seq)
    blk = SWA_WINDOW
    r = tq // blk
    nb = seq // blk
    pb3 = pb.reshape(bsz, seq, PB_W)
    gw = SWA_GROUP * SWA_HD
    cq = 2 * D_MODEL // gw
    ck = (2 * D_MODEL + SWA_HEADS * SWA_HD) // SWA_HD
    cv = ck + SWA_KV
    left = lambda n: jnp.maximum(n * r - 1, 0)
    right = lambda n: jnp.minimum(n * r + r, nb - 1)
    kern = functools.partial(_swa_kernel, tq=tq)
    grid_spec = pltpu.PrefetchScalarGridSpec(
        num_scalar_prefetch=1,
        grid=(bsz, SWA_KV, seq // tq),
        in_specs=[
            pl.BlockSpec((1, tq, gw), lambda b, h, n, s: (b, n, cq + h)),
            pl.BlockSpec((1, blk, SWA_HD), lambda b, h, n, s: (b, left(n), ck + h)),
            pl.BlockSpec((1, tq, SWA_HD), lambda b, h, n, s: (b, n, ck + h)),
            pl.BlockSpec((1, blk, SWA_HD), lambda b, h, n, s: (b, right(n), ck + h)),
            pl.BlockSpec((1, blk, SWA_HD), lambda b, h, n, s: (b, left(n), cv + h)),
            pl.BlockSpec((1, tq, SWA_HD), lambda b, h, n, s: (b, n, cv + h)),
            pl.BlockSpec((1, blk, SWA_HD), lambda b, h, n, s: (b, right(n), cv + h)),
        ],
        out_specs=pl.BlockSpec((1, tq, gw), lambda b, h, n, s: (b, n, h)),
    )
    return pl.pallas_call(
        kern,
        grid_spec=grid_spec,
        out_shape=jax.ShapeDtypeStruct((bsz, seq, SWA_HEADS * SWA_HD), BF16),
        compiler_params=_cparams(("parallel", "parallel", "parallel")),
        name="swa",
    )(sinks, pb3, pb3, pb3, pb3, pb3, pb3, pb3)


def _layer_norm(y, g, b):
    mu = jnp.mean(y, axis=-1, keepdims=True)
    d = y - mu
    var = jnp.mean(d * d, axis=-1, keepdims=True)
    return d * lax.rsqrt(var + LN_EPS) * g + b


def _merge_kernel(oa_ref, ob_ref, ga_ref, gb_ref, x_ref, wo_ref, g_ref, b_ref, rw_ref, rb_ref,
                  before_ref, xp_ref, gate_ref, route_ref, cnt_ref, run_ref, *, tm):
    @pl.when(pl.program_id(0) == 0)
    def _():
        run_ref[...] = jnp.zeros_like(run_ref)

    gm = MERGE_GROUP
    lane_i = lax.broadcasted_iota(I32, (gm, LANES), 1)
    lane = lane_i.astype(F32)

    def spread(cols):
        out = jnp.zeros((gm, LANES), F32)
        for kk, col in enumerate(cols):
            out = jnp.where(lane_i == kk, col, out)
        return out

    run = run_ref[...]
    for r0 in range(0, tm, gm):
        rows = slice(r0, r0 + gm)
        h = (_sigmoid(ga_ref[rows, :]) * oa_ref[rows, :]
             + _sigmoid(gb_ref[rows, :]) * ob_ref[rows, :])
        y = DEEPNORM_ALPHA * x_ref[rows, :] + _dot(h, wo_ref[...])
        x1 = _layer_norm(y, g_ref[...], b_ref[...])
        xp_ref[rows, :] = _pack_bf16_pair(x1[:, :HALF], x1[:, HALF:])

        x_hi, x_lo = _split2(x1)
        both = _dot(x_hi, rw_ref[...])
        logits = (both[:, :LANES] + both[:, LANES:] + _dot(x_lo, rw_ref[:, :LANES])
                  + rb_ref[...])
        work = logits
        vals, idxs, hots = [], [], []
        for _ in range(TOP_K):
            m = jnp.max(work, axis=-1, keepdims=True)
            sel = jnp.min(jnp.where(work == m, lane, float(LANES)), axis=-1, keepdims=True)
            hot = lane == sel
            vals.append(m)
            idxs.append(sel)
            hots.append(hot)
            work = jnp.where(hot, -jnp.inf, work)
        e = [jnp.exp(v - vals[0]) for v in vals]
        tot = e[0] + e[1] + e[2] + e[3]
        gate_ref[rows, :] = spread([ei / tot for ei in e])

        member = (hots[0] | hots[1] | hots[2] | hots[3])
        cum = _dot(before_ref[...], member.astype(BF16)) + run
        ranks = [jnp.sum(jnp.where(hot, cum, 0.0), axis=-1, keepdims=True) for hot in hots]
        route = spread(idxs + ranks).T
        route_ref[:, rows] = route[:2 * TOP_K].astype(I32)
        run = run + jnp.sum(member.astype(F32), axis=0, keepdims=True)
    run_ref[...] = run
    cnt_ref[...] = run


def _merge(oa, ob, pb, x2d, wo, g, b, rw, rb):
    t = x2d.shape[0]
    tm = MERGE_TM
    const = lambda i: (0, 0)
    kern = functools.partial(_merge_kernel, tm=tm)
    gm = MERGE_GROUP
    before = (jnp.arange(gm)[None, :] < jnp.arange(gm)[:, None]).astype(BF16)
    return pl.pallas_call(
        kern,
        grid=(t // tm,),
        in_specs=[
            pl.BlockSpec((tm, D_MODEL), lambda i: (i, 0)),
            pl.BlockSpec((tm, D_MODEL), lambda i: (i, 0)),
            pl.BlockSpec((tm, D_MODEL), lambda i: (i, 0)),
            pl.BlockSpec((tm, D_MODEL), lambda i: (i, 1)),
            pl.BlockSpec((tm, D_MODEL), lambda i: (i, 0)),
            pl.BlockSpec((D_MODEL, D_MODEL), const),
            pl.BlockSpec((1, D_MODEL), const),
            pl.BlockSpec((1, D_MODEL), const),
            pl.BlockSpec((D_MODEL, 2 * LANES), const),
            pl.BlockSpec((1, LANES), const),
            pl.BlockSpec((gm, gm), const),
        ],
        out_specs=[
            pl.BlockSpec((tm, HALF), lambda i: (i, 0)),
            pl.BlockSpec((tm, LANES), lambda i: (i, 0)),
            pl.BlockSpec((2 * TOP_K, tm), lambda i: (0, i)),
            pl.BlockSpec((1, LANES), const),
        ],
        out_shape=[
            jax.ShapeDtypeStruct((t, HALF), U32),
            jax.ShapeDtypeStruct((t, LANES), F32),
            jax.ShapeDtypeStruct((2 * TOP_K, t), I32),
            jax.ShapeDtypeStruct((1, LANES), F32),
        ],
        scratch_shapes=[pltpu.VMEM((1, LANES), F32)],
        compiler_params=_cparams(("arbitrary",)),
        name="merge",
    )(oa, ob, pb, pb, x2d, wo, g, b, rw, rb, before)


def _gather_rows(table, idx):
    n = idx.shape[0]
    d = table.shape[1]
    info = plsc.get_sparse_core_info()
    nw = info.num_cores * info.num_subcores
    per_w = n // nw
    n_pairs = per_w // (2 * SC_CHUNK)
    assert per_w * nw == n and n_pairs * 2 * SC_CHUNK == per_w
    mesh = plsc.VectorSubcoreMesh(core_axis_name="c", subcore_axis_name="s")

    @functools.partial(
        pl.kernel, mesh=mesh,
        out_type=jax.ShapeDtypeStruct((n, d), table.dtype),
        scratch_types=[
            pltpu.VMEM((per_w,), I32),
            pltpu.VMEM((SC_CHUNK, d), table.dtype),
            pltpu.VMEM((SC_CHUNK, d), table.dtype),
            pltpu.SemaphoreType.DMA, pltpu.SemaphoreType.DMA,
            pltpu.SemaphoreType.DMA, pltpu.SemaphoreType.DMA,
        ],
    )
    def k(table_hbm, idx_hbm, out_hbm, idx_v, rows_a, rows_b, gsem_a, gsem_b, wsem_a, wsem_b):
        wid = lax.axis_index("s") * info.num_cores + lax.axis_index("c")
        base = pl.multiple_of(wid * per_w, 8)
        pltpu.sync_copy(idx_hbm.at[pl.ds(base, per_w)], idx_v)

        def gather(c, buf, sem):
            rows = idx_v.at[pl.ds(pl.multiple_of(c * SC_CHUNK, 8), SC_CHUNK)]
            return pltpu.make_async_copy(table_hbm.at[rows], buf, sem)

        def write(c, buf, sem):
            dst = out_hbm.at[pl.ds(pl.multiple_of(base + c * SC_CHUNK, 8), SC_CHUNK)]
            return pltpu.make_async_copy(buf, dst, sem)

        gather(0, rows_a, gsem_a).start()

        @pl.loop(0, n_pairs)
        def _(j):
            c0 = 2 * j
            gather(c0, rows_a, gsem_a).wait()

            @pl.when(j > 0)
            def _():
                write(c0 - 1, rows_b, wsem_b).wait()

            gather(c0 + 1, rows_b, gsem_b).start()
            write(c0, rows_a, wsem_a).start()
            gather(c0 + 1, rows_b, gsem_b).wait()
            write(c0, rows_a, wsem_a).wait()

            @pl.when(j < n_pairs - 1)
            def _():
                gather(c0 + 2, rows_a, gsem_a).start()

            write(c0 + 1, rows_b, wsem_b).start()

        write(2 * n_pairs - 1, rows_b, wsem_b).wait()

    return k(table, idx)


def _scatter_rows(src, dest_km, n_rows):
    t, d = src.shape
    info = plsc.get_sparse_core_info()
    nw = info.num_cores * info.num_subcores
    per_w = t // nw
    assert per_w * nw == t and per_w % SC_CHUNK == 0
    n_chunks = per_w // SC_CHUNK
    mesh = plsc.VectorSubcoreMesh(core_axis_name="c", subcore_axis_name="s")

    @functools.partial(
        pl.kernel, mesh=mesh,
        out_type=jax.ShapeDtypeStruct((n_rows, d), src.dtype),
        scratch_types=[
            pltpu.VMEM((SC_CHUNK,), I32),
            pltpu.VMEM((SC_CHUNK, d), src.dtype),
            pltpu.SemaphoreType.DMA,
        ],
    )
    def k(src_hbm, dest_hbm, out_hbm, idx_v, rows_v, sem):
        wid = lax.axis_index("s") * info.num_cores + lax.axis_index("c")
        base = wid * per_w

        @pl.loop(0, n_chunks)
        def _(j):
            off = pl.multiple_of(base + j * SC_CHUNK, 8)
            pltpu.sync_copy(src_hbm.at[pl.ds(off, SC_CHUNK)], rows_v)
            for kk in range(TOP_K):
                pltpu.sync_copy(dest_hbm.at[pl.ds(pl.multiple_of(kk * t + off, 8), SC_CHUNK)], idx_v)
                pltpu.async_copy(rows_v, out_hbm.at[idx_v], sem).wait()

    return k(src, dest_km)


def _expert_kernel(be_ref, nused_ref, nvalid_ref, xs_ref, wgu_ref, bgu_ref, wd_ref, bd_ref, ys_ref,
                   wgu_bf, wd_bf):
    i = pl.program_id(0)

    @pl.when(i < nused_ref[0])
    def _():
        @pl.when((i == 0) | (be_ref[i] != be_ref[jnp.maximum(i - 1, 0)]))
        def _():
            rows = 256
            for r0 in range(0, D_MODEL, rows):
                wgu_bf[r0:r0 + rows, :] = wgu_ref[0, r0:r0 + rows, :].astype(BF16)
            for r0 in range(0, D_EXPERT, rows):
                wd_bf[r0:r0 + rows, :] = wd_ref[0, r0:r0 + rows, :].astype(BF16)

        def proj(x, col):
            return _dot(x, wgu_bf[:, col:col + MOE_HC]) + bgu_ref[0, :, col:col + MOE_HC]

        rowid = lax.broadcasted_iota(I32, (MOE_BM, 1), 0)
        words = jnp.where(rowid < nvalid_ref[i], xs_ref[...], jnp.uint32(0))
        lo, hi = _unpack_bf16_pair(words)
        x = jnp.concatenate([lo.astype(BF16), hi.astype(BF16)], axis=1)
        acts = []
        for c0 in range(0, D_EXPERT, MOE_HC):
            gate = jnp.minimum(proj(x, c0), SWIGLU_LIMIT)
            up = jnp.clip(proj(x, D_EXPERT + c0), -SWIGLU_LIMIT, SWIGLU_LIMIT)
            acts.append(((up + 1.0) * gate * _sigmoid(gate * SWIGLU_ALPHA)).astype(BF16))
        y = _dot(jnp.concatenate(acts, axis=1), wd_bf[...]) + bd_ref[0]
        ys_ref[...] = _pack_bf16_pair(y[:, :HALF], y[:, HALF:])


def _experts(xs, block_expert, n_used, n_valid, wgu, bgu, wd, bd):
    n_rows = xs.shape[0]
    bm = MOE_BM
    n_blocks = n_rows // bm
    rows = lambda i, be, nu, nv: (jnp.minimum(i, nu[0] - 1), 0)
    expert = lambda i, be, nu, nv: (be[i], 0, 0)
    grid_spec = pltpu.PrefetchScalarGridSpec(
        num_scalar_prefetch=3,
        grid=(n_blocks,),
        in_specs=[
            pl.BlockSpec((bm, HALF), rows),
            pl.BlockSpec((1, D_MODEL, 2 * D_EXPERT), expert),
            pl.BlockSpec((1, 1, 2 * D_EXPERT), expert),
            pl.BlockSpec((1, D_EXPERT, D_MODEL), expert),
            pl.BlockSpec((1, 1, D_MODEL), expert),
        ],
        out_specs=pl.BlockSpec((bm, HALF), rows),
        scratch_shapes=[
            pltpu.VMEM((D_MODEL, 2 * D_EXPERT), BF16),
            pltpu.VMEM((D_EXPERT, D_MODEL), BF16),
        ],
    )
    return pl.pallas_call(
        _expert_kernel,
        grid_spec=grid_spec,
        out_shape=jax.ShapeDtypeStruct((n_rows, HALF), U32),
        compiler_params=_cparams(("arbitrary",)),
        name="experts",
    )(block_expert, n_used, n_valid, xs, wgu, bgu, wd, bd)


def _combine_kernel(xp_ref, yg_ref, gate_ref, g_ref, b_ref, o_ref):
    gates = gate_ref[...]
    f_lo = None
    f_hi = None
    for kk in range(TOP_K):
        lo, hi = _unpack_bf16_pair(yg_ref[kk])
        w = gates[:, kk:kk + 1]
        f_lo = lo * w if f_lo is None else f_lo + lo * w
        f_hi = hi * w if f_hi is None else f_hi + hi * w
    x_lo, x_hi = _unpack_bf16_pair(xp_ref[...])
    y_lo = DEEPNORM_ALPHA * x_lo + f_lo
    y_hi = DEEPNORM_ALPHA * x_hi + f_hi
    mu = (jnp.sum(y_lo, axis=-1, keepdims=True) + jnp.sum(y_hi, axis=-1, keepdims=True)) / D_MODEL
    d_lo = y_lo - mu
    d_hi = y_hi - mu
    var = (jnp.sum(d_lo * d_lo, axis=-1, keepdims=True)
           + jnp.sum(d_hi * d_hi, axis=-1, keepdims=True)) / D_MODEL
    inv = lax.rsqrt(var + LN_EPS)
    g = g_ref[...]
    b = b_ref[...]
    o_ref[:, :HALF] = d_lo * inv * g[:, :HALF] + b[:, :HALF]
    o_ref[:, HALF:] = d_hi * inv * g[:, HALF:] + b[:, HALF:]


def _combine(xp, yg, gates, g, b):
    t = xp.shape[0]
    tm = MERGE_TM
    const = lambda i: (0, 0)
    return pl.pallas_call(
        _combine_kernel,
        grid=(t // tm,),
        in_specs=[
            pl.BlockSpec((tm, HALF), lambda i: (i, 0)),
            pl.BlockSpec((TOP_K, tm, HALF), lambda i: (0, i, 0)),
            pl.BlockSpec((tm, LANES), lambda i: (i, 0)),
            pl.BlockSpec((1, D_MODEL), const),
            pl.BlockSpec((1, D_MODEL), const),
        ],
        out_specs=pl.BlockSpec((tm, D_MODEL), lambda i: (i, 0)),
        out_shape=jax.ShapeDtypeStruct((t, D_MODEL), F32),
        compiler_params=_cparams(("parallel",)),
        name="combine",
    )(xp, yg, gates, g, b)


def _rope_tables(seq):
    half = ROPE_DIM // 2
    pos = jnp.arange(seq, dtype=F32)
    inv_freq = ROPE_THETA ** (-jnp.arange(0, ROPE_DIM, 2, dtype=F32) / ROPE_DIM)
    ang = pos[:, None] * inv_freq[None, :]
    cos, sin = jnp.cos(ang), jnp.sin(ang)
    zeros = jnp.zeros((seq, LANES - ROPE_DIM), F32)
    cos_f = jnp.concatenate([cos, cos, jnp.ones((seq, LANES - ROPE_DIM), F32)], axis=1)
    sin_a = jnp.concatenate([-sin, jnp.zeros((seq, half), F32), zeros], axis=1)
    sin_b = jnp.concatenate([jnp.zeros((seq, half), F32), sin, zeros], axis=1)
    return cos_f, sin_a, sin_b


def _prep_weights(w_in, gla_gate_w2, gla_gate_b, gla_norm_w, w_o, ln1_g, ln1_b, router_w, router_b,
                  w_gate_up, b_gate_up, w_down, b_down, ln2_g, ln2_b):
    sizes = (GLA_KEY, GLA_KEY, GLA_VAL, GLA_VAL, 2 * GLA_RANK, SWA_HEADS * SWA_HD,
             SWA_KV * SWA_HD, SWA_KV * SWA_HD, D_MODEL, D_MODEL)
    cols, acc = [], 0
    for s in sizes:
        cols.append(w_in[:, acc:acc + s])
        acc += s
    gq, gk, gv, gr, glr, sq, sk, sv, ga, gb = cols
    wa = jnp.concatenate([gq, gk, gv, gr], axis=1).astype(BF16)
    wl = jnp.pad(glr, ((0, 0), (0, LANES - 2 * GLA_RANK))).astype(BF16)
    wb = jnp.concatenate([ga, gb, sq, sk, sv], axis=1).astype(BF16)
    w2f = jnp.zeros((LANES, 2 * GLA_KEY), F32)
    w2f = w2f.at[:GLA_RANK, :GLA_KEY].set(gla_gate_w2[0])
    w2f = w2f.at[GLA_RANK:2 * GLA_RANK, GLA_KEY:].set(gla_gate_w2[1]).astype(BF16)
    rw_hi, rw_lo = _split2(jnp.pad(router_w, ((0, 0), (0, LANES - N_EXPERTS))))
    rw = jnp.concatenate([rw_hi, rw_lo], axis=1)
    rb = jnp.concatenate([router_b, jnp.full((LANES - N_EXPERTS,), -jnp.inf, F32)]).reshape(1, LANES)
    return dict(
        wa=wa, wl=wl, wb=wb, w2f=w2f, bgaf=gla_gate_b.reshape(1, 2 * GLA_KEY),
        nw=gla_norm_w.reshape(1, GLA_VAL), wo=w_o.astype(BF16),
        ln1_g=ln1_g.reshape(1, D_MODEL), ln1_b=ln1_b.reshape(1, D_MODEL), rw=rw, rb=rb,
        wgu=w_gate_up, bgu=b_gate_up.reshape(N_EXPERTS, 1, 2 * D_EXPERT),
        wd=w_down, bd=b_down.reshape(N_EXPERTS, 1, D_MODEL),
        ln2_g=ln2_g.reshape(1, D_MODEL), ln2_b=ln2_b.reshape(1, D_MODEL))


def _routing_plan(route, counts):
    idx, rank = route[:TOP_K], route[TOP_K:]
    t = route.shape[1]
    bm = MOE_BM
    n_rows = t * TOP_K + N_EXPERTS * bm
    n_blocks = n_rows // bm
    counts = counts[0, :N_EXPERTS].astype(I32)
    padded = (counts + bm - 1) // bm * bm
    pad_ends = jnp.cumsum(padded)
    pad_starts = pad_ends - padded
    dest = rank
    for ex in range(N_EXPERTS):
        dest = dest + jnp.where(idx == ex, pad_starts[ex], 0)
    block_start = jnp.arange(n_blocks, dtype=I32) * bm
    block_expert = jnp.minimum(
        jnp.sum(block_start[:, None] >= pad_ends[None, :], axis=1), N_EXPERTS - 1).astype(I32)
    mine = block_expert[:, None] == jnp.arange(N_EXPERTS, dtype=I32)[None, :]
    seg_end = jnp.sum(jnp.where(mine, (pad_starts + counts)[None, :], 0), axis=1)
    n_valid = jnp.clip(seg_end - block_start, 0, bm).astype(I32)
    n_used = (pad_ends[-1] // bm).astype(I32).reshape(1)
    return dest.reshape(-1), n_rows, block_expert, n_used, n_valid


def _trunk(x, w, sinks, gather_rows, scatter_rows):
    bsz, seq, _ = x.shape
    t = bsz * seq
    x2d = x.reshape(t, D_MODEL)
    cos_f, sin_a, sin_b = _rope_tables(seq)
    pa, bdec, pb = _in_proj(x2d, w["wa"], w["wl"], w["w2f"], w["bgaf"], w["wb"],
                            cos_f, sin_a, sin_b, seq)
    oa = _gla(pa, bdec, w["nw"], bsz, seq).reshape(t, GLA_VAL)
    ob = _swa(pb, sinks, bsz, seq).reshape(t, SWA_HEADS * SWA_HD)
    xp, gates, route, counts = _merge(
        oa, ob, pb, x2d, w["wo"], w["ln1_g"], w["ln1_b"], w["rw"], w["rb"])
    dest_km, n_rows, block_expert, n_used, n_valid = _routing_plan(route, counts)
    xs = scatter_rows(xp, dest_km, n_rows)
    ys = _experts(xs, block_expert, n_used, n_valid, w["wgu"], w["bgu"], w["wd"], w["bd"])
    yg = gather_rows(ys, dest_km).reshape(TOP_K, t, HALF)
    out = _combine(xp, yg, gates, w["ln2_g"], w["ln2_b"])
    return out.reshape(bsz, seq, D_MODEL)


def _forward(x_prompt, x_sample, w_in, gla_gate_w2, gla_gate_b, gla_norm_w, swa_sinks, w_o, ln1_g,
             ln1_b, router_w, router_b, w_gate_up, b_gate_up, w_down, b_down, ln2_g, ln2_b):
    w = _prep_weights(w_in[0], gla_gate_w2[0], gla_gate_b[0], gla_norm_w[0], w_o[0], ln1_g[0],
                      ln1_b[0], router_w[0], router_b[0], w_gate_up[0], b_gate_up[0], w_down[0],
                      b_down[0], ln2_g[0], ln2_b[0])
    sinks = swa_sinks[0].astype(F32)
    return (_trunk(x_prompt, w, sinks, _gather_rows, _scatter_rows),
            _trunk(x_sample, w, sinks, _gather_rows, _scatter_rows))


def kernel(x_prompt, x_sample, w_in, gla_gate_w2, gla_gate_b, gla_norm_w, swa_sinks, w_o, ln1_g, ln1_b, router_w, router_b, w_gate_up, b_gate_up, w_down, b_down, ln2_g, ln2_b):
    return _forward(x_prompt, x_sample, w_in, gla_gate_w2, gla_gate_b, gla_norm_w, swa_sinks, w_o,
                    ln1_g, ln1_b, router_w, router_b, w_gate_up, b_gate_up, w_down, b_down,
                    ln2_g, ln2_b)
```

```python
import functools

import jax
import jax.numpy as jnp
from jax import lax
from jax.experimental import pallas as pl
from jax.experimental.pallas import tpu as pltpu
from jax.experimental.pallas import tpu_sc as plsc

F32 = jnp.float32
BF16 = jnp.bfloat16
I32 = jnp.int32
U32 = jnp.uint32

D_MODEL = 1024
GLA_HEADS = 4
GLA_DK = 128
GLA_DV = 256
GLA_KEY = GLA_HEADS * GLA_DK
GLA_VAL = GLA_HEADS * GLA_DV
GLA_RANK = 16
GLA_NORMALIZER = 16.0
GLA_CHUNK = 64
SWA_HEADS = 8
SWA_KV = 2
SWA_GROUP = SWA_HEADS // SWA_KV
SWA_HD = 128
SWA_WINDOW = 128
ROPE_THETA = 500000.0
ROPE_DIM = SWA_HD // 4
N_EXPERTS = 32
TOP_K = 4
D_EXPERT = D_MODEL
SWIGLU_LIMIT = 7.0
SWIGLU_ALPHA = 1.702
DEEPNORM_ALPHA = 2.0 ** 0.25
LN_EPS = 1e-5
RMS_EPS = 1e-6

LANES = 128
PA_W = 2 * GLA_KEY + 2 * GLA_VAL
PB_W = 2 * D_MODEL + SWA_HEADS * SWA_HD + 2 * SWA_KV * SWA_HD
HALF = D_MODEL // 2

PROJ_TM = 512
MERGE_TM = 1024
MERGE_GROUP = 512
GLA_TS = 4096
GLA_GROUP = 512
SWA_TQ = 2048
MOE_BM = 1024
MOE_HC = 256
SC_CHUNK = 64
VMEM_LIMIT = 56 * 1024 * 1024


def _cparams(sem):
    return pltpu.CompilerParams(dimension_semantics=sem, vmem_limit_bytes=VMEM_LIMIT)


def _split2(a):
    hi = a.astype(BF16)
    lo = (a - hi.astype(F32)).astype(BF16)
    return hi, lo


def _dot(a, b):
    return jnp.dot(a, b, preferred_element_type=F32)


def _dot_nt(a, b):
    return lax.dot_general(a, b, (((1,), (1,)), ((), ())), preferred_element_type=F32)


def _sigmoid(x):
    return 0.5 * jnp.tanh(0.5 * x) + 0.5


def _pack_bf16_pair(lo_f32, hi_f32):
    lo = lax.bitcast_convert_type(lo_f32.astype(BF16).astype(F32), U32)
    hi = lax.bitcast_convert_type(hi_f32.astype(BF16).astype(F32), U32)
    return (hi & jnp.uint32(0xFFFF0000)) | (lo >> 16)


def _unpack_bf16_pair(w):
    lo = lax.bitcast_convert_type(w << 16, F32)
    hi = lax.bitcast_convert_type(w & jnp.uint32(0xFFFF0000), F32)
    return lo, hi


def _chunk_cumsum(x, reverse):
    rows = x.shape[0]
    pos = lax.broadcasted_iota(I32, x.shape, 0) & (GLA_CHUNK - 1)
    s = 1
    while s < GLA_CHUNK:
        if reverse:
            x = x + jnp.where(pos < GLA_CHUNK - s, pltpu.roll(x, rows - s, 0), 0.0)
        else:
            x = x + jnp.where(pos >= s, pltpu.roll(x, s, 0), 0.0)
        s *= 2
    return x


def _in_proj_kernel(x_ref, wa_ref, wl_ref, w2_ref, bga_ref, wb_ref, cos_ref, sina_ref, sinb_ref,
                    pa_ref, bd_ref, pb_ref):
    xb = x_ref[...].astype(BF16)
    lrb = _dot(xb, wl_ref[...]).astype(BF16)

    tm = x_ref.shape[0]
    piece_rows = 128

    def decay_piece(j, r0):
        cols = slice(j * LANES, (j + 1) * LANES)
        rows = slice(r0, r0 + piece_rows)
        z = _dot(lrb[rows], w2_ref[:, cols]) + bga_ref[:, cols]
        la = (jnp.minimum(z, 0.0) - jnp.log(1.0 + jnp.exp(-jnp.abs(z)))) * (1.0 / GLA_NORMALIZER)
        bd_ref[rows, cols] = _chunk_cumsum(la, reverse=j * LANES >= GLA_KEY)

    pieces = [(j, r0) for j in range(2 * GLA_KEY // LANES) for r0 in range(0, tm, piece_rows)]
    step = 256
    chunks_left = [(PA_W + PB_W) // step]

    def emit_pieces():
        for _ in range(-(-len(pieces) // chunks_left[0])):
            decay_piece(*pieces.pop(0))
        chunks_left[0] -= 1

    for c0 in range(0, PA_W, step):
        r = _dot(xb, wa_ref[:, c0:c0 + step])
        if c0 < GLA_KEY:
            r = r * (GLA_DK ** -0.5)
        pa_ref[:, c0:c0 + step] = r.astype(BF16)
        emit_pieces()
    cos = cos_ref[...]
    sina = sina_ref[...]
    sinb = sinb_ref[...]
    rope0 = 2 * D_MODEL
    rope_end = rope0 + (SWA_HEADS + SWA_KV) * SWA_HD
    for c0 in range(0, PB_W, step):
        r = _dot(xb, wb_ref[:, c0:c0 + step])
        if c0 + step > rope0 and c0 < rope_end:
            parts = []
            for j in range(step // SWA_HD):
                t = r[:, j * SWA_HD:(j + 1) * SWA_HD]
                col = c0 + j * SWA_HD
                if rope0 <= col < rope_end:
                    t = (t * cos + pltpu.roll(t, SWA_HD - ROPE_DIM // 2, 1) * sina
                         + pltpu.roll(t, ROPE_DIM // 2, 1) * sinb)
                if rope0 <= col < rope0 + SWA_HEADS * SWA_HD:
                    t = t * (SWA_HD ** -0.5)
                parts.append(t)
            r = jnp.concatenate(parts, axis=1)
        pb_ref[:, c0:c0 + step] = r.astype(BF16)
        emit_pieces()
    assert not pieces


def _in_proj(x2d, wa, wl, w2f, bgaf, wb, cos, sina, sinb, seq):
    t = x2d.shape[0]
    tm = PROJ_TM
    nseq = seq // tm
    const = lambda i: (0, 0)
    return pl.pallas_call(
        _in_proj_kernel,
        grid=(t // tm,),
        in_specs=[
            pl.BlockSpec((tm, D_MODEL), lambda i: (i, 0)),
            pl.BlockSpec((D_MODEL, PA_W), const),
            pl.BlockSpec((D_MODEL, LANES), const),
            pl.BlockSpec((LANES, 2 * GLA_KEY), const),
            pl.BlockSpec((1, 2 * GLA_KEY), const),
            pl.BlockSpec((D_MODEL, PB_W), const),
            pl.BlockSpec((tm, LANES), lambda i: (i % nseq, 0)),
            pl.BlockSpec((tm, LANES), lambda i: (i % nseq, 0)),
            pl.BlockSpec((tm, LANES), lambda i: (i % nseq, 0)),
        ],
        out_specs=[
            pl.BlockSpec((tm, PA_W), lambda i: (i, 0)),
            pl.BlockSpec((tm, 2 * GLA_KEY), lambda i: (i, 0)),
            pl.BlockSpec((tm, PB_W), lambda i: (i, 0)),
        ],
        out_shape=[
            jax.ShapeDtypeStruct((t, PA_W), BF16),
            jax.ShapeDtypeStruct((t, 2 * GLA_KEY), F32),
            jax.ShapeDtypeStruct((t, PB_W), BF16),
        ],
        compiler_params=_cparams(("parallel",)),
        name="in_proj",
    )(x2d, wa, wl, w2f, bgaf, wb, cos, sina, sinb)


def _gla_kernel(q_ref, k_ref, v_ref, r_ref, b_ref, nw_ref, o_ref,
                state_ref, ofwd_ref, *group_scratch, ts, nt):
    ph = pl.program_id(2)
    t = pl.program_id(3)
    c = GLA_CHUNK
    gs = min(GLA_GROUP, ts)
    n_groups = ts // gs
    nc = gs // c

    @pl.when(t == 0)
    def _():
        state_ref[...] = jnp.zeros_like(state_ref)

    row = lax.broadcasted_iota(I32, (c, c), 0)
    col = lax.broadcasted_iota(I32, (c, c), 1)

    def run(forward):
        keep = (col <= row) if forward else (col > row)
        mid = c // 2 if forward else c // 2 - 1
        end = c - 1 if forward else 0
        off = pl.multiple_of((t if forward else nt - 1 - t) * ts, ts)
        group_order = range(n_groups) if forward else reversed(range(n_groups))
        chunk_order = list(range(nc)) if forward else list(reversed(range(nc)))
        st = state_ref[...]
        for g in group_order:
            sall_ref, ds_ref, dec_ref = group_scratch[3 * g:3 * g + 3]
            rows = slice(g * gs, (g + 1) * gs)
            q3 = q_ref[0, rows, :].reshape(nc, c, GLA_DK)
            k3 = k_ref[0, rows, :].reshape(nc, c, GLA_DK)
            v3 = v_ref[0, rows, :].reshape(nc, c, GLA_DV)
            b3 = b_ref[0, rows, :].reshape(nc, c, GLA_DK)
            b_mid = b3[:, mid:mid + 1]
            b_end = b3[:, end:end + 1]
            q_in = q3 * jnp.exp(b3 - b_mid).astype(BF16)
            k_in = k3 * jnp.exp(b_mid - b3).astype(BF16)
            k_st = k_in * jnp.exp(b_end - b_mid).astype(BF16)
            q_b = q_in * jnp.exp(b_mid).astype(BF16)
            dec_ref[...] = jnp.exp(b_end)
            scores = jnp.einsum("cid,cjd->cij", q_in, k_in, preferred_element_type=F32)
            scores = jnp.where(keep[None], scores, 0.0).astype(BF16)
            o_intra = jnp.einsum("cij,cjv->civ", scores, v3, preferred_element_type=F32)
            ds_ref[...] = jnp.einsum("cjv,cjd->cvd", v3, k_st, preferred_element_type=F32)
            for ci in chunk_order:
                sall_ref[ci] = st.astype(BF16)
                st = st * dec_ref[ci] + ds_ref[ci]
            o_inter = jnp.einsum("cid,cvd->civ", q_b, sall_ref[...], preferred_element_type=F32)
            o = (o_intra + o_inter).reshape(gs, GLA_DV)
            orows = pl.ds(pl.multiple_of(off + g * gs, gs), gs)
            if forward:
                ofwd_ref[orows, :] = o
            else:
                tot = o + ofwd_ref[orows, :]
                tot = tot * lax.rsqrt(jnp.mean(tot * tot, axis=-1, keepdims=True) + RMS_EPS)
                r = r_ref[0, rows, :]
                o_ref[0, rows, :] = (
                    tot * nw_ref[...] * (r * _sigmoid(r)).astype(F32)).astype(BF16)
        state_ref[...] = st

    @pl.when(ph == 0)
    def _():
        run(True)

    @pl.when(ph == 1)
    def _():
        run(False)


def _gla(pa, bdec, nw, bsz, seq):
    ts = min(GLA_TS, seq)
    nt = seq // ts
    gs = min(GLA_GROUP, ts)
    nc = gs // GLA_CHUNK
    group_scratch = []
    for _ in range(ts // gs):
        group_scratch += [pltpu.VMEM((nc, GLA_DV, GLA_DK), BF16),
                          pltpu.VMEM((nc, GLA_DV, GLA_DK), F32),
                          pltpu.VMEM((nc, 1, GLA_DK), F32)]
    pa3 = pa.reshape(bsz, seq, PA_W)
    bd3 = bdec.reshape(bsz, seq, 2 * GLA_KEY)
    tile = lambda p, t: t + p * (nt - 1 - 2 * t)
    wk, wv = GLA_DK, GLA_DV
    kq = GLA_KEY // wk
    kv = 2 * GLA_KEY // wv
    kr = kv + GLA_VAL // wv
    kern = functools.partial(_gla_kernel, ts=ts, nt=nt)
    return pl.pallas_call(
        kern,
        grid=(bsz, GLA_HEADS, 2, nt),
        in_specs=[
            pl.BlockSpec((1, ts, wk), lambda b, h, p, t: (b, tile(p, t), h)),
            pl.BlockSpec((1, ts, wk), lambda b, h, p, t: (b, tile(p, t), kq + h)),
            pl.BlockSpec((1, ts, wv), lambda b, h, p, t: (b, tile(p, t), kv + h)),
            pl.BlockSpec((1, ts, wv), lambda b, h, p, t: (b, nt - 1 - p * t, kr + h)),
            pl.BlockSpec((1, ts, wk), lambda b, h, p, t: (b, tile(p, t), p * kq + h)),
            pl.BlockSpec((1, wv), lambda b, h, p, t: (0, h)),
        ],
        out_specs=pl.BlockSpec((1, ts, wv), lambda b, h, p, t: (b, nt - 1 - p * t, h)),
        out_shape=jax.ShapeDtypeStruct((bsz, seq, GLA_VAL), BF16),
        scratch_shapes=[
            pltpu.VMEM((GLA_DV, GLA_DK), F32),
            pltpu.VMEM((seq, GLA_DV), F32),
        ] + group_scratch,
        compiler_params=_cparams(("parallel", "parallel", "arbitrary", "arbitrary")),
        name="gla",
    )(pa3, pa3, pa3, pa3, bd3, nw)


def _swa_kernel(sink_ref, q_ref, kl_ref, kc_ref, kr_ref, vl_ref, vc_ref, vr_ref, o_ref, *, tq):
    kvh = pl.program_id(1)
    n = pl.program_id(2)
    blk = SWA_WINDOW
    k_all = jnp.concatenate([kl_ref[0], kc_ref[0], kr_ref[0]], axis=0)
    v_all = jnp.concatenate([vl_ref[0], vc_ref[0], vr_ref[0]], axis=0)
    rows = SWA_GROUP * blk
    qi = lax.broadcasted_iota(I32, (rows, 3 * blk), 0) & (blk - 1)
    kj = lax.broadcasted_iota(I32, (rows, 3 * blk), 1) - blk
    band = jnp.where(jnp.abs(kj - qi) <= SWA_WINDOW, 0.0, -jnp.inf)
    head = lax.broadcasted_iota(I32, (rows, 1), 0) // blk
    sink = jnp.zeros((rows, 1), F32)
    for g in range(SWA_GROUP):
        sink = jnp.where(head == g, sink_ref[kvh * SWA_GROUP + g], sink)
    n_sub = tq // blk
    before = jnp.where(n == 0, -jnp.inf, 0.0)
    after = jnp.where(n == pl.num_programs(2) - 1, -jnp.inf, 0.0)
    for jq in range(n_sub):
        bias = band
        if jq == 0:
            bias = bias + jnp.where(kj < 0, before, 0.0)
        if jq == n_sub - 1:
            bias = bias + jnp.where(kj >= blk, after, 0.0)
        qs = q_ref[0, jq * blk:(jq + 1) * blk, :]
        q4 = jnp.concatenate([qs[:, g * SWA_HD:(g + 1) * SWA_HD] for g in range(SWA_GROUP)], axis=0)
        keys = k_all[jq * blk:(jq + 3) * blk]
        vals = v_all[jq * blk:(jq + 3) * blk]
        s = _dot_nt(q4, keys) + bias
        m = jnp.maximum(jnp.max(s, axis=-1, keepdims=True), sink)
        p = jnp.exp(s - m)
        denom = jnp.sum(p, axis=-1, keepdims=True) + jnp.exp(sink - m)
        o = _dot(p.astype(BF16), vals) * (1.0 / denom)
        for g in range(SWA_GROUP):
            o_ref[0, jq * blk:(jq + 1) * blk, g * SWA_HD:(g + 1) * SWA_HD] = (
                o[g * blk:(g + 1) * blk].astype(BF16))


def _swa(pb, sinks, bsz, seq):
    tq = min(SWA_TQ, seq)
    blk = SWA_WINDOW
    r = tq // blk
    nb = seq // blk
    pb3 = pb.reshape(bsz, seq, PB_W)
    gw = SWA_GROUP * SWA_HD
    cq = 2 * D_MODEL // gw
    ck = (2 * D_MODEL + SWA_HEADS * SWA_HD) // SWA_HD
    cv = ck + SWA_KV
    left = lambda n: jnp.maximum(n * r - 1, 0)
    right = lambda n: jnp.minimum(n * r + r, nb - 1)
    kern = functools.partial(_swa_kernel, tq=tq)
    grid_spec = pltpu.PrefetchScalarGridSpec(
        num_scalar_prefetch=1,
        grid=(bsz, SWA_KV, seq // tq),
        in_specs=[
            pl.BlockSpec((1, tq, gw), lambda b, h, n, s: (b, n, cq + h)),
            pl.BlockSpec((1, blk, SWA_HD), lambda b, h, n, s: (b, left(n), ck + h)),
            pl.BlockSpec((1, tq, SWA_HD), lambda b, h, n, s: (b, n, ck + h)),
            pl.BlockSpec((1, blk, SWA_HD), lambda b, h, n, s: (b, right(n), ck + h)),
            pl.BlockSpec((1, blk, SWA_HD), lambda b, h, n, s: (b, left(n), cv + h)),
            pl.BlockSpec((1, tq, SWA_HD), lambda b, h, n, s: (b, n, cv + h)),
            pl.BlockSpec((1, blk, SWA_HD), lambda b, h, n, s: (b, right(n), cv + h)),
        ],
        out_specs=pl.BlockSpec((1, tq, gw), lambda b, h, n, s: (b, n, h)),
    )
    return pl.pallas_call(
        kern,
        grid_spec=grid_spec,
        out_shape=jax.ShapeDtypeStruct((bsz, seq, SWA_HEADS * SWA_HD), BF16),
        compiler_params=_cparams(("parallel", "parallel", "parallel")),
        name="swa",
    )(sinks, pb3, pb3, pb3, pb3, pb3, pb3, pb3)


def _layer_norm(y, g, b):
    mu = jnp.mean(y, axis=-1, keepdims=True)
    d = y - mu
    var = jnp.mean(d * d, axis=-1, keepdims=True)
    return d * lax.rsqrt(var + LN_EPS) * g + b


def _merge_kernel(oa_ref, ob_ref, ga_ref, gb_ref, x_ref, wo_ref, g_ref, b_ref, rw_ref, rb_ref,
                  before_ref, xp_ref, gate_ref, route_ref, cnt_ref, run_ref, *, tm):
    @pl.when(pl.program_id(0) == 0)
    def _():
        run_ref[...] = jnp.zeros_like(run_ref)

    gm = MERGE_GROUP
    lane_i = lax.broadcasted_iota(I32, (gm, LANES), 1)
    lane = lane_i.astype(F32)

    def spread(cols):
        out = jnp.zeros((gm, LANES), F32)
        for kk, col in enumerate(cols):
            out = jnp.where(lane_i == kk, col, out)
        return out

    run = run_ref[...]
    for r0 in range(0, tm, gm):
        rows = slice(r0, r0 + gm)
        h = (_sigmoid(ga_ref[rows, :]) * oa_ref[rows, :]
             + _sigmoid(gb_ref[rows, :]) * ob_ref[rows, :])
        y = DEEPNORM_ALPHA * x_ref[rows, :] + _dot(h, wo_ref[...])
        x1 = _layer_norm(y, g_ref[...], b_ref[...])
        xp_ref[rows, :] = _pack_bf16_pair(x1[:, :HALF], x1[:, HALF:])

        x_hi, x_lo = _split2(x1)
        both = _dot(x_hi, rw_ref[...])
        logits = (both[:, :LANES] + both[:, LANES:] + _dot(x_lo, rw_ref[:, :LANES])
                  + rb_ref[...])
        work = logits
        vals, idxs, hots = [], [], []
        for _ in range(TOP_K):
            m = jnp.max(work, axis=-1, keepdims=True)
            sel = jnp.min(jnp.where(work == m, lane, float(LANES)), axis=-1, keepdims=True)
            hot = lane == sel
            vals.append(m)
            idxs.append(sel)
            hots.append(hot)
            work = jnp.where(hot, -jnp.inf, work)
        e = [jnp.exp(v - vals[0]) for v in vals]
        tot = e[0] + e[1] + e[2] + e[3]
        gate_ref[rows, :] = spread([ei / tot for ei in e])

        member = (hots[0] | hots[1] | hots[2] | hots[3])
        cum = _dot(before_ref[...], member.astype(BF16)) + run
        ranks = [jnp.sum(jnp.where(hot, cum, 0.0), axis=-1, keepdims=True) for hot in hots]
        route = spread(idxs + ranks).T
        route_ref[:, rows] = route[:2 * TOP_K].astype(I32)
        run = run + jnp.sum(member.astype(F32), axis=0, keepdims=True)
    run_ref[...] = run
    cnt_ref[...] = run


def _merge(oa, ob, pb, x2d, wo, g, b, rw, rb):
    t = x2d.shape[0]
    tm = MERGE_TM
    const = lambda i: (0, 0)
    kern = functools.partial(_merge_kernel, tm=tm)
    gm = MERGE_GROUP
    before = (jnp.arange(gm)[None, :] < jnp.arange(gm)[:, None]).astype(BF16)
    return pl.pallas_call(
        kern,
        grid=(t // tm,),
        in_specs=[
            pl.BlockSpec((tm, D_MODEL), lambda i: (i, 0)),
            pl.BlockSpec((tm, D_MODEL), lambda i: (i, 0)),
            pl.BlockSpec((tm, D_MODEL), lambda i: (i, 0)),
            pl.BlockSpec((tm, D_MODEL), lambda i: (i, 1)),
            pl.BlockSpec((tm, D_MODEL), lambda i: (i, 0)),
            pl.BlockSpec((D_MODEL, D_MODEL), const),
            pl.BlockSpec((1, D_MODEL), const),
            pl.BlockSpec((1, D_MODEL), const),
            pl.BlockSpec((D_MODEL, 2 * LANES), const),
            pl.BlockSpec((1, LANES), const),
            pl.BlockSpec((gm, gm), const),
        ],
        out_specs=[
            pl.BlockSpec((tm, HALF), lambda i: (i, 0)),
            pl.BlockSpec((tm, LANES), lambda i: (i, 0)),
            pl.BlockSpec((2 * TOP_K, tm), lambda i: (0, i)),
            pl.BlockSpec((1, LANES), const),
        ],
        out_shape=[
            jax.ShapeDtypeStruct((t, HALF), U32),
            jax.ShapeDtypeStruct((t, LANES), F32),
            jax.ShapeDtypeStruct((2 * TOP_K, t), I32),
            jax.ShapeDtypeStruct((1, LANES), F32),
        ],
        scratch_shapes=[pltpu.VMEM((1, LANES), F32)],
        compiler_params=_cparams(("arbitrary",)),
        name="merge",
    )(oa, ob, pb, pb, x2d, wo, g, b, rw, rb, before)


def _gather_rows(table, idx):
    n = idx.shape[0]
    d = table.shape[1]
    info = plsc.get_sparse_core_info()
    nw = info.num_cores * info.num_subcores
    per_w = n // nw
    assert per_w * nw == n and per_w % SC_CHUNK == 0
    n_chunks = per_w // SC_CHUNK
    mesh = plsc.VectorSubcoreMesh(core_axis_name="c", subcore_axis_name="s")

    @functools.partial(
        pl.kernel, mesh=mesh,
        out_type=jax.ShapeDtypeStruct((n, d), table.dtype),
        scratch_types=[
            pltpu.VMEM((SC_CHUNK,), I32),
            pltpu.VMEM((SC_CHUNK, d), table.dtype),
            pltpu.SemaphoreType.DMA,
        ],
    )
    def k(table_hbm, idx_hbm, out_hbm, idx_v, rows_v, sem):
        wid = lax.axis_index("s") * info.num_cores + lax.axis_index("c")
        base = wid * per_w

        @pl.loop(0, n_chunks)
        def _(j):
            off = pl.multiple_of(base + j * SC_CHUNK, 8)
            pltpu.sync_copy(idx_hbm.at[pl.ds(off, SC_CHUNK)], idx_v)
            pltpu.async_copy(table_hbm.at[idx_v], rows_v, sem).wait()
            pltpu.sync_copy(rows_v, out_hbm.at[pl.ds(off, SC_CHUNK)])

    return k(table, idx)


def _scatter_rows(src, dest_km, n_rows):
    t, d = src.shape
    info = plsc.get_sparse_core_info()
    nw = info.num_cores * info.num_subcores
    per_w = t // nw
    assert per_w * nw == t and per_w % SC_CHUNK == 0
    n_chunks = per_w // SC_CHUNK
    mesh = plsc.VectorSubcoreMesh(core_axis_name="c", subcore_axis_name="s")

    @functools.partial(
        pl.kernel, mesh=mesh,
        out_type=jax.ShapeDtypeStruct((n_rows, d), src.dtype),
        scratch_types=[
            pltpu.VMEM((SC_CHUNK,), I32),
            pltpu.VMEM((SC_CHUNK, d), src.dtype),
            pltpu.SemaphoreType.DMA,
        ],
    )
    def k(src_hbm, dest_hbm, out_hbm, idx_v, rows_v, sem):
        wid = lax.axis_index("s") * info.num_cores + lax.axis_index("c")
        base = wid * per_w

        @pl.loop(0, n_chunks)
        def _(j):
            off = pl.multiple_of(base + j * SC_CHUNK, 8)
            pltpu.sync_copy(src_hbm.at[pl.ds(off, SC_CHUNK)], rows_v)
            for kk in range(TOP_K):
                pltpu.sync_copy(dest_hbm.at[pl.ds(pl.multiple_of(kk * t + off, 8), SC_CHUNK)], idx_v)
                pltpu.async_copy(rows_v, out_hbm.at[idx_v], sem).wait()

    return k(src, dest_km)


def _expert_kernel(be_ref, nused_ref, nvalid_ref, xs_ref, wgu_ref, bgu_ref, wd_ref, bd_ref, ys_ref,
                   wgu_bf, wd_bf):
    i = pl.program_id(0)

    @pl.when(i < nused_ref[0])
    def _():
        @pl.when((i == 0) | (be_ref[i] != be_ref[jnp.maximum(i - 1, 0)]))
        def _():
            rows = 256
            for r0 in range(0, D_MODEL, rows):
                wgu_bf[r0:r0 + rows, :] = wgu_ref[0, r0:r0 + rows, :].astype(BF16)
            for r0 in range(0, D_EXPERT, rows):
                wd_bf[r0:r0 + rows, :] = wd_ref[0, r0:r0 + rows, :].astype(BF16)

        def proj(x, col):
            return _dot(x, wgu_bf[:, col:col + MOE_HC]) + bgu_ref[0, :, col:col + MOE_HC]

        rowid = lax.broadcasted_iota(I32, (MOE_BM, 1), 0)
        words = jnp.where(rowid < nvalid_ref[i], xs_ref[...], jnp.uint32(0))
        lo, hi = _unpack_bf16_pair(words)
        x = jnp.concatenate([lo.astype(BF16), hi.astype(BF16)], axis=1)
        acts = []
        for c0 in range(0, D_EXPERT, MOE_HC):
            gate = jnp.minimum(proj(x, c0), SWIGLU_LIMIT)
            up = jnp.clip(proj(x, D_EXPERT + c0), -SWIGLU_LIMIT, SWIGLU_LIMIT)
            acts.append(((up + 1.0) * gate * _sigmoid(gate * SWIGLU_ALPHA)).astype(BF16))
        y = _dot(jnp.concatenate(acts, axis=1), wd_bf[...]) + bd_ref[0]
        ys_ref[...] = _pack_bf16_pair(y[:, :HALF], y[:, HALF:])


def _experts(xs, block_expert, n_used, n_valid, wgu, bgu, wd, bd):
    n_rows = xs.shape[0]
    bm = MOE_BM
    n_blocks = n_rows // bm
    rows = lambda i, be, nu, nv: (jnp.minimum(i, nu[0] - 1), 0)
    expert = lambda i, be, nu, nv: (be[i], 0, 0)
    grid_spec = pltpu.PrefetchScalarGridSpec(
        num_scalar_prefetch=3,
        grid=(n_blocks,),
        in_specs=[
            pl.BlockSpec((bm, HALF), rows),
            pl.BlockSpec((1, D_MODEL, 2 * D_EXPERT), expert),
            pl.BlockSpec((1, 1, 2 * D_EXPERT), expert),
            pl.BlockSpec((1, D_EXPERT, D_MODEL), expert),
            pl.BlockSpec((1, 1, D_MODEL), expert),
        ],
        out_specs=pl.BlockSpec((bm, HALF), rows),
        scratch_shapes=[
            pltpu.VMEM((D_MODEL, 2 * D_EXPERT), BF16),
            pltpu.VMEM((D_EXPERT, D_MODEL), BF16),
        ],
    )
    return pl.pallas_call(
        _expert_kernel,
        grid_spec=grid_spec,
        out_shape=jax.ShapeDtypeStruct((n_rows, HALF), U32),
        compiler_params=_cparams(("arbitrary",)),
        name="experts",
    )(block_expert, n_used, n_valid, xs, wgu, bgu, wd, bd)


def _combine_kernel(xp_ref, yg_ref, gate_ref, g_ref, b_ref, o_ref):
    gates = gate_ref[...]
    f_lo = None
    f_hi = None
    for kk in range(TOP_K):
        lo, hi = _unpack_bf16_pair(yg_ref[kk])
        w = gates[:, kk:kk + 1]
        f_lo = lo * w if f_lo is None else f_lo + lo * w
        f_hi = hi * w if f_hi is None else f_hi + hi * w
    x_lo, x_hi = _unpack_bf16_pair(xp_ref[...])
    y_lo = DEEPNORM_ALPHA * x_lo + f_lo
    y_hi = DEEPNORM_ALPHA * x_hi + f_hi
    mu = (jnp.sum(y_lo, axis=-1, keepdims=True) + jnp.sum(y_hi, axis=-1, keepdims=True)) / D_MODEL
    d_lo = y_lo - mu
    d_hi = y_hi - mu
    var = (jnp.sum(d_lo * d_lo, axis=-1, keepdims=True)
           + jnp.sum(d_hi * d_hi, axis=-1, keepdims=True)) / D_MODEL
    inv = lax.rsqrt(var + LN_EPS)
    g = g_ref[...]
    b = b_ref[...]
    o_ref[:, :HALF] = d_lo * inv * g[:, :HALF] + b[:, :HALF]
    o_ref[:, HALF:] = d_hi * inv * g[:, HALF:] + b[:, HALF:]


def _combine(xp, yg, gates, g, b):
    t = xp.shape[0]
    tm = MERGE_TM
    const = lambda i: (0, 0)
    return pl.pallas_call(
        _combine_kernel,
        grid=(t // tm,),
        in_specs=[
            pl.BlockSpec((tm, HALF), lambda i: (i, 0)),
            pl.BlockSpec((TOP_K, tm, HALF), lambda i: (0, i, 0)),
            pl.BlockSpec((tm, LANES), lambda i: (i, 0)),
            pl.BlockSpec((1, D_MODEL), const),
            pl.BlockSpec((1, D_MODEL), const),
        ],
        out_specs=pl.BlockSpec((tm, D_MODEL), lambda i: (i, 0)),
        out_shape=jax.ShapeDtypeStruct((t, D_MODEL), F32),
        compiler_params=_cparams(("parallel",)),
        name="combine",
    )(xp, yg, gates, g, b)


def _rope_tables(seq):
    half = ROPE_DIM // 2
    pos = jnp.arange(seq, dtype=F32)
    inv_freq = ROPE_THETA ** (-jnp.arange(0, ROPE_DIM, 2, dtype=F32) / ROPE_DIM)
    ang = pos[:, None] * inv_freq[None, :]
    cos, sin = jnp.cos(ang), jnp.sin(ang)
    zeros = jnp.zeros((seq, LANES - ROPE_DIM), F32)
    cos_f = jnp.concatenate([cos, cos, jnp.ones((seq, LANES - ROPE_DIM), F32)], axis=1)
    sin_a = jnp.concatenate([-sin, jnp.zeros((seq, half), F32), zeros], axis=1)
    sin_b = jnp.concatenate([jnp.zeros((seq, half), F32), sin, zeros], axis=1)
    return cos_f, sin_a, sin_b


def _prep_weights(w_in, gla_gate_w2, gla_gate_b, gla_norm_w, w_o, ln1_g, ln1_b, router_w, router_b,
                  w_gate_up, b_gate_up, w_down, b_down, ln2_g, ln2_b):
    sizes = (GLA_KEY, GLA_KEY, GLA_VAL, GLA_VAL, 2 * GLA_RANK, SWA_HEADS * SWA_HD,
             SWA_KV * SWA_HD, SWA_KV * SWA_HD, D_MODEL, D_MODEL)
    cols, acc = [], 0
    for s in sizes:
        cols.append(w_in[:, acc:acc + s])
        acc += s
    gq, gk, gv, gr, glr, sq, sk, sv, ga, gb = cols
    wa = jnp.concatenate([gq, gk, gv, gr], axis=1).astype(BF16)
    wl = jnp.pad(glr, ((0, 0), (0, LANES - 2 * GLA_RANK))).astype(BF16)
    wb = jnp.concatenate([ga, gb, sq, sk, sv], axis=1).astype(BF16)
    w2f = jnp.zeros((LANES, 2 * GLA_KEY), F32)
    w2f = w2f.at[:GLA_RANK, :GLA_KEY].set(gla_gate_w2[0])
    w2f = w2f.at[GLA_RANK:2 * GLA_RANK, GLA_KEY:].set(gla_gate_w2[1]).astype(BF16)
    rw_hi, rw_lo = _split2(jnp.pad(router_w, ((0, 0), (0, LANES - N_EXPERTS))))
    rw = jnp.concatenate([rw_hi, rw_lo], axis=1)
    rb = jnp.concatenate([router_b, jnp.full((LANES - N_EXPERTS,), -jnp.inf, F32)]).reshape(1, LANES)
    return dict(
        wa=wa, wl=wl, wb=wb, w2f=w2f, bgaf=gla_gate_b.reshape(1, 2 * GLA_KEY),
        nw=gla_norm_w.reshape(1, GLA_VAL), wo=w_o.astype(BF16),
        ln1_g=ln1_g.reshape(1, D_MODEL), ln1_b=ln1_b.reshape(1, D_MODEL), rw=rw, rb=rb,
        wgu=w_gate_up, bgu=b_gate_up.reshape(N_EXPERTS, 1, 2 * D_EXPERT),
        wd=w_down, bd=b_down.reshape(N_EXPERTS, 1, D_MODEL),
        ln2_g=ln2_g.reshape(1, D_MODEL), ln2_b=ln2_b.reshape(1, D_MODEL))


def _routing_plan(route, counts):
    idx, rank = route[:TOP_K], route[TOP_K:]
    t = route.shape[1]
    bm = MOE_BM
    n_rows = t * TOP_K + N_EXPERTS * bm
    n_blocks = n_rows // bm
    counts = counts[0, :N_EXPERTS].astype(I32)
    padded = (counts + bm - 1) // bm * bm
    pad_ends = jnp.cumsum(padded)
    pad_starts = pad_ends - padded
    dest = rank
    for ex in range(N_EXPERTS):
        dest = dest + jnp.where(idx == ex, pad_starts[ex], 0)
    block_start = jnp.arange(n_blocks, dtype=I32) * bm
    block_expert = jnp.minimum(
        jnp.sum(block_start[:, None] >= pad_ends[None, :], axis=1), N_EXPERTS - 1).astype(I32)
    mine = block_expert[:, None] == jnp.arange(N_EXPERTS, dtype=I32)[None, :]
    seg_end = jnp.sum(jnp.where(mine, (pad_starts + counts)[None, :], 0), axis=1)
    n_valid = jnp.clip(seg_end - block_start, 0, bm).astype(I32)
    n_used = (pad_ends[-1] // bm).astype(I32).reshape(1)
    return dest.reshape(-1), n_rows, block_expert, n_used, n_valid


def _trunk(x, w, sinks, gather_rows, scatter_rows):
    bsz, seq, _ = x.shape
    t = bsz * seq
    x2d = x.reshape(t, D_MODEL)
    cos_f, sin_a, sin_b = _rope_tables(seq)
    pa, bdec, pb = _in_proj(x2d, w["wa"], w["wl"], w["w2f"], w["bgaf"], w["wb"],
                            cos_f, sin_a, sin_b, seq)
    oa = _gla(pa, bdec, w["nw"], bsz, seq).reshape(t, GLA_VAL)
    ob = _swa(pb, sinks, bsz, seq).reshape(t, SWA_HEADS * SWA_HD)
    xp, gates, route, counts = _merge(
        oa, ob, pb, x2d, w["wo"], w["ln1_g"], w["ln1_b"], w["rw"], w["rb"])
    dest_km, n_rows, block_expert, n_used, n_valid = _routing_plan(route, counts)
    xs = scatter_rows(xp, dest_km, n_rows)
    ys = _experts(xs, block_expert, n_used, n_valid, w["wgu"], w["bgu"], w["wd"], w["bd"])
    yg = gather_rows(ys, dest_km).reshape(TOP_K, t, HALF)
    out = _combine(xp, yg, gates, w["ln2_g"], w["ln2_b"])
    return out.reshape(bsz, seq, D_MODEL)


def _forward(x_prompt, x_sample, w_in, gla_gate_w2, gla_gate_b, gla_norm_w, swa_sinks, w_o, ln1_g,
             ln1_b, router_w, router_b, w_gate_up, b_gate_up, w_down, b_down, ln2_g, ln2_b):
    w = _prep_weights(w_in[0], gla_gate_w2[0], gla_gate_b[0], gla_norm_w[0], w_o[0], ln1_g[0],
                      ln1_b[0], router_w[0], router_b[0], w_gate_up[0], b_gate_up[0], w_down[0],
                      b_down[0], ln2_g[0], ln2_b[0])
    sinks = swa_sinks[0].astype(F32)
    return (_trunk(x_prompt, w, sinks, _gather_rows, _scatter_rows),
            _trunk(x_sample, w, sinks, _gather_rows, _scatter_rows))


def kernel(x_prompt, x_sample, w_in, gla_gate_w2, gla_gate_b, gla_norm_w, swa_sinks, w_o, ln1_g, ln1_b, router_w, router_b, w_gate_up, b_gate_up, w_down, b_down, ln2_g, ln2_b):
    return _forward(x_prompt, x_sample, w_in, gla_gate_w2, gla_gate_b, gla_norm_w, swa_sinks, w_o,
                    ln1_g, ln1_b, router_w, router_b, w_gate_up, b_gate_up, w_down, b_down,
                    ln2_g, ln2_b)
```

```python
import functools

import jax
import jax.numpy as jnp
from jax import lax
from jax.experimental import pallas as pl
from jax.experimental.pallas import tpu as pltpu
from jax.experimental.pallas import tpu_sc as plsc

F32 = jnp.float32
BF16 = jnp.bfloat16
I32 = jnp.int32
U32 = jnp.uint32

D_MODEL = 1024
GLA_HEADS = 4
GLA_DK = 128
GLA_DV = 256
GLA_KEY = GLA_HEADS * GLA_DK
GLA_VAL = GLA_HEADS * GLA_DV
GLA_RANK = 16
GLA_NORMALIZER = 16.0
GLA_CHUNK = 64
SWA_HEADS = 8
SWA_KV = 2
SWA_GROUP = SWA_HEADS // SWA_KV
SWA_HD = 128
SWA_WINDOW = 128
ROPE_THETA = 500000.0
ROPE_DIM = SWA_HD // 4
N_EXPERTS = 32
TOP_K = 4
D_EXPERT = D_MODEL
SWIGLU_LIMIT = 7.0
SWIGLU_ALPHA = 1.702
DEEPNORM_ALPHA = 2.0 ** 0.25
LN_EPS = 1e-5
RMS_EPS = 1e-6
LOG2_E = 1.4426950408889634

LANES = 128
PA_W = 2 * GLA_KEY + 2 * GLA_VAL
PB_W = 2 * D_MODEL + SWA_HEADS * SWA_HD + 2 * SWA_KV * SWA_HD
HALF = D_MODEL // 2

PROJ_TM = 512
MERGE_TM = 1024
MERGE_GROUP = 512
GLA_TS = 4096
GLA_GROUP = 512
SWA_TQ = 2048
MOE_BM = 1024
MOE_HC = 256
SC_CHUNK = 64
VMEM_LIMIT = 56 * 1024 * 1024


def _cparams(sem):
    return pltpu.CompilerParams(dimension_semantics=sem, vmem_limit_bytes=VMEM_LIMIT)


def _split2(a):
    hi = a.astype(BF16)
    lo = (a - hi.astype(F32)).astype(BF16)
    return hi, lo


def _dot(a, b):
    return jnp.dot(a, b, preferred_element_type=F32)


def _dot_nt(a, b):
    return lax.dot_general(a, b, (((1,), (1,)), ((), ())), preferred_element_type=F32)


def _sigmoid(x):
    return 0.5 * jnp.tanh(0.5 * x) + 0.5


def _pack_bf16_pair(lo_f32, hi_f32):
    lo = lax.bitcast_convert_type(lo_f32.astype(BF16).astype(F32), U32)
    hi = lax.bitcast_convert_type(hi_f32.astype(BF16).astype(F32), U32)
    return (hi & jnp.uint32(0xFFFF0000)) | (lo >> 16)


def _unpack_bf16_pair(w):
    lo = lax.bitcast_convert_type(w << 16, F32)
    hi = lax.bitcast_convert_type(w & jnp.uint32(0xFFFF0000), F32)
    return lo, hi


def _chunk_cumsum(x, reverse):
    rows = x.shape[0]
    pos = lax.broadcasted_iota(I32, x.shape, 0) & (GLA_CHUNK - 1)
    s = 1
    while s < GLA_CHUNK:
        if reverse:
            x = x + jnp.where(pos < GLA_CHUNK - s, pltpu.roll(x, rows - s, 0), 0.0)
        else:
            x = x + jnp.where(pos >= s, pltpu.roll(x, s, 0), 0.0)
        s *= 2
    return x


def _in_proj_kernel(x_ref, wa_ref, wl_ref, w2_ref, bga_ref, wb_ref, cos_ref, sina_ref, sinb_ref,
                    pa_ref, bd_ref, pb_ref):
    xb = x_ref[...].astype(BF16)
    lrb = _dot(xb, wl_ref[...]).astype(BF16)

    tm = x_ref.shape[0]
    piece_rows = 128

    def decay_piece(j, r0):
        cols = slice(j * LANES, (j + 1) * LANES)
        rows = slice(r0, r0 + piece_rows)
        z = _dot(lrb[rows], w2_ref[:, cols]) + bga_ref[:, cols]
        la = (jnp.minimum(z, 0.0) - jnp.log(1.0 + jnp.exp(-jnp.abs(z)))) * (LOG2_E / GLA_NORMALIZER)
        bd_ref[rows, cols] = _chunk_cumsum(la, reverse=j * LANES >= GLA_KEY)

    pieces = [(j, r0) for j in range(2 * GLA_KEY // LANES) for r0 in range(0, tm, piece_rows)]
    step = 256
    chunks_left = [(PA_W + PB_W) // step]

    def emit_pieces():
        for _ in range(-(-len(pieces) // chunks_left[0])):
            decay_piece(*pieces.pop(0))
        chunks_left[0] -= 1

    for c0 in range(0, PA_W, step):
        r = _dot(xb, wa_ref[:, c0:c0 + step])
        if c0 < GLA_KEY:
            r = r * (GLA_DK ** -0.5)
        pa_ref[:, c0:c0 + step] = r.astype(BF16)
        emit_pieces()
    cos = cos_ref[...]
    sina = sina_ref[...]
    sinb = sinb_ref[...]
    rope0 = 2 * D_MODEL
    rope_end = rope0 + (SWA_HEADS + SWA_KV) * SWA_HD
    for c0 in range(0, PB_W, step):
        r = _dot(xb, wb_ref[:, c0:c0 + step])
        if c0 + step > rope0 and c0 < rope_end:
            parts = []
            for j in range(step // SWA_HD):
                t = r[:, j * SWA_HD:(j + 1) * SWA_HD]
                col = c0 + j * SWA_HD
                if rope0 <= col < rope_end:
                    t = (t * cos + pltpu.roll(t, SWA_HD - ROPE_DIM // 2, 1) * sina
                         + pltpu.roll(t, ROPE_DIM // 2, 1) * sinb)
                if rope0 <= col < rope0 + SWA_HEADS * SWA_HD:
                    t = t * (SWA_HD ** -0.5 * LOG2_E)
                parts.append(t)
            r = jnp.concatenate(parts, axis=1)
        pb_ref[:, c0:c0 + step] = r.astype(BF16)
        emit_pieces()
    assert not pieces


def _in_proj(x2d, wa, wl, w2f, bgaf, wb, cos, sina, sinb, seq):
    t = x2d.shape[0]
    tm = PROJ_TM
    nseq = seq // tm
    const = lambda i: (0, 0)
    return pl.pallas_call(
        _in_proj_kernel,
        grid=(t // tm,),
        in_specs=[
            pl.BlockSpec((tm, D_MODEL), lambda i: (i, 0)),
            pl.BlockSpec((D_MODEL, PA_W), const),
            pl.BlockSpec((D_MODEL, LANES), const),
            pl.BlockSpec((LANES, 2 * GLA_KEY), const),
            pl.BlockSpec((1, 2 * GLA_KEY), const),
            pl.BlockSpec((D_MODEL, PB_W), const),
            pl.BlockSpec((tm, LANES), lambda i: (i % nseq, 0)),
            pl.BlockSpec((tm, LANES), lambda i: (i % nseq, 0)),
            pl.BlockSpec((tm, LANES), lambda i: (i % nseq, 0)),
        ],
        out_specs=[
            pl.BlockSpec((tm, PA_W), lambda i: (i, 0)),
            pl.BlockSpec((tm, 2 * GLA_KEY), lambda i: (i, 0)),
            pl.BlockSpec((tm, PB_W), lambda i: (i, 0)),
        ],
        out_shape=[
            jax.ShapeDtypeStruct((t, PA_W), BF16),
            jax.ShapeDtypeStruct((t, 2 * GLA_KEY), F32),
            jax.ShapeDtypeStruct((t, PB_W), BF16),
        ],
        compiler_params=_cparams(("parallel",)),
        name="in_proj",
    )(x2d, wa, wl, w2f, bgaf, wb, cos, sina, sinb)


def _gla_kernel(q_ref, k_ref, v_ref, r_ref, b_ref, nw_ref, o_ref,
                state_ref, ofwd_ref, *group_scratch, ts, nt):
    ph = pl.program_id(2)
    t = pl.program_id(3)
    c = GLA_CHUNK
    gs = min(GLA_GROUP, ts)
    n_groups = ts // gs
    nc = gs // c

    @pl.when(t == 0)
    def _():
        state_ref[...] = jnp.zeros_like(state_ref)

    row = lax.broadcasted_iota(I32, (c, c), 0)
    col = lax.broadcasted_iota(I32, (c, c), 1)

    def run(forward):
        keep = (col <= row) if forward else (col > row)
        mid = c // 2 if forward else c // 2 - 1
        end = c - 1 if forward else 0
        off = pl.multiple_of((t if forward else nt - 1 - t) * ts, ts)
        group_order = range(n_groups) if forward else reversed(range(n_groups))
        chunk_order = list(range(nc)) if forward else list(reversed(range(nc)))
        st = state_ref[...]
        for g in group_order:
            sall_ref, ds_ref, dec_ref = group_scratch[3 * g:3 * g + 3]
            rows = slice(g * gs, (g + 1) * gs)
            q3 = q_ref[0, rows, :].reshape(nc, c, GLA_DK)
            k3 = k_ref[0, rows, :].reshape(nc, c, GLA_DK)
            v3 = v_ref[0, rows, :].reshape(nc, c, GLA_DV)
            b3 = b_ref[0, rows, :].reshape(nc, c, GLA_DK)
            b_mid = b3[:, mid:mid + 1]
            b_end = b3[:, end:end + 1]
            q_in = q3 * jnp.exp2(b3 - b_mid).astype(BF16)
            k_in = k3 * jnp.exp2(b_mid - b3).astype(BF16)
            k_st = k_in * jnp.exp2(b_end - b_mid).astype(BF16)
            q_b = q_in * jnp.exp2(b_mid).astype(BF16)
            dec_ref[...] = jnp.exp2(b_end)
            scores = jnp.einsum("cid,cjd->cij", q_in, k_in, preferred_element_type=F32)
            scores = jnp.where(keep[None], scores, 0.0).astype(BF16)
            o_intra = jnp.einsum("cij,cjv->civ", scores, v3, preferred_element_type=F32)
            ds_ref[...] = jnp.einsum("cjv,cjd->cvd", v3, k_st, preferred_element_type=F32)
            for ci in chunk_order:
                sall_ref[ci] = st.astype(BF16)
                st = st * dec_ref[ci] + ds_ref[ci]
            o_inter = jnp.einsum("cid,cvd->civ", q_b, sall_ref[...], preferred_element_type=F32)
            o = (o_intra + o_inter).reshape(gs, GLA_DV)
            orows = pl.ds(pl.multiple_of(off + g * gs, gs), gs)
            if forward:
                ofwd_ref[orows, :] = o
            else:
                tot = o + ofwd_ref[orows, :]
                tot = tot * lax.rsqrt(jnp.mean(tot * tot, axis=-1, keepdims=True) + RMS_EPS)
                r = r_ref[0, rows, :]
                o_ref[0, rows, :] = (
                    tot * nw_ref[...] * (r * _sigmoid(r)).astype(F32)).astype(BF16)
        state_ref[...] = st

    @pl.when(ph == 0)
    def _():
        run(True)

    @pl.when(ph == 1)
    def _():
        run(False)


def _gla(pa, bdec, nw, bsz, seq):
    ts = min(GLA_TS, seq)
    nt = seq // ts
    gs = min(GLA_GROUP, ts)
    nc = gs // GLA_CHUNK
    group_scratch = []
    for _ in range(ts // gs):
        group_scratch += [pltpu.VMEM((nc, GLA_DV, GLA_DK), BF16),
                          pltpu.VMEM((nc, GLA_DV, GLA_DK), F32),
                          pltpu.VMEM((nc, 1, GLA_DK), F32)]
    pa3 = pa.reshape(bsz, seq, PA_W)
    bd3 = bdec.reshape(bsz, seq, 2 * GLA_KEY)
    tile = lambda p, t: t + p * (nt - 1 - 2 * t)
    wk, wv = GLA_DK, GLA_DV
    kq = GLA_KEY // wk
    kv = 2 * GLA_KEY // wv
    kr = kv + GLA_VAL // wv
    kern = functools.partial(_gla_kernel, ts=ts, nt=nt)
    return pl.pallas_call(
        kern,
        grid=(bsz, GLA_HEADS, 2, nt),
        in_specs=[
            pl.BlockSpec((1, ts, wk), lambda b, h, p, t: (b, tile(p, t), h)),
            pl.BlockSpec((1, ts, wk), lambda b, h, p, t: (b, tile(p, t), kq + h)),
            pl.BlockSpec((1, ts, wv), lambda b, h, p, t: (b, tile(p, t), kv + h)),
            pl.BlockSpec((1, ts, wv), lambda b, h, p, t: (b, nt - 1 - p * t, kr + h)),
            pl.BlockSpec((1, ts, wk), lambda b, h, p, t: (b, tile(p, t), p * kq + h)),
            pl.BlockSpec((1, wv), lambda b, h, p, t: (0, h)),
        ],
        out_specs=pl.BlockSpec((1, ts, wv), lambda b, h, p, t: (b, nt - 1 - p * t, h)),
        out_shape=jax.ShapeDtypeStruct((bsz, seq, GLA_VAL), BF16),
        scratch_shapes=[
            pltpu.VMEM((GLA_DV, GLA_DK), F32),
            pltpu.VMEM((seq, GLA_DV), F32),
        ] + group_scratch,
        compiler_params=_cparams(("parallel", "parallel", "arbitrary", "arbitrary")),
        name="gla",
    )(pa3, pa3, pa3, pa3, bd3, nw)


def _swa_kernel(sink_ref, q_ref, kl_ref, kc_ref, kr_ref, vl_ref, vc_ref, vr_ref, o_ref, *, tq):
    kvh = pl.program_id(1)
    n = pl.program_id(2)
    blk = SWA_WINDOW
    k_all = jnp.concatenate([kl_ref[0], kc_ref[0], kr_ref[0]], axis=0)
    v_all = jnp.concatenate([vl_ref[0], vc_ref[0], vr_ref[0]], axis=0)
    rows = SWA_GROUP * blk
    qi = lax.broadcasted_iota(I32, (rows, 3 * blk), 0) & (blk - 1)
    kj = lax.broadcasted_iota(I32, (rows, 3 * blk), 1) - blk
    band = jnp.where(jnp.abs(kj - qi) <= SWA_WINDOW, 0.0, -jnp.inf)
    head = lax.broadcasted_iota(I32, (rows, 1), 0) // blk
    sink = jnp.zeros((rows, 1), F32)
    for g in range(SWA_GROUP):
        sink = jnp.where(head == g, sink_ref[kvh * SWA_GROUP + g] * LOG2_E, sink)
    n_sub = tq // blk
    before = jnp.where(n == 0, -jnp.inf, 0.0)
    after = jnp.where(n == pl.num_programs(2) - 1, -jnp.inf, 0.0)
    for jq in range(n_sub):
        bias = band
        if jq == 0:
            bias = bias + jnp.where(kj < 0, before, 0.0)
        if jq == n_sub - 1:
            bias = bias + jnp.where(kj >= blk, after, 0.0)
        qs = q_ref[0, jq * blk:(jq + 1) * blk, :]
        q4 = jnp.concatenate([qs[:, g * SWA_HD:(g + 1) * SWA_HD] for g in range(SWA_GROUP)], axis=0)
        keys = k_all[jq * blk:(jq + 3) * blk]
        vals = v_all[jq * blk:(jq + 3) * blk]
        s = _dot_nt(q4, keys) + bias
        m = jnp.maximum(jnp.max(s, axis=-1, keepdims=True), sink)
        p = jnp.exp2(s - m)
        denom = jnp.sum(p, axis=-1, keepdims=True) + jnp.exp2(sink - m)
        o = _dot(p.astype(BF16), vals) * (1.0 / denom)
        for g in range(SWA_GROUP):
            o_ref[0, jq * blk:(jq + 1) * blk, g * SWA_HD:(g + 1) * SWA_HD] = (
                o[g * blk:(g + 1) * blk].astype(BF16))


def _swa(pb, sinks, bsz, seq):
    tq = min(SWA_TQ, seq)
    blk = SWA_WINDOW
    r = tq // blk
    nb = seq // blk
    pb3 = pb.reshape(bsz, seq, PB_W)
    gw = SWA_GROUP * SWA_HD
    cq = 2 * D_MODEL // gw
    ck = (2 * D_MODEL + SWA_HEADS * SWA_HD) // SWA_HD
    cv = ck + SWA_KV
    left = lambda n: jnp.maximum(n * r - 1, 0)
    right = lambda n: jnp.minimum(n * r + r, nb - 1)
    kern = functools.partial(_swa_kernel, tq=tq)
    grid_spec = pltpu.PrefetchScalarGridSpec(
        num_scalar_prefetch=1,
        grid=(bsz, SWA_KV, seq // tq),
        in_specs=[
            pl.BlockSpec((1, tq, gw), lambda b, h, n, s: (b, n, cq + h)),
            pl.BlockSpec((1, blk, SWA_HD), lambda b, h, n, s: (b, left(n), ck + h)),
            pl.BlockSpec((1, tq, SWA_HD), lambda b, h, n, s: (b, n, ck + h)),
            pl.BlockSpec((1, blk, SWA_HD), lambda b, h, n, s: (b, right(n), ck + h)),
            pl.BlockSpec((1, blk, SWA_HD), lambda b, h, n, s: (b, left(n), cv + h)),
            pl.BlockSpec((1, tq, SWA_HD), lambda b, h, n, s: (b, n, cv + h)),
            pl.BlockSpec((1, blk, SWA_HD), lambda b, h, n, s: (b, right(n), cv + h)),
        ],
        out_specs=pl.BlockSpec((1, tq, gw), lambda b, h, n, s: (b, n, h)),
    )
    return pl.pallas_call(
        kern,
        grid_spec=grid_spec,
        out_shape=jax.ShapeDtypeStruct((bsz, seq, SWA_HEADS * SWA_HD), BF16),
        compiler_params=_cparams(("parallel", "parallel", "parallel")),
        name="swa",
    )(sinks, pb3, pb3, pb3, pb3, pb3, pb3, pb3)


def _layer_norm(y, g, b):
    mu = jnp.mean(y, axis=-1, keepdims=True)
    d = y - mu
    var = jnp.mean(d * d, axis=-1, keepdims=True)
    return d * lax.rsqrt(var + LN_EPS) * g + b


def _merge_kernel(oa_ref, ob_ref, ga_ref, gb_ref, x_ref, wo_ref, g_ref, b_ref, rw_ref, rb_ref,
                  before_ref, xp_ref, gate_ref, route_ref, cnt_ref, run_ref, *, tm):
    @pl.when(pl.program_id(0) == 0)
    def _():
        run_ref[...] = jnp.zeros_like(run_ref)

    gm = MERGE_GROUP
    lane_i = lax.broadcasted_iota(I32, (gm, LANES), 1)
    lane = lane_i.astype(F32)

    def spread(cols):
        out = jnp.zeros((gm, LANES), F32)
        for kk, col in enumerate(cols):
            out = jnp.where(lane_i == kk, col, out)
        return out

    run = run_ref[...]
    for r0 in range(0, tm, gm):
        rows = slice(r0, r0 + gm)
        h = (_sigmoid(ga_ref[rows, :]) * oa_ref[rows, :]
             + _sigmoid(gb_ref[rows, :]) * ob_ref[rows, :])
        y = DEEPNORM_ALPHA * x_ref[rows, :] + _dot(h, wo_ref[...])
        x1 = _layer_norm(y, g_ref[...], b_ref[...])
        xp_ref[rows, :] = _pack_bf16_pair(x1[:, :HALF], x1[:, HALF:])

        x_hi, x_lo = _split2(x1)
        both = _dot(x_hi, rw_ref[...])
        logits = (both[:, :LANES] + both[:, LANES:] + _dot(x_lo, rw_ref[:, :LANES])
                  + rb_ref[...])
        work = logits
        vals, idxs, hots = [], [], []
        for _ in range(TOP_K):
            m = jnp.max(work, axis=-1, keepdims=True)
            sel = jnp.min(jnp.where(work == m, lane, float(LANES)), axis=-1, keepdims=True)
            hot = lane == sel
            vals.append(m)
            idxs.append(sel)
            hots.append(hot)
            work = jnp.where(hot, -jnp.inf, work)
        e = [jnp.exp(v - vals[0]) for v in vals]
        tot = e[0] + e[1] + e[2] + e[3]
        gate_ref[rows, :] = spread([ei / tot for ei in e])

        member = (hots[0] | hots[1] | hots[2] | hots[3])
        cum = _dot(before_ref[...], member.astype(BF16)) + run
        ranks = [jnp.sum(jnp.where(hot, cum, 0.0), axis=-1, keepdims=True) for hot in hots]
        route = spread(idxs + ranks).T
        route_ref[:, rows] = route[:2 * TOP_K].astype(I32)
        run = run + jnp.sum(member.astype(F32), axis=0, keepdims=True)
    run_ref[...] = run
    cnt_ref[...] = run


def _merge(oa, ob, pb, x2d, wo, g, b, rw, rb):
    t = x2d.shape[0]
    tm = MERGE_TM
    const = lambda i: (0, 0)
    kern = functools.partial(_merge_kernel, tm=tm)
    gm = MERGE_GROUP
    before = (jnp.arange(gm)[None, :] < jnp.arange(gm)[:, None]).astype(BF16)
    return pl.pallas_call(
        kern,
        grid=(t // tm,),
        in_specs=[
            pl.BlockSpec((tm, D_MODEL), lambda i: (i, 0)),
            pl.BlockSpec((tm, D_MODEL), lambda i: (i, 0)),
            pl.BlockSpec((tm, D_MODEL), lambda i: (i, 0)),
            pl.BlockSpec((tm, D_MODEL), lambda i: (i, 1)),
            pl.BlockSpec((tm, D_MODEL), lambda i: (i, 0)),
            pl.BlockSpec((D_MODEL, D_MODEL), const),
            pl.BlockSpec((1, D_MODEL), const),
            pl.BlockSpec((1, D_MODEL), const),
            pl.BlockSpec((D_MODEL, 2 * LANES), const),
            pl.BlockSpec((1, LANES), const),
            pl.BlockSpec((gm, gm), const),
        ],
        out_specs=[
            pl.BlockSpec((tm, HALF), lambda i: (i, 0)),
            pl.BlockSpec((tm, LANES), lambda i: (i, 0)),
            pl.BlockSpec((2 * TOP_K, tm), lambda i: (0, i)),
            pl.BlockSpec((1, LANES), const),
        ],
        out_shape=[
            jax.ShapeDtypeStruct((t, HALF), U32),
            jax.ShapeDtypeStruct((t, LANES), F32),
            jax.ShapeDtypeStruct((2 * TOP_K, t), I32),
            jax.ShapeDtypeStruct((1, LANES), F32),
        ],
        scratch_shapes=[pltpu.VMEM((1, LANES), F32)],
        compiler_params=_cparams(("arbitrary",)),
        name="merge",
    )(oa, ob, pb, pb, x2d, wo, g, b, rw, rb, before)


def _gather_rows(table, idx):
    n = idx.shape[0]
    d = table.shape[1]
    info = plsc.get_sparse_core_info()
    nw = info.num_cores * info.num_subcores
    per_w = n // nw
    assert per_w * nw == n and per_w % SC_CHUNK == 0
    n_chunks = per_w // SC_CHUNK
    mesh = plsc.VectorSubcoreMesh(core_axis_name="c", subcore_axis_name="s")

    @functools.partial(
        pl.kernel, mesh=mesh,
        out_type=jax.ShapeDtypeStruct((n, d), table.dtype),
        scratch_types=[
            pltpu.VMEM((SC_CHUNK,), I32),
            pltpu.VMEM((SC_CHUNK, d), table.dtype),
            pltpu.SemaphoreType.DMA,
        ],
    )
    def k(table_hbm, idx_hbm, out_hbm, idx_v, rows_v, sem):
        wid = lax.axis_index("s") * info.num_cores + lax.axis_index("c")
        base = wid * per_w

        @pl.loop(0, n_chunks)
        def _(j):
            off = pl.multiple_of(base + j * SC_CHUNK, 8)
            pltpu.sync_copy(idx_hbm.at[pl.ds(off, SC_CHUNK)], idx_v)
            pltpu.async_copy(table_hbm.at[idx_v], rows_v, sem).wait()
            pltpu.sync_copy(rows_v, out_hbm.at[pl.ds(off, SC_CHUNK)])

    return k(table, idx)


def _scatter_rows(src, dest_km, n_rows):
    t, d = src.shape
    info = plsc.get_sparse_core_info()
    nw = info.num_cores * info.num_subcores
    per_w = t // nw
    assert per_w * nw == t and per_w % SC_CHUNK == 0
    n_chunks = per_w // SC_CHUNK
    mesh = plsc.VectorSubcoreMesh(core_axis_name="c", subcore_axis_name="s")

    @functools.partial(
        pl.kernel, mesh=mesh,
        out_type=jax.ShapeDtypeStruct((n_rows, d), src.dtype),
        scratch_types=[
            pltpu.VMEM((SC_CHUNK,), I32),
            pltpu.VMEM((SC_CHUNK, d), src.dtype),
            pltpu.SemaphoreType.DMA,
        ],
    )
    def k(src_hbm, dest_hbm, out_hbm, idx_v, rows_v, sem):
        wid = lax.axis_index("s") * info.num_cores + lax.axis_index("c")
        base = wid * per_w

        @pl.loop(0, n_chunks)
        def _(j):
            off = pl.multiple_of(base + j * SC_CHUNK, 8)
            pltpu.sync_copy(src_hbm.at[pl.ds(off, SC_CHUNK)], rows_v)
            for kk in range(TOP_K):
                pltpu.sync_copy(dest_hbm.at[pl.ds(pl.multiple_of(kk * t + off, 8), SC_CHUNK)], idx_v)
                pltpu.async_copy(rows_v, out_hbm.at[idx_v], sem).wait()

    return k(src, dest_km)


def _expert_kernel(be_ref, nused_ref, nvalid_ref, xs_ref, wgu_ref, bgu_ref, wd_ref, bd_ref, ys_ref,
                   wgu_bf, wd_bf):
    i = pl.program_id(0)

    @pl.when(i < nused_ref[0])
    def _():
        @pl.when((i == 0) | (be_ref[i] != be_ref[jnp.maximum(i - 1, 0)]))
        def _():
            rows = 256
            for r0 in range(0, D_MODEL, rows):
                wgu_bf[r0:r0 + rows, :] = wgu_ref[0, r0:r0 + rows, :].astype(BF16)
            for r0 in range(0, D_EXPERT, rows):
                wd_bf[r0:r0 + rows, :] = wd_ref[0, r0:r0 + rows, :].astype(BF16)

        def proj(x, col):
            return _dot(x, wgu_bf[:, col:col + MOE_HC]) + bgu_ref[0, :, col:col + MOE_HC]

        rowid = lax.broadcasted_iota(I32, (MOE_BM, 1), 0)
        words = jnp.where(rowid < nvalid_ref[i], xs_ref[...], jnp.uint32(0))
        lo, hi = _unpack_bf16_pair(words)
        x = jnp.concatenate([lo.astype(BF16), hi.astype(BF16)], axis=1)
        acts = []
        for c0 in range(0, D_EXPERT, MOE_HC):
            gate = jnp.minimum(proj(x, c0), SWIGLU_LIMIT)
            up = jnp.clip(proj(x, D_EXPERT + c0), -SWIGLU_LIMIT, SWIGLU_LIMIT)
            acts.append(((up + 1.0) * gate * _sigmoid(gate * SWIGLU_ALPHA)).astype(BF16))
        y = _dot(jnp.concatenate(acts, axis=1), wd_bf[...]) + bd_ref[0]
        ys_ref[...] = _pack_bf16_pair(y[:, :HALF], y[:, HALF:])


def _experts(xs, block_expert, n_used, n_valid, wgu, bgu, wd, bd):
    n_rows = xs.shape[0]
    bm = MOE_BM
    n_blocks = n_rows // bm
    rows = lambda i, be, nu, nv: (jnp.minimum(i, nu[0] - 1), 0)
    expert = lambda i, be, nu, nv: (be[i], 0, 0)
    grid_spec = pltpu.PrefetchScalarGridSpec(
        num_scalar_prefetch=3,
        grid=(n_blocks,),
        in_specs=[
            pl.BlockSpec((bm, HALF), rows),
            pl.BlockSpec((1, D_MODEL, 2 * D_EXPERT), expert),
            pl.BlockSpec((1, 1, 2 * D_EXPERT), expert),
            pl.BlockSpec((1, D_EXPERT, D_MODEL), expert),
            pl.BlockSpec((1, 1, D_MODEL), expert),
        ],
        out_specs=pl.BlockSpec((bm, HALF), rows),
        scratch_shapes=[
            pltpu.VMEM((D_MODEL, 2 * D_EXPERT), BF16),
            pltpu.VMEM((D_EXPERT, D_MODEL), BF16),
        ],
    )
    return pl.pallas_call(
        _expert_kernel,
        grid_spec=grid_spec,
        out_shape=jax.ShapeDtypeStruct((n_rows, HALF), U32),
        compiler_params=_cparams(("arbitrary",)),
        name="experts",
    )(block_expert, n_used, n_valid, xs, wgu, bgu, wd, bd)


def _combine_kernel(xp_ref, yg_ref, gate_ref, g_ref, b_ref, o_ref):
    gates = gate_ref[...]
    f_lo = None
    f_hi = None
    for kk in range(TOP_K):
        lo, hi = _unpack_bf16_pair(yg_ref[kk])
        w = gates[:, kk:kk + 1]
        f_lo = lo * w if f_lo is None else f_lo + lo * w
        f_hi = hi * w if f_hi is None else f_hi + hi * w
    x_lo, x_hi = _unpack_bf16_pair(xp_ref[...])
    y_lo = DEEPNORM_ALPHA * x_lo + f_lo
    y_hi = DEEPNORM_ALPHA * x_hi + f_hi
    mu = (jnp.sum(y_lo, axis=-1, keepdims=True) + jnp.sum(y_hi, axis=-1, keepdims=True)) / D_MODEL
    d_lo = y_lo - mu
    d_hi = y_hi - mu
    var = (jnp.sum(d_lo * d_lo, axis=-1, keepdims=True)
           + jnp.sum(d_hi * d_hi, axis=-1, keepdims=True)) / D_MODEL
    inv = lax.rsqrt(var + LN_EPS)
    g = g_ref[...]
    b = b_ref[...]
    o_ref[:, :HALF] = d_lo * inv * g[:, :HALF] + b[:, :HALF]
    o_ref[:, HALF:] = d_hi * inv * g[:, HALF:] + b[:, HALF:]


def _combine(xp, yg, gates, g, b):
    t = xp.shape[0]
    tm = MERGE_TM
    const = lambda i: (0, 0)
    return pl.pallas_call(
        _combine_kernel,
        grid=(t // tm,),
        in_specs=[
            pl.BlockSpec((tm, HALF), lambda i: (i, 0)),
            pl.BlockSpec((TOP_K, tm, HALF), lambda i: (0, i, 0)),
            pl.BlockSpec((tm, LANES), lambda i: (i, 0)),
            pl.BlockSpec((1, D_MODEL), const),
            pl.BlockSpec((1, D_MODEL), const),
        ],
        out_specs=pl.BlockSpec((tm, D_MODEL), lambda i: (i, 0)),
        out_shape=jax.ShapeDtypeStruct((t, D_MODEL), F32),
        compiler_params=_cparams(("parallel",)),
        name="combine",
    )(xp, yg, gates, g, b)


def _rope_tables(seq):
    half = ROPE_DIM // 2
    pos = jnp.arange(seq, dtype=F32)
    inv_freq = ROPE_THETA ** (-jnp.arange(0, ROPE_DIM, 2, dtype=F32) / ROPE_DIM)
    ang = pos[:, None] * inv_freq[None, :]
    cos, sin = jnp.cos(ang), jnp.sin(ang)
    zeros = jnp.zeros((seq, LANES - ROPE_DIM), F32)
    cos_f = jnp.concatenate([cos, cos, jnp.ones((seq, LANES - ROPE_DIM), F32)], axis=1)
    sin_a = jnp.concatenate([-sin, jnp.zeros((seq, half), F32), zeros], axis=1)
    sin_b = jnp.concatenate([jnp.zeros((seq, half), F32), sin, zeros], axis=1)
    return cos_f, sin_a, sin_b


def _prep_weights(w_in, gla_gate_w2, gla_gate_b, gla_norm_w, w_o, ln1_g, ln1_b, router_w, router_b,
                  w_gate_up, b_gate_up, w_down, b_down, ln2_g, ln2_b):
    sizes = (GLA_KEY, GLA_KEY, GLA_VAL, GLA_VAL, 2 * GLA_RANK, SWA_HEADS * SWA_HD,
             SWA_KV * SWA_HD, SWA_KV * SWA_HD, D_MODEL, D_MODEL)
    cols, acc = [], 0
    for s in sizes:
        cols.append(w_in[:, acc:acc + s])
        acc += s
    gq, gk, gv, gr, glr, sq, sk, sv, ga, gb = cols
    wa = jnp.concatenate([gq, gk, gv, gr], axis=1).astype(BF16)
    wl = jnp.pad(glr, ((0, 0), (0, LANES - 2 * GLA_RANK))).astype(BF16)
    wb = jnp.concatenate([ga, gb, sq, sk, sv], axis=1).astype(BF16)
    w2f = jnp.zeros((LANES, 2 * GLA_KEY), F32)
    w2f = w2f.at[:GLA_RANK, :GLA_KEY].set(gla_gate_w2[0])
    w2f = w2f.at[GLA_RANK:2 * GLA_RANK, GLA_KEY:].set(gla_gate_w2[1]).astype(BF16)
    rw_hi, rw_lo = _split2(jnp.pad(router_w, ((0, 0), (0, LANES - N_EXPERTS))))
    rw = jnp.concatenate([rw_hi, rw_lo], axis=1)
    rb = jnp.concatenate([router_b, jnp.full((LANES - N_EXPERTS,), -jnp.inf, F32)]).reshape(1, LANES)
    return dict(
        wa=wa, wl=wl, wb=wb, w2f=w2f, bgaf=gla_gate_b.reshape(1, 2 * GLA_KEY),
        nw=gla_norm_w.reshape(1, GLA_VAL), wo=w_o.astype(BF16),
        ln1_g=ln1_g.reshape(1, D_MODEL), ln1_b=ln1_b.reshape(1, D_MODEL), rw=rw, rb=rb,
        wgu=w_gate_up, bgu=b_gate_up.reshape(N_EXPERTS, 1, 2 * D_EXPERT),
        wd=w_down, bd=b_down.reshape(N_EXPERTS, 1, D_MODEL),
        ln2_g=ln2_g.reshape(1, D_MODEL), ln2_b=ln2_b.reshape(1, D_MODEL))


def _routing_plan(route, counts):
    idx, rank = route[:TOP_K], route[TOP_K:]
    t = route.shape[1]
    bm = MOE_BM
    n_rows = t * TOP_K + N_EXPERTS * bm
    n_blocks = n_rows // bm
    counts = counts[0, :N_EXPERTS].astype(I32)
    padded = (counts + bm - 1) // bm * bm
    pad_ends = jnp.cumsum(padded)
    pad_starts = pad_ends - padded
    dest = rank
    for ex in range(N_EXPERTS):
        dest = dest + jnp.where(idx == ex, pad_starts[ex], 0)
    block_start = jnp.arange(n_blocks, dtype=I32) * bm
    block_expert = jnp.minimum(
        jnp.sum(block_start[:, None] >= pad_ends[None, :], axis=1), N_EXPERTS - 1).astype(I32)
    mine = block_expert[:, None] == jnp.arange(N_EXPERTS, dtype=I32)[None, :]
    seg_end = jnp.sum(jnp.where(mine, (pad_starts + counts)[None, :], 0), axis=1)
    n_valid = jnp.clip(seg_end - block_start, 0, bm).astype(I32)
    n_used = (pad_ends[-1] // bm).astype(I32).reshape(1)
    return dest.reshape(-1), n_rows, block_expert, n_used, n_valid


def _trunk(x, w, sinks, gather_rows, scatter_rows):
    bsz, seq, _ = x.shape
    t = bsz * seq
    x2d = x.reshape(t, D_MODEL)
    cos_f, sin_a, sin_b = _rope_tables(seq)
    pa, bdec, pb = _in_proj(x2d, w["wa"], w["wl"], w["w2f"], w["bgaf"], w["wb"],
                            cos_f, sin_a, sin_b, seq)
    oa = _gla(pa, bdec, w["nw"], bsz, seq).reshape(t, GLA_VAL)
    ob = _swa(pb, sinks, bsz, seq).reshape(t, SWA_HEADS * SWA_HD)
    xp, gates, route, counts = _merge(
        oa, ob, pb, x2d, w["wo"], w["ln1_g"], w["ln1_b"], w["rw"], w["rb"])
    dest_km, n_rows, block_expert, n_used, n_valid = _routing_plan(route, counts)
    xs = scatter_rows(xp, dest_km, n_rows)
    ys = _experts(xs, block_expert, n_used, n_valid, w["wgu"], w["bgu"], w["wd"], w["bd"])
    yg = gather_rows(ys, dest_km).reshape(TOP_K, t, HALF)
    out = _combine(xp, yg, gates, w["ln2_g"], w["ln2_b"])
    return out.reshape(bsz, seq, D_MODEL)


def _forward(x_prompt, x_sample, w_in, gla_gate_w2, gla_gate_b, gla_norm_w, swa_sinks, w_o, ln1_g,
             ln1_b, router_w, router_b, w_gate_up, b_gate_up, w_down, b_down, ln2_g, ln2_b):
    w = _prep_weights(w_in[0], gla_gate_w2[0], gla_gate_b[0], gla_norm_w[0], w_o[0], ln1_g[0],
                      ln1_b[0], router_w[0], router_b[0], w_gate_up[0], b_gate_up[0], w_down[0],
                      b_down[0], ln2_g[0], ln2_b[0])
    sinks = swa_sinks[0].astype(F32)
    return (_trunk(x_prompt, w, sinks, _gather_rows, _scatter_rows),
            _trunk(x_sample, w, sinks, _gather_rows, _scatter_rows))


def kernel(x_prompt, x_sample, w_in, gla_gate_w2, gla_gate_b, gla_norm_w, swa_sinks, w_o, ln1_g, ln1_b, router_w, router_b, w_gate_up, b_gate_up, w_down, b_down, ln2_g, ln2_b):
    return _forward(x_prompt, x_sample, w_in, gla_gate_w2, gla_gate_b, gla_norm_w, swa_sinks, w_o,
                    ln1_g, ln1_b, router_w, router_b, w_gate_up, b_gate_up, w_down, b_down,
                    ln2_g, ln2_b)
```
